```python
import math, functools
import jax, jax.numpy as jnp
from jax import lax
import numpy as np

D_MODEL = 2048
BATCH = 2
SEQ = 4096
DEPTH = 1
DEC_BATCH = 32
DEC_SEQ = 8
PAST_LEN = 16384
PAGE_SIZE = 128

H_A = 8
HD_A = 128
W_A = H_A * HD_A
MOBA_BLOCK = 256
MOBA_TOPK = 3
MOBA_QBLOCK = 64
N_BUCKETS = 32
MAX_DISTANCE = 4096
H_B = 8
DK_B = 128
DV_B = 128
WK_B = H_B * DK_B
WV_B = H_B * DV_B
QKV_B = 2 * WK_B + WV_B
CONV_WIDTH = 4
GDN_CHUNK = 64
N_EXPERTS = 32
TOP_K = 4
D_EXPERT = 2048
SWIGLU_LIMIT = 7.0
SWIGLU_ALPHA = 1.702
MOE_BLOCK = 128
RMS_EPS = 1e-6
NEG_INF = -1e30
IN_SPLITS = (W_A, 2 * W_A, 3 * W_A, 3 * W_A + QKV_B, 3 * W_A + QKV_B + WV_B,
             3 * W_A + QKV_B + WV_B + H_B, 3 * W_A + QKV_B + WV_B + 2 * H_B,
             3 * W_A + QKV_B + WV_B + 2 * H_B + D_MODEL)
IN_COLS = 3 * W_A + QKV_B + WV_B + 2 * H_B + 2 * D_MODEL

kernel_name = 'hybrid_moba_gdn_moe_step'


def rms_norm(x, g):
    xf = x.astype(jnp.float32)
    y = xf * lax.rsqrt(jnp.mean(xf * xf, axis=-1, keepdims=True) + RMS_EPS)
    return (y * g.astype(jnp.float32)).astype(x.dtype)


def l2_normalize(x):
    return x * lax.rsqrt(jnp.sum(x * x, axis=-1, keepdims=True) + RMS_EPS)


def masked_softmax(logits, mask):
    return jax.nn.softmax(jnp.where(mask, logits, NEG_INF), axis=-1)


def rel_pos_bias(rel_bias, h_idx, q_pos, k_pos):
    n = jnp.maximum(q_pos - k_pos, 0)
    max_exact = N_BUCKETS // 2
    nf = jnp.maximum(n, max_exact).astype(jnp.float32)
    large = max_exact + (jnp.log(nf / max_exact) / math.log(MAX_DISTANCE / max_exact)
                         * (N_BUCKETS - max_exact)).astype(jnp.int32)
    bucket = jnp.where(n < max_exact, n, jnp.minimum(large, N_BUCKETS - 1))
    return rel_bias[h_idx, bucket].astype(jnp.float32)


def moba_prompt(q, k, v, rel_bias):
    n_seq, L = q.shape[:2]
    nb = -(-L // MOBA_BLOCK)
    pad = nb * MOBA_BLOCK - L
    qt = (q * HD_A ** -0.5).transpose(0, 2, 1, 3)
    kt = jnp.pad(k, ((0, 0), (0, pad), (0, 0), (0, 0))).transpose(0, 2, 1, 3)
    vt = jnp.pad(v, ((0, 0), (0, pad), (0, 0), (0, 0))).transpose(0, 2, 1, 3)
    kb = kt.reshape(n_seq, H_A, nb, MOBA_BLOCK, HD_A)
    vb = vt.reshape(n_seq, H_A, nb, MOBA_BLOCK, HD_A)
    k_mean = jnp.mean(kb.astype(jnp.float32), axis=3)
    n_sel = min(MOBA_TOPK, nb)
    ns = n_sel * MOBA_BLOCK
    b_idx = jnp.arange(n_seq)[:, None, None, None]
    h_idx = jnp.arange(H_A)[None, :, None, None]
    blk = jnp.arange(nb)
    offs = jnp.arange(MOBA_BLOCK)

    def query_block(c):
        q0 = c * MOBA_QBLOCK
        qc = lax.dynamic_slice_in_dim(qt, q0, MOBA_QBLOCK, axis=2)
        q_pos = (q0 + jnp.arange(MOBA_QBLOCK))[:, None]
        own = q0 // MOBA_BLOCK
        scores = jnp.einsum('bhqd,bhnd->bhqn', qc.astype(jnp.float32), k_mean)
        _, sel = lax.top_k(jnp.where(blk < own, scores, NEG_INF), n_sel)
        k_sel = kb[b_idx, h_idx, sel].reshape(n_seq, H_A, MOBA_QBLOCK, ns, HD_A)
        v_sel = vb[b_idx, h_idx, sel].reshape(n_seq, H_A, MOBA_QBLOCK, ns, HD_A)
        sel_pos = (sel[..., None] * MOBA_BLOCK + offs).reshape(n_seq, H_A, MOBA_QBLOCK, ns)
        sel_mask = jnp.broadcast_to((sel < own)[..., None], sel.shape + (MOBA_BLOCK,)).reshape(sel_pos.shape)
        sel_logits = (jnp.einsum('bhqd,bhqkd->bhqk', qc, k_sel, preferred_element_type=jnp.float32)
                      + rel_pos_bias(rel_bias, h_idx, q_pos, sel_pos))
        k_own = lax.dynamic_slice_in_dim(kt, own * MOBA_BLOCK, MOBA_BLOCK, axis=2)
        v_own = lax.dynamic_slice_in_dim(vt, own * MOBA_BLOCK, MOBA_BLOCK, axis=2)
        own_pos = own * MOBA_BLOCK + offs
        own_logits = (jnp.einsum('bhqd,bhkd->bhqk', qc, k_own, preferred_element_type=jnp.float32)
                      + rel_pos_bias(rel_bias, h_idx, q_pos, own_pos))
        own_mask = jnp.broadcast_to(own_pos <= q_pos, own_logits.shape)
        p = masked_softmax(jnp.concatenate([sel_logits, own_logits], axis=-1),
                           jnp.concatenate([sel_mask, own_mask], axis=-1)).astype(v.dtype)
        return (jnp.einsum('bhqk,bhqkd->bhqd', p[..., :ns], v_sel)
                + jnp.einsum('bhqk,bhkd->bhqd', p[..., ns:], v_own))

    out = lax.map(query_block, jnp.arange(L // MOBA_QBLOCK))
    return out.transpose(1, 0, 3, 2, 4).reshape(n_seq, L, W_A)


def moba_sample(q, k, v, cache_k, cache_v, page_table, rel_bias):
    n_seq, L = q.shape[:2]
    ppb = MOBA_BLOCK // PAGE_SIZE
    n_full = PAST_LEN // MOBA_BLOCK
    own_start = n_full * MOBA_BLOCK
    own_past = PAST_LEN - own_start
    qs = q * HD_A ** -0.5
    q_pos = PAST_LEN + jnp.arange(L)
    q_pos_b = q_pos[None, :, None, None]
    h_idx = jnp.arange(H_A)[None, None, :, None]
    k_own, v_own = k, v
    if own_past > 0:
        own_pages = page_table[:, own_start // PAGE_SIZE: PAST_LEN // PAGE_SIZE]
        k_own = jnp.concatenate([cache_k[own_pages].reshape(n_seq, own_past, H_A, HD_A), k], axis=1)
        v_own = jnp.concatenate([cache_v[own_pages].reshape(n_seq, own_past, H_A, HD_A), v], axis=1)
    own_pos = own_start + jnp.arange(own_past + L)
    own_logits = (jnp.einsum('bqhd,bkhd->bqhk', qs, k_own, preferred_element_type=jnp.float32)
                  + rel_pos_bias(rel_bias, h_idx, q_pos_b, own_pos[None, None, None, :]))
    own_mask = jnp.broadcast_to((own_pos[None, :] <= q_pos[:, None])[None, :, None, :], own_logits.shape)
    n_sel = min(MOBA_TOPK, n_full)
    if n_sel == 0:
        p = masked_softmax(own_logits, own_mask).astype(v.dtype)
        out = jnp.einsum('bqhk,bkhd->bqhd', p, v_own)
        return out.reshape(n_seq, L, W_A)
    ns = n_sel * MOBA_BLOCK
    k_hist = cache_k[page_table[:, :n_full * ppb]].reshape(n_seq, n_full, MOBA_BLOCK, H_A, HD_A)
    k_mean = jnp.mean(k_hist.astype(jnp.float32), axis=2)
    scores = jnp.einsum('bqhd,bnhd->bqhn', qs.astype(jnp.float32), k_mean)
    _, sel = lax.top_k(scores, n_sel)
    logical = sel[..., None] * ppb + jnp.arange(ppb)
    phys = page_table[jnp.arange(n_seq)[:, None, None, None, None], logical]
    hh = jnp.arange(H_A)[None, None, :, None, None]
    k_sel = cache_k[phys, :, hh].reshape(n_seq, L, H_A, ns, HD_A)
    v_sel = cache_v[phys, :, hh].reshape(n_seq, L, H_A, ns, HD_A)
    sel_pos = (logical[..., None] * PAGE_SIZE + jnp.arange(PAGE_SIZE)).reshape(n_seq, L, H_A, ns)
    sel_logits = (jnp.einsum('bqhd,bqhkd->bqhk', qs, k_sel, preferred_element_type=jnp.float32)
                  + rel_pos_bias(rel_bias, h_idx, q_pos_b, sel_pos))
    p = masked_softmax(jnp.concatenate([sel_logits, own_logits], axis=-1),
                       jnp.concatenate([jnp.ones(sel_logits.shape, bool), own_mask], axis=-1)).astype(v.dtype)
    out = (jnp.einsum('bqhk,bqhkd->bqhd', p[..., :ns], v_sel)
           + jnp.einsum('bqhk,bkhd->bqhd', p[..., ns:], v_own))
    return out.reshape(n_seq, L, W_A)


def chunk_gated_delta(q, k, v, g, beta, s0):
    N, L, H, _ = q.shape
    dv = v.shape[-1]
    C = math.gcd(L, GDN_CHUNK)
    nc = L // C

    def chunks(t):
        return t.reshape(N, nc, C, H, t.shape[-1]).transpose(1, 0, 3, 2, 4)

    qc, kc, vc = chunks(q), chunks(k), chunks(v)
    gc = chunks(g[..., None])[..., 0]
    bc = chunks(beta[..., None])[..., 0]
    G = jnp.cumsum(gc, axis=-1)
    i = jnp.arange(C)
    incl = i[:, None] >= i[None, :]
    strict = i[:, None] > i[None, :]
    decay = jnp.exp(jnp.where(incl, G[..., :, None] - G[..., None, :], -jnp.inf))
    A = jnp.where(strict, bc[..., :, None] * jnp.einsum('...id,...jd->...ij', kc, kc) * decay, 0.0)
    gam = jnp.exp(G)
    rhs = jnp.concatenate([bc[..., None] * vc, (bc * gam)[..., None] * kc], axis=-1)
    sol = lax.linalg.triangular_solve(jnp.eye(C, dtype=jnp.float32) + A, rhs,
                                      left_side=True, lower=True, unit_diagonal=True)
    u_t, w = sol[..., :dv], sol[..., dv:]
    qk = jnp.einsum('...id,...jd->...ij', qc, kc) * decay
    q_g = gam[..., None] * qc
    k_tail = jnp.exp(G[..., -1:] - G)[..., None] * kc
    g_tot = jnp.exp(G[..., -1])

    def step(S, inp):
        u_c, w_c, qk_c, qg_c, kt_c, gt_c = inp
        U = u_c - jnp.einsum('nhck,nhkv->nhcv', w_c, S)
        O = jnp.einsum('nhck,nhkv->nhcv', qg_c, S) + jnp.einsum('nhij,nhjv->nhiv', qk_c, U)
        S = gt_c[..., None, None] * S + jnp.einsum('nhck,nhcv->nhkv', kt_c, U)
        return S, O

    s_fin, O = lax.scan(step, s0, (u_t, w, qk, q_g, k_tail, g_tot))
    return O.transpose(1, 0, 3, 2, 4).reshape(N, L, H, dv), s_fin


def gated_delta_branch(qkv, z, a, b, conv_buf, s0, conv_w, a_log, dt_bias, norm_g):
    N, L, _ = qkv.shape
    xpad = jnp.concatenate([conv_buf.astype(qkv.dtype), qkv], axis=1)
    conv = xpad[:, 0:L] * conv_w[0]
    for w in range(1, CONV_WIDTH):
        conv = conv + xpad[:, w:w + L] * conv_w[w]
    new_buf = xpad[:, L:]
    act = jax.nn.silu(conv).astype(jnp.float32)
    q, k, v = jnp.split(act, (WK_B, 2 * WK_B), axis=-1)
    q = l2_normalize(q.reshape(N, L, H_B, DK_B)) * DK_B ** -0.5
    k = l2_normalize(k.reshape(N, L, H_B, DK_B))
    v = v.reshape(N, L, H_B, DV_B)
    beta = jax.nn.sigmoid(b.astype(jnp.float32))
    g = -jnp.exp(a_log.astype(jnp.float32)) * jax.nn.softplus(a.astype(jnp.float32) + dt_bias.astype(jnp.float32))
    o, s_new = chunk_gated_delta(q, k, v, g, beta, s0.astype(jnp.float32))
    o = o * lax.rsqrt(jnp.mean(o * o, axis=-1, keepdims=True) + RMS_EPS) * norm_g.astype(jnp.float32)
    o = o * jax.nn.silu(z.astype(jnp.float32).reshape(N, L, H_B, DV_B))
    return o.reshape(N, L, WV_B).astype(qkv.dtype), new_buf, s_new


def token_mixer(h, attend, conv_buf, s0, w_in, conv_w, a_log, dt_bias, gdn_norm_g, w_branch_a, w_branch_b, w_out):
    n_seq, L, _ = h.shape
    proj = h @ w_in
    q_a, k_a, v_a, qkv_b, z_b, a_b, b_b, gate_a, gate_b = jnp.split(proj, IN_SPLITS, axis=-1)
    heads = (n_seq, L, H_A, HD_A)
    k_a = k_a.reshape(heads)
    v_a = v_a.reshape(heads)
    out_a = attend(q_a.reshape(heads), k_a, v_a)
    out_b, new_conv, new_s = gated_delta_branch(qkv_b, z_b, a_b, b_b, conv_buf, s0,
                                                conv_w, a_log, dt_bias, gdn_norm_g)
    merged = jax.nn.sigmoid(gate_a) * (out_a @ w_branch_a) + jax.nn.sigmoid(gate_b) * (out_b @ w_branch_b)
    return merged @ w_out, k_a, v_a, new_conv, new_s


def moe_ffn(h, router_w, router_b, w_gate_up, b_gate_up, w_down, b_down):
    n_seq, L, d = h.shape
    n_tok = n_seq * L
    x = h.reshape(n_tok, d)
    logits = jnp.matmul(x, router_w, preferred_element_type=jnp.float32) + router_b.astype(jnp.float32)
    top_v, top_e = lax.top_k(logits, TOP_K)
    gate = jax.nn.softmax(top_v, axis=-1)
    n_assign = n_tok * TOP_K
    flat_e = top_e.reshape(-1)
    order = jnp.argsort(flat_e, stable=True)
    e_sorted = flat_e[order]
    tok_sorted = (order // TOP_K).astype(jnp.int32)
    gate_sorted = gate.reshape(-1)[order]
    counts = jnp.zeros((N_EXPERTS,), jnp.int32).at[flat_e].add(1)
    padded = (counts + MOE_BLOCK - 1) // MOE_BLOCK * MOE_BLOCK
    pad_end = jnp.cumsum(padded)
    pad_start = pad_end - padded
    start = jnp.cumsum(counts) - counts
    dest = pad_start[e_sorted] + jnp.arange(n_assign) - start[e_sorted]
    n_blocks = -(-n_assign // MOE_BLOCK) + N_EXPERTS
    n_rows = n_blocks * MOE_BLOCK
    row_tok = jnp.full((n_rows,), n_tok, jnp.int32).at[dest].set(tok_sorted)
    row_gate = jnp.zeros((n_rows,), jnp.float32).at[dest].set(gate_sorted)
    block_e = jnp.minimum(jnp.searchsorted(pad_end, jnp.arange(n_blocks) * MOE_BLOCK, side='right'), N_EXPERTS - 1)
    x_rows = jnp.concatenate([x, jnp.zeros((1, d), x.dtype)], axis=0)[row_tok].reshape(n_blocks, MOE_BLOCK, d)

    def expert_block(args):
        xb, e = args
        gu = xb @ w_gate_up[e] + b_gate_up[e]
        gl = jnp.minimum(gu[..., 0::2], SWIGLU_LIMIT)
        up = jnp.clip(gu[..., 1::2], -SWIGLU_LIMIT, SWIGLU_LIMIT)
        act = gl * jax.nn.sigmoid(SWIGLU_ALPHA * gl) * (up + 1.0)
        return act @ w_down[e] + b_down[e]

    y_rows = lax.map(expert_block, (x_rows, block_e)).reshape(n_rows, d)
    y = jax.ops.segment_sum(y_rows.astype(jnp.float32) * row_gate[:, None], row_tok, num_segments=n_tok + 1)[:n_tok]
    return y.astype(h.dtype).reshape(n_seq, L, d)


def setup_inputs(seed: int = 0) -> dict:
    key = jax.random.key(seed)
    ks = jax.random.split(key, 24)
    f32 = jnp.float32

    def nrm(k, shape, scale):
        return scale * jax.random.normal(k, shape, f32)

    n_pages = PAST_LEN // PAGE_SIZE
    n_used = DEC_BATCH * n_pages
    n_pool = n_used + (n_used + 3) // 4
    page_table = jax.random.permutation(ks[0], n_pool)[:n_used].reshape(DEC_BATCH, n_pages).astype(jnp.int32)
    dt = jnp.exp(jax.random.uniform(ks[1], (DEPTH, H_B), f32, math.log(1e-3), math.log(1e-1)))
    return {
        'x_prompt': nrm(ks[2], (BATCH, SEQ, D_MODEL), 1.0),
        'x_sample': nrm(ks[3], (DEC_BATCH, DEC_SEQ, D_MODEL), 1.0),
        'cache_k': nrm(ks[4], (DEPTH, n_pool, PAGE_SIZE, H_A, HD_A), 1.0),
        'cache_v': nrm(ks[5], (DEPTH, n_pool, PAGE_SIZE, H_A, HD_A), 1.0),
        'state_gdn': nrm(ks[6], (DEPTH, DEC_BATCH, H_B, DK_B, DV_B), 0.1),
        'state_conv': nrm(ks[7], (DEPTH, DEC_BATCH, CONV_WIDTH - 1, QKV_B), 1.0),
        'page_table': page_table,
        'norm_mix_g': 1.0 + nrm(ks[8], (DEPTH, D_MODEL), 0.02),
        'w_in': nrm(ks[9], (DEPTH, D_MODEL, IN_COLS), D_MODEL ** -0.5),
        'rel_bias': nrm(ks[10], (H_A, N_BUCKETS), 0.5),
        'conv_w': nrm(ks[11], (DEPTH, CONV_WIDTH, QKV_B), 0.5),
        'a_log': jnp.log(jax.random.uniform(ks[12], (DEPTH, H_B), f32, 1.0, 16.0)),
        'dt_bias': dt + jnp.log(-jnp.expm1(-dt)),
        'gdn_norm_g': 1.0 + nrm(ks[13], (DEPTH, DV_B), 0.02),
        'w_branch_a': nrm(ks[14], (DEPTH, W_A, D_MODEL), W_A ** -0.5),
        'w_branch_b': nrm(ks[15], (DEPTH, WV_B, D_MODEL), WV_B ** -0.5),
        'w_out': nrm(ks[16], (DEPTH, D_MODEL, D_MODEL), D_MODEL ** -0.5),
        'norm_ffn_g': 1.0 + nrm(ks[17], (DEPTH, D_MODEL), 0.02),
        'router_w': nrm(ks[18], (DEPTH, D_MODEL, N_EXPERTS), D_MODEL ** -0.5),
        'router_b': nrm(ks[19], (DEPTH, N_EXPERTS), 0.01),
        'w_gate_up': nrm(ks[20], (DEPTH, N_EXPERTS, D_MODEL, 2 * D_EXPERT), D_MODEL ** -0.5),
        'b_gate_up': nrm(ks[21], (DEPTH, N_EXPERTS, 2 * D_EXPERT), 0.01),
        'w_down': nrm(ks[22], (DEPTH, N_EXPERTS, D_EXPERT, D_MODEL), D_EXPERT ** -0.5),
        'b_down': nrm(ks[23], (DEPTH, N_EXPERTS, D_MODEL), 0.01),
        'norm_final_g': 1.0 + nrm(jax.random.fold_in(key, 99), (D_MODEL,), 0.02),
    }


def reference(x_prompt, x_sample, cache_k, cache_v, state_gdn, state_conv, page_table,
              norm_mix_g, w_in, rel_bias, conv_w, a_log, dt_bias, gdn_norm_g,
              w_branch_a, w_branch_b, w_out, norm_ffn_g, router_w, router_b,
              w_gate_up, b_gate_up, w_down, b_down, norm_final_g):
    yp, ys = x_prompt, x_sample
    kp_l, vp_l, sp_l, cp_l, ks_l, vs_l, ss_l, cs_l = [], [], [], [], [], [], [], []
    n_prompt = x_prompt.shape[0]
    for l in range(DEPTH):
        mix_w = (w_in[l], conv_w[l], a_log[l], dt_bias[l], gdn_norm_g[l], w_branch_a[l], w_branch_b[l], w_out[l])
        moe_w = (router_w[l], router_b[l], w_gate_up[l], b_gate_up[l], w_down[l], b_down[l])
        conv0 = jnp.zeros((n_prompt, CONV_WIDTH - 1, QKV_B), x_prompt.dtype)
        s0 = jnp.zeros((n_prompt, H_B, DK_B, DV_B), jnp.float32)
        attend_p = functools.partial(moba_prompt, rel_bias=rel_bias)
        mix, kp, vp, cp, sp = token_mixer(rms_norm(yp, norm_mix_g[l]), attend_p, conv0, s0, *mix_w)
        yp = yp + mix
        yp = yp + moe_ffn(rms_norm(yp, norm_ffn_g[l]), *moe_w)
        attend_s = functools.partial(moba_sample, cache_k=cache_k[l], cache_v=cache_v[l],
                                     page_table=page_table, rel_bias=rel_bias)
        mix, kss, vss, css, sss = token_mixer(rms_norm(ys, norm_mix_g[l]), attend_s, state_conv[l], state_gdn[l], *mix_w)
        ys = ys + mix
        ys = ys + moe_ffn(rms_norm(ys, norm_ffn_g[l]), *moe_w)
        kp_l.append(kp.astype(cache_k.dtype))
        vp_l.append(vp.astype(cache_v.dtype))
        sp_l.append(sp.astype(state_gdn.dtype))
        cp_l.append(cp.astype(state_conv.dtype))
        ks_l.append(kss.astype(cache_k.dtype))
        vs_l.append(vss.astype(cache_v.dtype))
        ss_l.append(sss.astype(state_gdn.dtype))
        cs_l.append(css.astype(state_conv.dtype))
    y_prompt = rms_norm(yp, norm_final_g)
    y_sample = rms_norm(ys, norm_final_g)
    return (y_prompt, y_sample,
            jnp.stack(kp_l), jnp.stack(vp_l), jnp.stack(sp_l), jnp.stack(cp_l),
            jnp.stack(ks_l), jnp.stack(vs_l), jnp.stack(ss_l), jnp.stack(cs_l))
```

```python
import functools
import math

import jax
import jax.numpy as jnp
from jax import lax
from jax.experimental import pallas as pl
from jax.experimental.pallas import tpu as pltpu

F32 = jnp.float32
BF16 = jnp.bfloat16
HIGHEST = lax.Precision.HIGHEST

RMS_EPS = 1e-6
NEG_INF = -1e30

H_A = 8
HD_A = 128
MOBA_BLOCK = 256
MOBA_TOPK = 3
PAGE_SIZE = 128
N_BUCKETS = 32
MAX_DISTANCE = 4096

H_B = 8
DK_B = 128
CONV_WIDTH = 4
GDN_CHUNK = 64

TOP_K = 4
SWIGLU_LIMIT = 7.0
SWIGLU_ALPHA = 1.702
MOE_ROWS = 256

VMEM_LIMIT = 56 * 1024 * 1024


def _cparams(sem):
    return pltpu.CompilerParams(dimension_semantics=sem, vmem_limit_bytes=VMEM_LIMIT)


def _dot(a, b, precision=None):
    return jnp.dot(a, b, preferred_element_type=F32, precision=precision)


def _dot_nt(a, b, precision=None):
    return lax.dot_general(a, b, (((1,), (1,)), ((), ())),
                           preferred_element_type=F32, precision=precision)


def _dot_tn(a, b, precision=None):
    return lax.dot_general(a, b, (((0,), (0,)), ((), ())),
                           preferred_element_type=F32, precision=precision)


def _norm_matmul_kernel(x_ref, g_ref, w_ref, o_ref, *rest, emit_h):
    if emit_h:
        h_out_ref, h_ref = rest
    else:
        (h_ref,) = rest

    @pl.when(pl.program_id(1) == 0)
    def _():
        x = x_ref[...]
        y = x * lax.rsqrt(jnp.mean(x * x, axis=-1, keepdims=True) + RMS_EPS)
        h_ref[...] = (y * g_ref[...]).astype(BF16)

    if emit_h:
        h_out_ref[...] = h_ref[...]
    o_ref[...] = _dot(h_ref[...], w_ref[...])


def norm_matmul(x, g, w, tm, tn, emit_h=False):
    m, k = x.shape
    n = w.shape[1]
    out_shape = [jax.ShapeDtypeStruct((m, n), F32)]
    out_specs = [pl.BlockSpec((tm, tn), lambda i, j: (i, j))]
    if emit_h:
        out_shape.append(jax.ShapeDtypeStruct((m, k), BF16))
        out_specs.append(pl.BlockSpec((tm, k), lambda i, j: (i, 0)))
    res = pl.pallas_call(
        functools.partial(_norm_matmul_kernel, emit_h=emit_h),
        grid=(m // tm, n // tn),
        in_specs=[pl.BlockSpec((tm, k), lambda i, j: (i, 0)),
                  pl.BlockSpec((1, k), lambda i, j: (0, 0)),
                  pl.BlockSpec((k, tn), lambda i, j: (0, j))],
        out_specs=out_specs,
        out_shape=out_shape,
        scratch_shapes=[pltpu.VMEM((tm, k), BF16)],
        compiler_params=_cparams(("parallel", "arbitrary")),
        name="norm_matmul",
    )(x, g.reshape(1, k), w)
    return res if emit_h else res[0]


def _moba_prompt_kernel(q_ref, k_ref, v_ref, bias_ref, o_ref,
                        kb_ref, vb_ref, kmean_ref, sel_ref, m_ref, l_ref, acc_ref,
                        *, n_blocks):
    i = pl.program_id(2)
    blk = MOBA_BLOCK

    @pl.when(i == 0)
    def _():
        k = k_ref[...]
        kb_ref[...] = k.astype(BF16)
        vb_ref[...] = v_ref[...].astype(BF16)
        kmean_ref[...] = jnp.mean(k.reshape(n_blocks, blk, HD_A), axis=1)

    qs = q_ref[...] * (HD_A ** -0.5)
    qb = qs.astype(BF16)

    scores = _dot_nt(qs, kmean_ref[...], precision=HIGHEST)
    col = lax.broadcasted_iota(jnp.int32, scores.shape, 1)
    scores = jnp.where(col < i, scores, NEG_INF)
    sel = jnp.zeros(scores.shape, F32)
    for c in range(n_blocks - 1):
        sc = scores[:, c:c + 1]
        ahead = jnp.where((scores > sc) | ((scores == sc) & (col < c)), 1.0, 0.0)
        rank = jnp.sum(ahead, axis=-1, keepdims=True)
        sel = jnp.where((col == c) & (rank < MOBA_TOPK), 1.0, sel)
    sel_ref[...] = jnp.where(col < i, sel, 0.0)

    row0 = pl.multiple_of(i * blk, blk)
    r = lax.broadcasted_iota(jnp.int32, (blk, blk), 0)
    c_ = lax.broadcasted_iota(jnp.int32, (blk, blk), 1)
    logits = _dot_nt(qb, kb_ref[pl.ds(row0, blk), :]) + bias_ref[0, 0]
    logits = jnp.where(c_ <= r, logits, NEG_INF)
    m0 = jnp.max(logits, axis=-1, keepdims=True)
    p = jnp.exp(logits - m0)
    m_ref[...] = m0
    l_ref[...] = jnp.sum(p, axis=-1, keepdims=True)
    acc_ref[...] = _dot(p.astype(BF16), vb_ref[pl.ds(row0, blk), :])

    for j in range(n_blocks - 1):
        @pl.when(j < i)
        def _(j=j):
            lg = _dot_nt(qb, kb_ref[j * blk:(j + 1) * blk, :]) + bias_ref[0, i - j]
            lg = jnp.where(sel_ref[:, j:j + 1] > 0.0, lg, NEG_INF)
            m_old = m_ref[...]
            m_new = jnp.maximum(m_old, jnp.max(lg, axis=-1, keepdims=True))
            alpha = jnp.exp(m_old - m_new)
            pj = jnp.exp(lg - m_new)
            m_ref[...] = m_new
            l_ref[...] = alpha * l_ref[...] + jnp.sum(pj, axis=-1, keepdims=True)
            acc_ref[...] = alpha * acc_ref[...] + _dot(pj.astype(BF16),
                                                       vb_ref[j * blk:(j + 1) * blk, :])

    o_ref[...] = (acc_ref[...] / l_ref[...]).astype(o_ref.dtype)


def moba_prompt(proj, bias_tab, n_seq, seq_len, q_col, k_col, v_col):
    blk = MOBA_BLOCK
    nb = seq_len // blk
    return pl.pallas_call(
        functools.partial(_moba_prompt_kernel, n_blocks=nb),
        grid=(H_A, n_seq, nb),
        in_specs=[pl.BlockSpec((blk, HD_A), lambda h, n, i: (n * nb + i, q_col + h)),
                  pl.BlockSpec((seq_len, HD_A), lambda h, n, i: (n, k_col + h)),
                  pl.BlockSpec((seq_len, HD_A), lambda h, n, i: (n, v_col + h)),
                  pl.BlockSpec((1, nb, blk, blk), lambda h, n, i: (h, 0, 0, 0))],
        out_specs=pl.BlockSpec((blk, HD_A), lambda h, n, i: (n * nb + i, h)),
        out_shape=jax.ShapeDtypeStruct((n_seq * seq_len, H_A * HD_A), BF16),
        scratch_shapes=[pltpu.VMEM((seq_len, HD_A), BF16),
                        pltpu.VMEM((seq_len, HD_A), BF16),
                        pltpu.VMEM((nb, HD_A), F32),
                        pltpu.VMEM((blk, nb), F32),
                        pltpu.VMEM((blk, 1), F32),
                        pltpu.VMEM((blk, 1), F32),
                        pltpu.VMEM((blk, HD_A), F32)],
        compiler_params=_cparams(("arbitrary", "arbitrary", "arbitrary")),
        name="moba_prompt",
    )(proj, proj, proj, bias_tab)


PAGES_PER_STEP = 16
BLOCKS_PER_STEP = PAGES_PER_STEP * PAGE_SIZE // MOBA_BLOCK


def _moba_sample_keys_kernel(pt_ref, qbd_ref, knew_ref, bias_ref, bown_ref, *rest,
                             n_steps, n_tok):
    pps = PAGES_PER_STEP
    bps = BLOCKS_PER_STEP
    ppb = pps // bps
    kp = rest[:pps]
    p_ref, pown_ref, lg_ref, ksum_ref = rest[pps:]
    s = pl.program_id(1)
    qbd = qbd_ref[0]

    def lg_block(b):
        return b // bps, slice((b % bps) * MOBA_BLOCK, (b % bps + 1) * MOBA_BLOCK)

    sums = []
    for k in range(pps):
        page = kp[k][0]
        lg = _dot_nt(qbd, page) + bias_ref[:, k * PAGE_SIZE:(k + 1) * PAGE_SIZE]
        lg_ref[s, :, k * PAGE_SIZE:(k + 1) * PAGE_SIZE] = lg
        sums.append(jnp.sum(page, axis=0, keepdims=True))
    blocks = [sum(sums[ppb * b:ppb * (b + 1)]) * (1.0 / MOBA_BLOCK) for b in range(bps)]
    ksum_ref[pl.ds(pl.multiple_of(s * bps, bps), bps), :] = jnp.concatenate(blocks, axis=0)

    @pl.when(s == n_steps - 1)
    def _():
        n_blk = n_steps * bps
        sc = _dot_nt(qbd, ksum_ref[...], precision=HIGHEST)
        bidx = lax.broadcasted_iota(jnp.int32, sc.shape, 1)
        sel = jnp.zeros(sc.shape, F32)
        for _ in range(MOBA_TOPK):
            mx = jnp.max(sc, axis=-1, keepdims=True)
            first = jnp.min(jnp.where(sc == mx, bidx, n_blk), axis=-1, keepdims=True)
            hit = bidx == first
            sel = jnp.where(hit, 1.0, sel)
            sc = jnp.where(hit, 2.0 * NEG_INF, sc)

        lo = _dot_nt(qbd, knew_ref[...]) + bown_ref[...]
        kc = lax.broadcasted_iota(jnp.int32, lo.shape, 1)
        qr = lax.broadcasted_iota(jnp.int32, lo.shape, 0) % n_tok
        lo = jnp.where(kc <= qr, lo, NEG_INF)
        m = jnp.max(lo, axis=-1, keepdims=True)

        for b in range(n_blk):
            st, sl = lg_block(b)
            x = jnp.where(sel[:, b:b + 1] > 0.0, lg_ref[st, :, sl], NEG_INF)
            lg_ref[st, :, sl] = x
            m = jnp.maximum(m, jnp.max(x, axis=-1, keepdims=True))
        e_own = jnp.exp(lo - m)
        l = jnp.sum(e_own, axis=-1, keepdims=True)
        for b in range(n_blk):
            st, sl = lg_block(b)
            e = jnp.exp(lg_ref[st, :, sl] - m)
            lg_ref[st, :, sl] = e
            l = l + jnp.sum(e, axis=-1, keepdims=True)
        inv = 1.0 / l
        pown_ref[0] = e_own * inv
        for b in range(n_blk):
            st, sl = lg_block(b)
            p_ref[0, :, b * MOBA_BLOCK:(b + 1) * MOBA_BLOCK] = (lg_ref[st, :, sl] * inv).astype(BF16)


def _moba_sample_values_kernel(pt_ref, p_ref, pown_ref, vnew_ref, *rest, n_steps, n_tok):
    pps = PAGES_PER_STEP
    vp = rest[:pps]
    o_ref, acc_ref = rest[pps:]
    s = pl.program_id(1)

    @pl.when(s == 0)
    def _():
        acc_ref[...] = _dot(pown_ref[0], vnew_ref[...])

    acc = acc_ref[...]
    for k in range(pps):
        acc = acc + _dot(p_ref[0, :, k * PAGE_SIZE:(k + 1) * PAGE_SIZE], vp[k][0].astype(BF16))
    acc_ref[...] = acc

    @pl.when(s == n_steps - 1)
    def _():
        for h in range(H_A):
            o_ref[:, h * HD_A:(h + 1) * HD_A] = acc[h * n_tok:(h + 1) * n_tok,
                                                    h * HD_A:(h + 1) * HD_A].astype(o_ref.dtype)


def moba_sample(proj, row_blk0, k_colblk, v_colblk, qbd, cache_k, cache_v, page_table,
                bias_past, bias_own, n_tok):
    n_seq, n_pages = page_table.shape
    pps = PAGES_PER_STEP
    n_steps = n_pages // pps
    n_past = n_pages * PAGE_SIZE
    wa = H_A * HD_A
    nrow = H_A * n_tok
    ck = cache_k.reshape(cache_k.shape[0], PAGE_SIZE, wa)
    cv = cache_v.reshape(cache_v.shape[0], PAGE_SIZE, wa)

    def pmap(k):
        return lambda b, s, pt: (pt[b, s * pps + k], 0, 0)

    p, p_own = pl.pallas_call(
        functools.partial(_moba_sample_keys_kernel, n_steps=n_steps, n_tok=n_tok),
        grid_spec=pltpu.PrefetchScalarGridSpec(
            num_scalar_prefetch=1,
            grid=(n_seq, n_steps),
            in_specs=[pl.BlockSpec((1, nrow, wa), lambda b, s, pt: (b, 0, 0)),
                      pl.BlockSpec((n_tok, wa), lambda b, s, pt: (row_blk0 + b, k_colblk)),
                      pl.BlockSpec((nrow, pps * PAGE_SIZE), lambda b, s, pt: (0, s)),
                      pl.BlockSpec((nrow, n_tok), lambda b, s, pt: (0, 0))]
            + [pl.BlockSpec((1, PAGE_SIZE, wa), pmap(k)) for k in range(pps)],
            out_specs=[pl.BlockSpec((1, nrow, n_past), lambda b, s, pt: (b, 0, 0)),
                       pl.BlockSpec((1, nrow, n_tok), lambda b, s, pt: (b, 0, 0))],
            scratch_shapes=[pltpu.VMEM((n_steps, nrow, pps * PAGE_SIZE), F32),
                            pltpu.VMEM((n_past // MOBA_BLOCK, wa), F32)]),
        out_shape=[jax.ShapeDtypeStruct((n_seq, nrow, n_past), BF16),
                   jax.ShapeDtypeStruct((n_seq, nrow, n_tok), F32)],
        compiler_params=_cparams(("arbitrary", "arbitrary")),
        name="moba_sample_keys",
    )(page_table, qbd, proj, bias_past, bias_own, *([ck] * pps))

    return pl.pallas_call(
        functools.partial(_moba_sample_values_kernel, n_steps=n_steps, n_tok=n_tok),
        grid_spec=pltpu.PrefetchScalarGridSpec(
            num_scalar_prefetch=1,
            grid=(n_seq, n_steps),
            in_specs=[pl.BlockSpec((1, nrow, pps * PAGE_SIZE), lambda b, s, pt: (b, 0, s)),
                      pl.BlockSpec((1, nrow, n_tok), lambda b, s, pt: (b, 0, 0)),
                      pl.BlockSpec((n_tok, wa), lambda b, s, pt: (row_blk0 + b, v_colblk))]
            + [pl.BlockSpec((1, PAGE_SIZE, wa), pmap(k)) for k in range(pps)],
            out_specs=pl.BlockSpec((n_tok, wa), lambda b, s, pt: (b, 0)),
            scratch_shapes=[pltpu.VMEM((nrow, wa), F32)]),
        out_shape=jax.ShapeDtypeStruct((n_seq * n_tok, wa), BF16),
        compiler_params=_cparams(("arbitrary", "arbitrary")),
        name="moba_sample_values",
    )(page_table, p, p_own, proj, *([cv] * pps))


def _gdn_prep_kernel(x_ref, halo_ref, ab_ref, cw_ref, alog_ref, dtb_ref,
                     q_ref, k_ref, v_ref, gb_ref, *, tt):
    x = x_ref[...]
    xf = jnp.concatenate([halo_ref[0], x], axis=0)
    cw = cw_ref[...]
    conv = xf[5:5 + tt] * cw[0:1]
    for w in range(1, CONV_WIDTH):
        conv = conv + xf[5 + w:5 + w + tt] * cw[w:w + 1]
    act = conv * jax.nn.sigmoid(conv)
    wk = H_B * DK_B
    for h in range(H_B):
        q = act[:, h * DK_B:(h + 1) * DK_B]
        k = act[:, wk + h * DK_B:wk + (h + 1) * DK_B]
        q_ref[0, h] = q * lax.rsqrt(jnp.sum(q * q, axis=-1, keepdims=True) + RMS_EPS) * (DK_B ** -0.5)
        k_ref[0, h] = k * lax.rsqrt(jnp.sum(k * k, axis=-1, keepdims=True) + RMS_EPS)
        v_ref[0, h] = act[:, 2 * wk + h * DK_B:2 * wk + (h + 1) * DK_B]
    ab = ab_ref[...]
    t = ab + dtb_ref[...]
    sp = jnp.maximum(t, 0.0) + jnp.log(1.0 + jnp.exp(-jnp.abs(t)))
    g = -jnp.exp(alog_ref[...]) * sp
    lane = lax.broadcasted_iota(jnp.int32, ab.shape, 1)
    gb_ref[0] = jnp.where(lane < H_B, g, jax.nn.sigmoid(ab))


def gdn_prep(proj, ab, halo, conv_w, a_log, dt_bias, row0, n_seq, seq_len, tt, qkv_colblk):
    w3 = conv_w.shape[1]
    nt = seq_len // tt
    rb0 = row0 // tt
    pad = jnp.zeros((1, 128 - H_B), F32)
    alog = jnp.concatenate([a_log.reshape(1, H_B), pad], axis=1)
    dtb = jnp.concatenate([dt_bias.reshape(1, H_B), pad], axis=1)
    hm = jax.ShapeDtypeStruct((n_seq, H_B, seq_len, DK_B), F32)
    hspec = pl.BlockSpec((1, H_B, tt, DK_B), lambda n, t: (n, 0, t, 0))
    return pl.pallas_call(
        functools.partial(_gdn_prep_kernel, tt=tt),
        grid=(n_seq, nt),
        in_specs=[pl.BlockSpec((tt, w3), lambda n, t: (rb0 + n * nt + t, qkv_colblk)),
                  pl.BlockSpec((1, 8, w3), lambda n, t: (n * nt + t, 0, 0)),
                  pl.BlockSpec((tt, 128), lambda n, t: (rb0 + n * nt + t, 0)),
                  pl.BlockSpec((CONV_WIDTH, w3), lambda n, t: (0, 0)),
                  pl.BlockSpec((1, 128), lambda n, t: (0, 0)),
                  pl.BlockSpec((1, 128), lambda n, t: (0, 0))],
        out_specs=[hspec, hspec, hspec,
                   pl.BlockSpec((1, tt, 128), lambda n, t: (n, t, 0))],
        out_shape=[hm, hm, hm, jax.ShapeDtypeStruct((n_seq, seq_len, 128), F32)],
        compiler_params=_cparams(("parallel", "parallel")),
        name="gdn_prep",
    )(proj, halo, ab, conv_w, alog, dtb)


def _unit_lower_inverse(a, c):
    r = lax.broadcasted_iota(jnp.int32, (c, c), 0)
    cc = lax.broadcasted_iota(jnp.int32, (c, c), 1)
    eye = jnp.where(r == cc, 1.0, 0.0)
    base = min(c, 8)
    n1 = jnp.where((r // base) == (cc // base), -a, 0.0)
    n2 = _dot(n1, n1, HIGHEST)
    n4 = _dot(n2, n2, HIGHEST)
    t = _dot(_dot(eye + n1, eye + n2, HIGHEST), eye + n4, HIGHEST)
    b = base
    while b < c:
        off = jnp.where(((r // (2 * b)) == (cc // (2 * b))) & ((r // b) != (cc // b)), a, 0.0)
        t = t - _dot(_dot(t, off, HIGHEST), t, HIGHEST)
        b *= 2
    return t


def _gdn_chunk_kernel(*refs, nb, c):
    q_ref, k_ref, v_ref, gb_ref = refs[:4]
    z_refs = refs[4:4 + nb]
    ng_ref, s0_ref, o_ref, sout_ref, s_ref = refs[4 + nb:]
    ci = pl.program_id(1)

    @pl.when(ci == 0)
    def _():
        s_ref[...] = s0_ref[...]

    r = lax.broadcasted_iota(jnp.int32, (c, c), 0)
    cc = lax.broadcasted_iota(jnp.int32, (c, c), 1)
    incl = r >= cc
    strict = r > cc
    ltri = jnp.where(incl, 1.0, 0.0)
    eye = jnp.where(r == cc, 1.0, 0.0)
    ones = jnp.ones((c, c), F32)
    ng = ng_ref[...]
    for n in range(nb):
        gbv = gb_ref[n]
        z = z_refs[n][...]
        for h in range(H_B):
            q = q_ref[n, h]
            k = k_ref[n, h]
            v = v_ref[n, h]
            gcol = jnp.broadcast_to(gbv[:, h:h + 1], (c, DK_B))
            bcol = jnp.broadcast_to(gbv[:, H_B + h:H_B + h + 1], (c, DK_B))
            big_g = _dot(ltri, gcol, HIGHEST)
            gi = big_g[:, :c]
            gj = _dot(ones, gi * eye, HIGHEST)
            decay = jnp.where(incl, jnp.exp(jnp.minimum(gi - gj, 0.0)), 0.0)
            kk = _dot_nt(k, k)
            a = jnp.where(strict, bcol[:, :c] * kk * decay, 0.0)
            t = _unit_lower_inverse(a, c)
            gam = jnp.exp(big_g)
            u = _dot(t, bcol * v, HIGHEST)
            w = _dot(t, bcol * gam * k, HIGHEST)
            qk = _dot_nt(q, k) * decay
            g_last = big_g[c - 1:c, :]
            k_tail = jnp.exp(g_last - big_g) * k
            s = s_ref[n, h]
            uu = u - _dot(w, s)
            o = _dot(gam * q, s) + _dot(qk, uu)
            s_ref[n, h] = jnp.exp(g_last) * s + _dot_tn(k_tail, uu)
            o = o * lax.rsqrt(jnp.mean(o * o, axis=-1, keepdims=True) + RMS_EPS) * ng
            zh = z[:, h * DK_B:(h + 1) * DK_B]
            o_ref[n, :, h * DK_B:(h + 1) * DK_B] = (o * (zh * jax.nn.sigmoid(zh))).astype(o_ref.dtype)

    @pl.when(ci == pl.num_programs(1) - 1)
    def _():
        sout_ref[...] = s_ref[...]


def gdn_chunks(q, k, v, gb, proj, z_colblk, row0, norm_g, s0, nb, c):
    n_seq, _, seq_len, _ = q.shape
    nc = seq_len // c
    wv = H_B * DK_B
    rb0 = row0 // c
    hspec = pl.BlockSpec((nb, H_B, c, DK_B), lambda n, ci: (n, 0, ci, 0))
    sspec = pl.BlockSpec((nb, H_B, DK_B, DK_B), lambda n, ci: (n, 0, 0, 0))

    def zmap(j):
        return lambda n, ci: (rb0 + (n * nb + j) * nc + ci, z_colblk)

    return pl.pallas_call(
        functools.partial(_gdn_chunk_kernel, nb=nb, c=c),
        grid=(n_seq // nb, nc),
        in_specs=[hspec, hspec, hspec,
                  pl.BlockSpec((nb, c, 128), lambda n, ci: (n, ci, 0))]
        + [pl.BlockSpec((c, wv), zmap(j)) for j in range(nb)]
        + [pl.BlockSpec((1, DK_B), lambda n, ci: (0, 0)), sspec],
        out_specs=[pl.BlockSpec((nb, c, wv), lambda n, ci: (n, ci, 0)), sspec],
        out_shape=[jax.ShapeDtypeStruct((n_seq, seq_len, wv), BF16),
                   jax.ShapeDtypeStruct((n_seq, H_B, DK_B, DK_B), F32)],
        scratch_shapes=[pltpu.VMEM((nb, H_B, DK_B, DK_B), F32)],
        compiler_params=_cparams(("parallel", "arbitrary")),
        name="gdn_chunks",
    )(q, k, v, gb, *([proj] * nb), norm_g.reshape(1, DK_B), s0)


def _merge_kernel(oa_ref, ob_ref, wa_ref, wb_ref, ga_ref, gb_ref, o_ref):
    ya = _dot(oa_ref[...], wa_ref[...])
    yb = _dot(ob_ref[...], wb_ref[...])
    o_ref[...] = (jax.nn.sigmoid(ga_ref[...]) * ya
                  + jax.nn.sigmoid(gb_ref[...]) * yb).astype(o_ref.dtype)


def branch_merge(out_a, out_b, wa, wb, proj, ga_col0, gb_col0, tm, tn):
    m, ka = out_a.shape
    n = wa.shape[1]
    ga_blk, gb_blk = ga_col0 // tn, gb_col0 // tn
    assert ga_blk * tn == ga_col0 and gb_blk * tn == gb_col0
    return pl.pallas_call(
        _merge_kernel,
        grid=(m // tm, n // tn),
        in_specs=[pl.BlockSpec((tm, ka), lambda i, j: (i, 0)),
                  pl.BlockSpec((tm, ka), lambda i, j: (i, 0)),
                  pl.BlockSpec((ka, tn), lambda i, j: (0, j)),
                  pl.BlockSpec((ka, tn), lambda i, j: (0, j)),
                  pl.BlockSpec((tm, tn), lambda i, j: (i, ga_blk + j)),
                  pl.BlockSpec((tm, tn), lambda i, j: (i, gb_blk + j))],
        out_specs=pl.BlockSpec((tm, tn), lambda i, j: (i, j)),
        out_shape=jax.ShapeDtypeStruct((m, n), BF16),
        compiler_params=_cparams(("parallel", "parallel")),
        name="branch_merge",
    )(out_a, out_b, wa, wb, proj, proj)


def _matmul_residual_kernel(x_ref, w_ref, r_ref, o_ref):
    o_ref[...] = r_ref[...] + _dot(x_ref[...], w_ref[...])


def matmul_residual(x, w, res, tm, tn):
    m, k = x.shape
    n = w.shape[1]
    return pl.pallas_call(
        _matmul_residual_kernel,
        grid=(m // tm, n // tn),
        in_specs=[pl.BlockSpec((tm, k), lambda i, j: (i, 0)),
                  pl.BlockSpec((k, tn), lambda i, j: (0, j)),
                  pl.BlockSpec((tm, tn), lambda i, j: (i, j))],
        out_specs=pl.BlockSpec((tm, tn), lambda i, j: (i, j)),
        out_shape=jax.ShapeDtypeStruct((m, n), F32),
        compiler_params=_cparams(("parallel", "parallel")),
        name="out_proj",
    )(x, w, res)


def _moe_up_kernel(be_ref, nu_ref, x_ref, wg_ref, wu_ref, bg_ref, bu_ref, o_ref):
    @pl.when(pl.program_id(1) < nu_ref[0])
    def _():
        x = x_ref[...]
        gl = jnp.minimum(_dot(x, wg_ref[0]) + bg_ref[0], SWIGLU_LIMIT)
        up = jnp.clip(_dot(x, wu_ref[0]) + bu_ref[0], -SWIGLU_LIMIT, SWIGLU_LIMIT)
        o_ref[...] = (gl * jax.nn.sigmoid(SWIGLU_ALPHA * gl) * (up + 1.0)).astype(o_ref.dtype)


def _moe_down_kernel(be_ref, nu_ref, a_ref, wd_ref, bd_ref, o_ref):
    @pl.when(pl.program_id(1) < nu_ref[0])
    def _():
        o_ref[...] = _dot(a_ref[...], wd_ref[0]) + bd_ref[0]


def moe_experts(x_rows, block_e, n_used, wg, wu, bg, bu, wd, bd, tn_up, tn_down):
    rows, d = x_rows.shape
    _, _, f = wg.shape
    nblk = rows // MOE_ROWS

    def rowmap(n, b, be, nu):
        return (jnp.minimum(b, nu[0] - 1), 0)

    def outmap(n, b, be, nu):
        return (jnp.minimum(b, nu[0] - 1), n)

    def wmap(n, b, be, nu):
        return (be[jnp.minimum(b, nu[0] - 1)], 0, n)

    act = pl.pallas_call(
        _moe_up_kernel,
        grid_spec=pltpu.PrefetchScalarGridSpec(
            num_scalar_prefetch=2,
            grid=(f // tn_up, nblk),
            in_specs=[pl.BlockSpec((MOE_ROWS, d), rowmap),
                      pl.BlockSpec((1, d, tn_up), wmap),
                      pl.BlockSpec((1, d, tn_up), wmap),
                      pl.BlockSpec((1, 1, tn_up), wmap),
                      pl.BlockSpec((1, 1, tn_up), wmap)],
            out_specs=pl.BlockSpec((MOE_ROWS, tn_up), outmap)),
        out_shape=jax.ShapeDtypeStruct((rows, f), BF16),
        compiler_params=_cparams(("arbitrary", "arbitrary")),
        name="moe_up",
    )(block_e, n_used, x_rows, wg, wu, bg, bu)
    dm = wd.shape[2]
    return pl.pallas_call(
        _moe_down_kernel,
        grid_spec=pltpu.PrefetchScalarGridSpec(
            num_scalar_prefetch=2,
            grid=(dm // tn_down, nblk),
            in_specs=[pl.BlockSpec((MOE_ROWS, f), rowmap),
                      pl.BlockSpec((1, f, tn_down), wmap),
                      pl.BlockSpec((1, 1, tn_down), wmap)],
            out_specs=pl.BlockSpec((MOE_ROWS, tn_down), outmap)),
        out_shape=jax.ShapeDtypeStruct((rows, dm), F32),
        compiler_params=_cparams(("arbitrary", "arbitrary")),
        name="moe_down",
    )(block_e, n_used, act, wd, bd)


def _final_kernel(y_ref, m_ref, g_ref, o_ref):
    x = y_ref[...] + m_ref[...]
    o_ref[...] = x * lax.rsqrt(jnp.mean(x * x, axis=-1, keepdims=True) + RMS_EPS) * g_ref[...]


def residual_norm(y, moe, g, tm):
    m, d = y.shape
    spec = pl.BlockSpec((tm, d), lambda i: (i, 0))
    return pl.pallas_call(
        _final_kernel,
        grid=(m // tm,),
        in_specs=[spec, spec, pl.BlockSpec((1, d), lambda i: (0, 0))],
        out_specs=spec,
        out_shape=jax.ShapeDtypeStruct((m, d), F32),
        compiler_params=_cparams(("parallel",)),
        name="residual_norm",
    )(y, moe, g.reshape(1, d))


def _rel_bias_table(rel_bias, dist):
    n = jnp.maximum(dist, 0)
    max_exact = N_BUCKETS // 2
    nf = jnp.maximum(n, max_exact).astype(F32)
    large = max_exact + (jnp.log(nf / max_exact) / math.log(MAX_DISTANCE / max_exact)
                         * (N_BUCKETS - max_exact)).astype(jnp.int32)
    bucket = jnp.where(n < max_exact, n, jnp.minimum(large, N_BUCKETS - 1))
    return rel_bias[:, bucket].astype(F32)


def _route(logits, n_tok):
    top_v, top_e = lax.top_k(logits, TOP_K)
    gate = jax.nn.softmax(top_v, axis=-1)
    n_exp = logits.shape[1]
    n_assign = n_tok * TOP_K
    flat_e = top_e.reshape(-1)
    order = jnp.argsort(flat_e, stable=True)
    e_sorted = flat_e[order]
    tok_sorted = (order // TOP_K).astype(jnp.int32)
    counts = jnp.zeros((n_exp,), jnp.int32).at[flat_e].add(1)
    padded = (counts + MOE_ROWS - 1) // MOE_ROWS * MOE_ROWS
    pad_end = jnp.cumsum(padded)
    pad_start = pad_end - padded
    start = jnp.cumsum(counts) - counts
    dest = (pad_start[e_sorted] + jnp.arange(n_assign) - start[e_sorted]).astype(jnp.int32)
    n_blocks = -(-n_assign // MOE_ROWS) + n_exp
    row_tok = jnp.full((n_blocks * MOE_ROWS,), n_tok, jnp.int32).at[dest].set(tok_sorted)
    block_e = jnp.minimum(jnp.searchsorted(pad_end, jnp.arange(n_blocks) * MOE_ROWS, side='right'),
                          n_exp - 1).astype(jnp.int32)
    n_used = (pad_end[-1] // MOE_ROWS).astype(jnp.int32).reshape(1)
    pos = jnp.zeros((n_assign,), jnp.int32).at[order].set(dest).reshape(n_tok, TOP_K)
    return gate, row_tok, block_e, n_used, pos


def kernel(x_prompt, x_sample, cache_k, cache_v, state_gdn, state_conv, page_table, norm_mix_g, w_in, rel_bias, conv_w, a_log, dt_bias, gdn_norm_g, w_branch_a, w_branch_b, w_out, norm_ffn_g, router_w, router_b, w_gate_up, b_gate_up, w_down, b_down, norm_final_g):
    n_p, seq, d = x_prompt.shape
    n_s, dseq, _ = x_sample.shape
    assert w_in.shape[0] == 1
    l = 0
    wa = H_A * HD_A
    wk = H_B * DK_B
    qkv_b = 3 * wk
    n_prow = n_p * seq
    n_srow = n_s * dseq
    n_tok = n_prow + n_srow
    past = page_table.shape[1] * PAGE_SIZE
    assert past % MOBA_BLOCK == 0 and seq % MOBA_BLOCK == 0

    x = jnp.concatenate([x_prompt.reshape(n_prow, d), x_sample.reshape(n_srow, d)], axis=0)

    w = w_in[l]
    c_z = 3 * wa + qkv_b
    c_ab = c_z + wk
    c_g = c_ab + 2 * H_B
    w_main = jnp.concatenate([w[:, :c_ab], w[:, c_g:]], axis=1).astype(BF16)
    w_ab = jnp.concatenate([w[:, c_ab:c_g], jnp.zeros((d, 128 - 2 * H_B), F32)], axis=1).astype(BF16)
    tm = 768
    proj = norm_matmul(x, norm_mix_g[l], w_main, tm, 1024)
    ab = norm_matmul(x, norm_mix_g[l], w_ab, tm, 128)

    k_all = proj[:, wa:2 * wa]
    v_all = proj[:, 2 * wa:3 * wa]

    nb = seq // MOBA_BLOCK
    dist = (jnp.arange(nb)[:, None, None] * MOBA_BLOCK + jnp.arange(MOBA_BLOCK)[None, :, None]
            - jnp.arange(MOBA_BLOCK)[None, None, :])
    bias_p = _rel_bias_table(rel_bias, dist)
    out_a_p = moba_prompt(proj, bias_p, n_p, seq, 0, H_A, 2 * H_A)

    q_s = proj[n_prow:, :wa].reshape(n_s, dseq, H_A, HD_A) * (HD_A ** -0.5)
    qbd = jnp.einsum('blhd,hg->bhlgd', q_s, jnp.eye(H_A, dtype=F32)).reshape(n_s, H_A * dseq, wa)
    q_pos = past + jnp.arange(dseq)
    bias_past = _rel_bias_table(rel_bias, q_pos[:, None] - jnp.arange(past)[None, :])
    bias_past = bias_past.reshape(H_A * dseq, past)
    bias_own = _rel_bias_table(rel_bias, jnp.arange(dseq)[:, None] - jnp.arange(dseq)[None, :])
    bias_own = bias_own.reshape(H_A * dseq, dseq)
    out_a_s = moba_sample(proj, n_prow // dseq, 1, 2, qbd, cache_k[l], cache_v[l], page_table,
                          bias_past, bias_own, dseq)
    out_a = jnp.concatenate([out_a_p, out_a_s], axis=0)

    qkv_colblk = 3 * wa // qkv_b
    assert qkv_colblk * qkv_b == 3 * wa
    tt = 512
    raw_p = proj[:n_prow, 3 * wa:3 * wa + qkv_b].reshape(n_p, seq // tt, tt, qkv_b)
    halo0 = jnp.zeros((n_p, 1, 8, qkv_b), F32)
    halo_p = jnp.concatenate([halo0, raw_p[:, :-1, tt - 8:, :]], axis=1).reshape(n_p * (seq // tt), 8, qkv_b)
    qp, kp, vp, gbp = gdn_prep(proj, ab, halo_p, conv_w[l], a_log[l], dt_bias[l],
                               0, n_p, seq, tt, qkv_colblk)
    raw_s = proj[n_prow:, 3 * wa:3 * wa + qkv_b].reshape(n_s, dseq, qkv_b)
    halo_s = jnp.concatenate([jnp.zeros((n_s, 8 - (CONV_WIDTH - 1), qkv_b), F32), state_conv[l]], axis=1)
    qs_, ks_, vs_, gbs = gdn_prep(proj, ab, halo_s, conv_w[l], a_log[l], dt_bias[l],
                                  n_prow, n_s, dseq, dseq, qkv_colblk)
    z_colblk = c_z // wk
    assert z_colblk * wk == c_z
    c_p = math.gcd(seq, GDN_CHUNK)
    out_b_p, s_p = gdn_chunks(qp, kp, vp, gbp, proj, z_colblk, 0, gdn_norm_g[l],
                              jnp.zeros((n_p, H_B, DK_B, DK_B), F32), n_p, c_p)
    c_s = math.gcd(dseq, GDN_CHUNK)
    out_b_s, s_s = gdn_chunks(qs_, ks_, vs_, gbs, proj, z_colblk, n_prow, gdn_norm_g[l],
                              state_gdn[l], 2, c_s)
    out_b = jnp.concatenate([out_b_p.reshape(n_prow, wk), out_b_s.reshape(n_srow, wk)], axis=0)
    conv_p = raw_p.reshape(n_p, seq, qkv_b)[:, seq - (CONV_WIDTH - 1):, :]
    conv_s = jnp.concatenate([state_conv[l], raw_s], axis=1)[:, dseq:, :]

    merged = branch_merge(out_a, out_b, w_branch_a[l].astype(BF16), w_branch_b[l].astype(BF16),
                          proj, c_ab, c_ab + d, tm, 1024)
    y1 = matmul_residual(merged, w_out[l].astype(BF16), x, tm, 1024)

    n_exp = router_w.shape[2]
    w_r = jnp.concatenate([router_w[l], jnp.zeros((d, 128 - n_exp), F32)], axis=1).astype(BF16)
    r_logits, h2 = norm_matmul(y1, norm_ffn_g[l], w_r, tm, 128, emit_h=True)
    logits = r_logits[:, :n_exp] + router_b[l].astype(F32)
    gate, row_tok, block_e, n_used, pos = _route(logits, n_tok)
    x_rows = jnp.concatenate([h2, jnp.zeros((1, d), BF16)], axis=0)[row_tok]
    wgu = w_gate_up[l]
    f = wgu.shape[2] // 2
    wg = wgu[:, :, 0::2].astype(BF16)
    wu = wgu[:, :, 1::2].astype(BF16)
    bg = b_gate_up[l][:, 0::2].reshape(n_exp, 1, f)
    bu = b_gate_up[l][:, 1::2].reshape(n_exp, 1, f)
    y_rows = moe_experts(x_rows, block_e, n_used, wg, wu, bg, bu,
                         w_down[l].astype(BF16), b_down[l].reshape(n_exp, 1, d), 1024, 1024)
    moe = jnp.sum(y_rows[pos] * gate[:, :, None], axis=1)
    y = residual_norm(y1, moe, norm_final_g, 384)

    y_prompt = y[:n_prow].reshape(n_p, seq, d)
    y_sample = y[n_prow:].reshape(n_s, dseq, d)
    k_p = k_all[:n_prow].reshape(1, n_p, seq, H_A, HD_A)
    v_p = v_all[:n_prow].reshape(1, n_p, seq, H_A, HD_A)
    k_s = k_all[n_prow:].reshape(1, n_s, dseq, H_A, HD_A)
    v_s = v_all[n_prow:].reshape(1, n_s, dseq, H_A, HD_A)
    return (y_prompt, y_sample, k_p, v_p, s_p[None], conv_p[None],
            k_s, v_s, s_s[None], conv_s[None])
```

```python
import functools
import math

import jax
import jax.numpy as jnp
from jax import lax
from jax.experimental import pallas as pl
from jax.experimental.pallas import tpu as pltpu

F32 = jnp.float32
BF16 = jnp.bfloat16
HIGHEST = lax.Precision.HIGHEST

RMS_EPS = 1e-6
NEG_INF = -1e30

H_A = 8
HD_A = 128
MOBA_BLOCK = 256
MOBA_TOPK = 3
PAGE_SIZE = 128
N_BUCKETS = 32
MAX_DISTANCE = 4096

H_B = 8
DK_B = 128
CONV_WIDTH = 4
GDN_CHUNK = 64

TOP_K = 4
SWIGLU_LIMIT = 7.0
SWIGLU_ALPHA = 1.702
MOE_ROWS = 256

VMEM_LIMIT = 56 * 1024 * 1024


def _cparams(sem):
    return pltpu.CompilerParams(dimension_semantics=sem, vmem_limit_bytes=VMEM_LIMIT)


def _dot(a, b, precision=None):
    return jnp.dot(a, b, preferred_element_type=F32, precision=precision)


def _dot_nt(a, b, precision=None):
    return lax.dot_general(a, b, (((1,), (1,)), ((), ())),
                           preferred_element_type=F32, precision=precision)


def _dot_tn(a, b, precision=None):
    return lax.dot_general(a, b, (((0,), (0,)), ((), ())),
                           preferred_element_type=F32, precision=precision)


def _norm_matmul_kernel(x_ref, g_ref, w_ref, o_ref, *rest, emit_h):
    if emit_h:
        h_out_ref, h_ref = rest
    else:
        (h_ref,) = rest

    @pl.when(pl.program_id(1) == 0)
    def _():
        x = x_ref[...]
        y = x * lax.rsqrt(jnp.mean(x * x, axis=-1, keepdims=True) + RMS_EPS)
        h_ref[...] = (y * g_ref[...]).astype(BF16)

    if emit_h:
        h_out_ref[...] = h_ref[...]
    o_ref[...] = _dot(h_ref[...], w_ref[...])


def norm_matmul(x, g, w, tm, tn, emit_h=False):
    m, k = x.shape
    n = w.shape[1]
    out_shape = [jax.ShapeDtypeStruct((m, n), F32)]
    out_specs = [pl.BlockSpec((tm, tn), lambda i, j: (i, j))]
    if emit_h:
        out_shape.append(jax.ShapeDtypeStruct((m, k), BF16))
        out_specs.append(pl.BlockSpec((tm, k), lambda i, j: (i, 0)))
    res = pl.pallas_call(
        functools.partial(_norm_matmul_kernel, emit_h=emit_h),
        grid=(m // tm, n // tn),
        in_specs=[pl.BlockSpec((tm, k), lambda i, j: (i, 0)),
                  pl.BlockSpec((1, k), lambda i, j: (0, 0)),
                  pl.BlockSpec((k, tn), lambda i, j: (0, j))],
        out_specs=out_specs,
        out_shape=out_shape,
        scratch_shapes=[pltpu.VMEM((tm, k), BF16)],
        compiler_params=_cparams(("parallel", "arbitrary")),
        name="norm_matmul",
    )(x, g.reshape(1, k), w)
    return res if emit_h else res[0]


def _moba_prompt_kernel(q_ref, k_ref, v_ref, bias_ref, o_ref,
                        kb_ref, vb_ref, kmean_ref, m_ref, l_ref, acc_ref,
                        *, n_blocks):
    i = pl.program_id(2)
    blk = MOBA_BLOCK
    seq_len = n_blocks * blk

    @pl.when(i == 0)
    def _():
        k = k_ref[...]
        kb_ref[:, :HD_A] = k.astype(BF16)
        rblk = lax.broadcasted_iota(jnp.int32, (seq_len, HD_A), 0) // blk
        lane = lax.broadcasted_iota(jnp.int32, (seq_len, HD_A), 1)
        kb_ref[:, HD_A:] = jnp.where(lane == rblk, 1.0, 0.0).astype(BF16)
        vb_ref[...] = v_ref[...].astype(BF16)
        kmean_ref[...] = jnp.mean(k.reshape(n_blocks, blk, HD_A), axis=1)

    qs = q_ref[...] * (HD_A ** -0.5)
    qb = qs.astype(BF16)

    st = _dot_nt(kmean_ref[...], qs, precision=HIGHEST)
    rowi = lax.broadcasted_iota(jnp.int32, st.shape, 0)
    st = jnp.where(rowi < i, st, NEG_INF)
    sel = jnp.zeros(st.shape, F32)
    for c in range(n_blocks - 1):
        sc = st[c:c + 1, :]
        ahead = jnp.where((st > sc) | ((st == sc) & (rowi < c)), 1.0, 0.0)
        rank = jnp.sum(ahead, axis=0, keepdims=True)
        sel = jnp.where((rowi == c) & (rank < MOBA_TOPK), 1.0, sel)
    sel = jnp.where(rowi < i, sel, 0.0)
    er = lax.broadcasted_iota(jnp.int32, (n_blocks, HD_A), 0)
    ec = lax.broadcasted_iota(jnp.int32, (n_blocks, HD_A), 1)
    selq = _dot_tn(sel, jnp.where(er == ec, 1.0, 0.0))
    lane = lax.broadcasted_iota(jnp.int32, selq.shape, 1)
    negm = jnp.where((lane < n_blocks) & (selq < 0.5), NEG_INF, 0.0)
    q_aug = jnp.concatenate([qb, negm.astype(BF16)], axis=1)

    row0 = pl.multiple_of(i * blk, blk)
    r = lax.broadcasted_iota(jnp.int32, (blk, blk), 0)
    c_ = lax.broadcasted_iota(jnp.int32, (blk, blk), 1)
    logits = _dot_nt(qb, kb_ref[pl.ds(row0, blk), :HD_A]) + bias_ref[0, 0]
    logits = jnp.where(c_ <= r, logits, NEG_INF)
    m0 = jnp.max(logits, axis=-1, keepdims=True)
    p = jnp.exp(logits - m0)
    m_ref[...] = m0
    l_ref[...] = jnp.sum(p, axis=-1, keepdims=True)
    acc_ref[...] = _dot(p.astype(BF16), vb_ref[pl.ds(row0, blk), :])

    def pair(t, carry):
        j0 = 2 * t
        rows = pl.ds(pl.multiple_of(j0 * blk, 2 * blk), 2 * blk)
        lg = _dot_nt(q_aug, kb_ref[rows, :])
        lg = lg + jnp.concatenate([bias_ref[0, i - j0], bias_ref[0, i - j0 - 1]], axis=1)
        m_old = m_ref[...]
        m_new = jnp.maximum(m_old, jnp.max(lg, axis=-1, keepdims=True))
        alpha = jnp.exp(m_old - m_new)
        pj = jnp.exp(lg - m_new)
        m_ref[...] = m_new
        l_ref[...] = alpha * l_ref[...] + jnp.sum(pj, axis=-1, keepdims=True)
        acc_ref[...] = alpha * acc_ref[...] + _dot(pj.astype(BF16), vb_ref[rows, :])
        return carry

    lax.fori_loop(0, (i + 1) // 2, pair, 0)
    o_ref[...] = (acc_ref[...] / l_ref[...]).astype(o_ref.dtype)


def moba_prompt(proj, bias_tab, n_seq, seq_len, q_col, k_col, v_col):
    blk = MOBA_BLOCK
    nb = seq_len // blk
    return pl.pallas_call(
        functools.partial(_moba_prompt_kernel, n_blocks=nb),
        grid=(H_A, n_seq, nb),
        in_specs=[pl.BlockSpec((blk, HD_A), lambda h, n, i: (n * nb + i, q_col + h)),
                  pl.BlockSpec((seq_len, HD_A), lambda h, n, i: (n, k_col + h)),
                  pl.BlockSpec((seq_len, HD_A), lambda h, n, i: (n, v_col + h)),
                  pl.BlockSpec((1, nb, blk, blk), lambda h, n, i: (h, 0, 0, 0))],
        out_specs=pl.BlockSpec((blk, HD_A), lambda h, n, i: (n * nb + i, h)),
        out_shape=jax.ShapeDtypeStruct((n_seq * seq_len, H_A * HD_A), BF16),
        scratch_shapes=[pltpu.VMEM((seq_len, 2 * HD_A), BF16),
                        pltpu.VMEM((seq_len, HD_A), BF16),
                        pltpu.VMEM((nb, HD_A), F32),
                        pltpu.VMEM((blk, 1), F32),
                        pltpu.VMEM((blk, 1), F32),
                        pltpu.VMEM((blk, HD_A), F32)],
        compiler_params=_cparams(("arbitrary", "arbitrary", "arbitrary")),
        name="moba_prompt",
    )(proj, proj, proj, bias_tab)


PAGES_PER_STEP = 16
BLOCKS_PER_STEP = PAGES_PER_STEP * PAGE_SIZE // MOBA_BLOCK


def _moba_sample_keys_kernel(pt_ref, qbd_ref, knew_ref, bias_ref, bown_ref, *rest,
                             n_steps, n_tok):
    pps = PAGES_PER_STEP
    bps = BLOCKS_PER_STEP
    ppb = pps // bps
    kp = rest[:pps]
    p_ref, pown_ref, lg_ref, ksum_ref = rest[pps:]
    s = pl.program_id(1)
    qbd = qbd_ref[0]

    def lg_block(b):
        return b // bps, slice((b % bps) * MOBA_BLOCK, (b % bps + 1) * MOBA_BLOCK)

    sums = []
    for k in range(pps):
        page = kp[k][0]
        lg = _dot_nt(qbd, page) + bias_ref[:, k * PAGE_SIZE:(k + 1) * PAGE_SIZE]
        lg_ref[s, :, k * PAGE_SIZE:(k + 1) * PAGE_SIZE] = lg
        sums.append(jnp.sum(page, axis=0, keepdims=True))
    blocks = [sum(sums[ppb * b:ppb * (b + 1)]) * (1.0 / MOBA_BLOCK) for b in range(bps)]
    ksum_ref[pl.ds(pl.multiple_of(s * bps, bps), bps), :] = jnp.concatenate(blocks, axis=0)

    @pl.when(s == n_steps - 1)
    def _():
        n_blk = n_steps * bps
        sc = _dot_nt(qbd, ksum_ref[...], precision=HIGHEST)
        bidx = lax.broadcasted_iota(jnp.int32, sc.shape, 1)
        sel = jnp.zeros(sc.shape, F32)
        for _ in range(MOBA_TOPK):
            mx = jnp.max(sc, axis=-1, keepdims=True)
            first = jnp.min(jnp.where(sc == mx, bidx, n_blk), axis=-1, keepdims=True)
            hit = bidx == first
            sel = jnp.where(hit, 1.0, sel)
            sc = jnp.where(hit, 2.0 * NEG_INF, sc)

        lo = _dot_nt(qbd, knew_ref[...]) + bown_ref[...]
        kc = lax.broadcasted_iota(jnp.int32, lo.shape, 1)
        qr = lax.broadcasted_iota(jnp.int32, lo.shape, 0) % n_tok
        lo = jnp.where(kc <= qr, lo, NEG_INF)
        m = jnp.max(lo, axis=-1, keepdims=True)

        for b in range(n_blk):
            st, sl = lg_block(b)
            x = jnp.where(sel[:, b:b + 1] > 0.0, lg_ref[st, :, sl], NEG_INF)
            lg_ref[st, :, sl] = x
            m = jnp.maximum(m, jnp.max(x, axis=-1, keepdims=True))
        e_own = jnp.exp(lo - m)
        l = jnp.sum(e_own, axis=-1, keepdims=True)
        for b in range(n_blk):
            st, sl = lg_block(b)
            e = jnp.exp(lg_ref[st, :, sl] - m)
            lg_ref[st, :, sl] = e
            l = l + jnp.sum(e, axis=-1, keepdims=True)
        inv = 1.0 / l
        pown_ref[0] = e_own * inv
        for b in range(n_blk):
            st, sl = lg_block(b)
            p_ref[0, :, b * MOBA_BLOCK:(b + 1) * MOBA_BLOCK] = (lg_ref[st, :, sl] * inv).astype(BF16)


def _moba_sample_values_kernel(pt_ref, p_ref, pown_ref, vnew_ref, *rest, n_steps, n_tok):
    pps = PAGES_PER_STEP
    vp = rest[:pps]
    o_ref, acc_ref = rest[pps:]
    s = pl.program_id(1)

    @pl.when(s == 0)
    def _():
        acc_ref[...] = _dot(pown_ref[0], vnew_ref[...])

    acc = acc_ref[...]
    for k in range(pps):
        acc = acc + _dot(p_ref[0, :, k * PAGE_SIZE:(k + 1) * PAGE_SIZE], vp[k][0].astype(BF16))
    acc_ref[...] = acc

    @pl.when(s == n_steps - 1)
    def _():
        for h in range(H_A):
            o_ref[:, h * HD_A:(h + 1) * HD_A] = acc[h * n_tok:(h + 1) * n_tok,
                                                    h * HD_A:(h + 1) * HD_A].astype(o_ref.dtype)


def moba_sample(proj, row_blk0, k_colblk, v_colblk, qbd, cache_k, cache_v, page_table,
                bias_past, bias_own, n_tok):
    n_seq, n_pages = page_table.shape
    pps = PAGES_PER_STEP
    n_steps = n_pages // pps
    n_past = n_pages * PAGE_SIZE
    wa = H_A * HD_A
    nrow = H_A * n_tok
    ck = cache_k.reshape(-1, PAGE_SIZE, wa)
    cv = cache_v.reshape(-1, PAGE_SIZE, wa)

    def pmap(k):
        return lambda b, s, pt: (pt[b, s * pps + k], 0, 0)

    p, p_own = pl.pallas_call(
        functools.partial(_moba_sample_keys_kernel, n_steps=n_steps, n_tok=n_tok),
        grid_spec=pltpu.PrefetchScalarGridSpec(
            num_scalar_prefetch=1,
            grid=(n_seq, n_steps),
            in_specs=[pl.BlockSpec((1, nrow, wa), lambda b, s, pt: (b, 0, 0)),
                      pl.BlockSpec((n_tok, wa), lambda b, s, pt: (row_blk0 + b, k_colblk)),
                      pl.BlockSpec((nrow, pps * PAGE_SIZE), lambda b, s, pt: (0, s)),
                      pl.BlockSpec((nrow, n_tok), lambda b, s, pt: (0, 0))]
            + [pl.BlockSpec((1, PAGE_SIZE, wa), pmap(k)) for k in range(pps)],
            out_specs=[pl.BlockSpec((1, nrow, n_past), lambda b, s, pt: (b, 0, 0)),
                       pl.BlockSpec((1, nrow, n_tok), lambda b, s, pt: (b, 0, 0))],
            scratch_shapes=[pltpu.VMEM((n_steps, nrow, pps * PAGE_SIZE), F32),
                            pltpu.VMEM((n_past // MOBA_BLOCK, wa), F32)]),
        out_shape=[jax.ShapeDtypeStruct((n_seq, nrow, n_past), BF16),
                   jax.ShapeDtypeStruct((n_seq, nrow, n_tok), F32)],
        compiler_params=_cparams(("arbitrary", "arbitrary")),
        name="moba_sample_keys",
    )(page_table, qbd, proj, bias_past, bias_own, *([ck] * pps))

    return pl.pallas_call(
        functools.partial(_moba_sample_values_kernel, n_steps=n_steps, n_tok=n_tok),
        grid_spec=pltpu.PrefetchScalarGridSpec(
            num_scalar_prefetch=1,
            grid=(n_seq, n_steps),
            in_specs=[pl.BlockSpec((1, nrow, pps * PAGE_SIZE), lambda b, s, pt: (b, 0, s)),
                      pl.BlockSpec((1, nrow, n_tok), lambda b, s, pt: (b, 0, 0)),
                      pl.BlockSpec((n_tok, wa), lambda b, s, pt: (row_blk0 + b, v_colblk))]
            + [pl.BlockSpec((1, PAGE_SIZE, wa), pmap(k)) for k in range(pps)],
            out_specs=pl.BlockSpec((n_tok, wa), lambda b, s, pt: (b, 0)),
            scratch_shapes=[pltpu.VMEM((nrow, wa), F32)]),
        out_shape=jax.ShapeDtypeStruct((n_seq * n_tok, wa), BF16),
        compiler_params=_cparams(("arbitrary", "arbitrary")),
        name="moba_sample_values",
    )(page_table, p, p_own, proj, *([cv] * pps))


def _gdn_prep_kernel(x_ref, halo_ref, ab_ref, cw_ref, alog_ref, dtb_ref,
                     q_ref, k_ref, v_ref, gb_ref, *, tt):
    x = x_ref[...]
    xf = jnp.concatenate([halo_ref[0], x], axis=0)
    cw = cw_ref[...]
    conv = xf[5:5 + tt] * cw[0:1]
    for w in range(1, CONV_WIDTH):
        conv = conv + xf[5 + w:5 + w + tt] * cw[w:w + 1]
    act = conv * jax.nn.sigmoid(conv)
    wk = H_B * DK_B
    for h in range(H_B):
        q = act[:, h * DK_B:(h + 1) * DK_B]
        k = act[:, wk + h * DK_B:wk + (h + 1) * DK_B]
        q_ref[0, h] = q * lax.rsqrt(jnp.sum(q * q, axis=-1, keepdims=True) + RMS_EPS) * (DK_B ** -0.5)
        k_ref[0, h] = k * lax.rsqrt(jnp.sum(k * k, axis=-1, keepdims=True) + RMS_EPS)
        v_ref[0, h] = act[:, 2 * wk + h * DK_B:2 * wk + (h + 1) * DK_B]
    ab = ab_ref[...]
    t = ab + dtb_ref[...]
    sp = jnp.maximum(t, 0.0) + jnp.log(1.0 + jnp.exp(-jnp.abs(t)))
    g = -jnp.exp(alog_ref[...]) * sp
    lane = lax.broadcasted_iota(jnp.int32, ab.shape, 1)
    gb_ref[0] = jnp.where(lane < H_B, g, jax.nn.sigmoid(ab))


def gdn_prep(proj, ab, halo, conv_w, a_log, dt_bias, row0, n_seq, seq_len, tt, qkv_colblk):
    w3 = conv_w.shape[1]
    nt = seq_len // tt
    rb0 = row0 // tt
    pad = jnp.zeros((1, 128 - H_B), F32)
    alog = jnp.concatenate([a_log.reshape(1, H_B), pad], axis=1)
    dtb = jnp.concatenate([dt_bias.reshape(1, H_B), pad], axis=1)
    hm = jax.ShapeDtypeStruct((n_seq, H_B, seq_len, DK_B), F32)
    hspec = pl.BlockSpec((1, H_B, tt, DK_B), lambda n, t: (n, 0, t, 0))
    return pl.pallas_call(
        functools.partial(_gdn_prep_kernel, tt=tt),
        grid=(n_seq, nt),
        in_specs=[pl.BlockSpec((tt, w3), lambda n, t: (rb0 + n * nt + t, qkv_colblk)),
                  pl.BlockSpec((1, 8, w3), lambda n, t: (n * nt + t, 0, 0)),
                  pl.BlockSpec((tt, 128), lambda n, t: (rb0 + n * nt + t, 0)),
                  pl.BlockSpec((CONV_WIDTH, w3), lambda n, t: (0, 0)),
                  pl.BlockSpec((1, 128), lambda n, t: (0, 0)),
                  pl.BlockSpec((1, 128), lambda n, t: (0, 0))],
        out_specs=[hspec, hspec, hspec,
                   pl.BlockSpec((1, tt, 128), lambda n, t: (n, t, 0))],
        out_shape=[hm, hm, hm, jax.ShapeDtypeStruct((n_seq, seq_len, 128), F32)],
        compiler_params=_cparams(("parallel", "parallel")),
        name="gdn_prep",
    )(proj, halo, ab, conv_w, alog, dtb)


def _split2(a):
    hi = a.astype(BF16)
    return hi, (a - hi.astype(F32)).astype(BF16)


def _split3(a):
    hi = a.astype(BF16)
    r1 = a - hi.astype(F32)
    mid = r1.astype(BF16)
    return hi, mid, (r1 - mid.astype(F32)).astype(BF16)


def _dot3(a, b):
    ah, al = _split2(a)
    bh, bl = _split2(b)
    return _dot(ah, bh) + (_dot(ah, bl) + _dot(al, bh))


def _dot_exact_lhs(a_bf16, b):
    b1, b2, b3 = _split3(b)
    return _dot(a_bf16, b1) + (_dot(a_bf16, b2) + _dot(a_bf16, b3))


def _unit_lower_inverse(a_list, c):
    r = lax.broadcasted_iota(jnp.int32, (c, c), 0)
    cc = lax.broadcasted_iota(jnp.int32, (c, c), 1)
    eye = jnp.where(r == cc, 1.0, 0.0)
    base = min(c, 8)
    n1 = [jnp.where((r // base) == (cc // base), -a, 0.0) for a in a_list]
    n2 = [_dot3(x, x) for x in n1]
    n4 = [_dot3(x, x) for x in n2]
    t = [_dot3(eye + x, eye + y) for x, y in zip(n1, n2)]
    t = [_dot3(x, eye + y) for x, y in zip(t, n4)]
    b = base
    while b < c:
        inner = ((r // (2 * b)) == (cc // (2 * b))) & ((r // b) != (cc // b))
        left = [_dot3(x, jnp.where(inner, a, 0.0)) for x, a in zip(t, a_list)]
        t = [x - _dot3(y, x) for x, y in zip(t, left)]
        b *= 2
    return t


def _gdn_chunk_kernel(*refs, nb, c):
    q_ref, k_ref, v_ref, gb_ref = refs[:4]
    z_refs = refs[4:4 + nb]
    ng_ref, s0_ref, o_ref, sout_ref, s_ref = refs[4 + nb:]
    ci = pl.program_id(1)

    @pl.when(ci == 0)
    def _():
        s_ref[...] = s0_ref[...]

    r = lax.broadcasted_iota(jnp.int32, (c, c), 0)
    cc = lax.broadcasted_iota(jnp.int32, (c, c), 1)
    incl = r >= cc
    strict = r > cc
    ltri = jnp.where(incl, 1.0, 0.0).astype(BF16)
    eye = jnp.where(r == cc, 1.0, 0.0)
    ones = jnp.ones((c, c), BF16)
    ng = ng_ref[...]
    ch = [(n, h) for n in range(nb) for h in range(H_B)]
    gbv = [gb_ref[n] for n in range(nb)]
    q = [q_ref[n, h] for n, h in ch]
    k = [k_ref[n, h] for n, h in ch]
    v = [v_ref[n, h] for n, h in ch]
    gcol = [jnp.broadcast_to(gbv[n][:, h:h + 1], (c, DK_B)) for n, h in ch]
    bcol = [jnp.broadcast_to(gbv[n][:, H_B + h:H_B + h + 1], (c, DK_B)) for n, h in ch]
    big_g = [_dot_exact_lhs(ltri, g) for g in gcol]
    gj = [_dot_exact_lhs(ones, g[:, :c] * eye) for g in big_g]
    decay = [jnp.where(incl, jnp.exp(jnp.minimum(g[:, :c] - x, 0.0)), 0.0) for g, x in zip(big_g, gj)]
    kk = [_dot_nt(x, x) for x in k]
    a = [jnp.where(strict, b[:, :c] * x * d, 0.0) for b, x, d in zip(bcol, kk, decay)]
    t = _unit_lower_inverse(a, c)
    gam = [jnp.exp(g) for g in big_g]
    u = [_dot3(x, b * y) for x, b, y in zip(t, bcol, v)]
    w = [_dot3(x, b * g * y) for x, b, g, y in zip(t, bcol, gam, k)]
    qk = [_dot_nt(x, y) * d for x, y, d in zip(q, k, decay)]
    g_last = [g[c - 1:c, :] for g in big_g]
    k_tail = [jnp.exp(gl - g) * y for gl, g, y in zip(g_last, big_g, k)]
    s = [s_ref[n, h] for n, h in ch]
    uu = [x - _dot(y, z_) for x, y, z_ in zip(u, w, s)]
    o = [_dot(g * x, z_) + _dot(y, x2) for g, x, z_, y, x2 in zip(gam, q, s, qk, uu)]
    s_new = [jnp.exp(gl) * z_ + _dot_tn(y, x) for gl, z_, y, x in zip(g_last, s, k_tail, uu)]
    for (n, h), x in zip(ch, s_new):
        s_ref[n, h] = x
    for (n, h), x in zip(ch, o):
        x = x * lax.rsqrt(jnp.mean(x * x, axis=-1, keepdims=True) + RMS_EPS) * ng
        zh = z_refs[n][:, h * DK_B:(h + 1) * DK_B]
        o_ref[n, :, h * DK_B:(h + 1) * DK_B] = (x * (zh * jax.nn.sigmoid(zh))).astype(o_ref.dtype)

    @pl.when(ci == pl.num_programs(1) - 1)
    def _():
        sout_ref[...] = s_ref[...]


def gdn_chunks(q, k, v, gb, proj, z_colblk, row0, norm_g, s0, nb, c):
    n_seq, _, seq_len, _ = q.shape
    nc = seq_len // c
    wv = H_B * DK_B
    rb0 = row0 // c
    hspec = pl.BlockSpec((nb, H_B, c, DK_B), lambda n, ci: (n, 0, ci, 0))
    sspec = pl.BlockSpec((nb, H_B, DK_B, DK_B), lambda n, ci: (n, 0, 0, 0))

    def zmap(j):
        return lambda n, ci: (rb0 + (n * nb + j) * nc + ci, z_colblk)

    return pl.pallas_call(
        functools.partial(_gdn_chunk_kernel, nb=nb, c=c),
        grid=(n_seq // nb, nc),
        in_specs=[hspec, hspec, hspec,
                  pl.BlockSpec((nb, c, 128), lambda n, ci: (n, ci, 0))]
        + [pl.BlockSpec((c, wv), zmap(j)) for j in range(nb)]
        + [pl.BlockSpec((1, DK_B), lambda n, ci: (0, 0)), sspec],
        out_specs=[pl.BlockSpec((nb, c, wv), lambda n, ci: (n, ci, 0)), sspec],
        out_shape=[jax.ShapeDtypeStruct((n_seq, seq_len, wv), BF16),
                   jax.ShapeDtypeStruct((n_seq, H_B, DK_B, DK_B), F32)],
        scratch_shapes=[pltpu.VMEM((nb, H_B, DK_B, DK_B), F32)],
        compiler_params=_cparams(("parallel", "arbitrary")),
        name="gdn_chunks",
    )(q, k, v, gb, *([proj] * nb), norm_g.reshape(1, DK_B), s0)


def _merge_kernel(oa_ref, ob_ref, wa_ref, wb_ref, ga_ref, gb_ref, o_ref):
    ya = _dot(oa_ref[...], wa_ref[...])
    yb = _dot(ob_ref[...], wb_ref[...])
    o_ref[...] = (jax.nn.sigmoid(ga_ref[...]) * ya
                  + jax.nn.sigmoid(gb_ref[...]) * yb).astype(o_ref.dtype)


def branch_merge(out_a, out_b, wa, wb, proj, ga_col0, gb_col0, tm, tn):
    m, ka = out_a.shape
    n = wa.shape[1]
    ga_blk, gb_blk = ga_col0 // tn, gb_col0 // tn
    assert ga_blk * tn == ga_col0 and gb_blk * tn == gb_col0
    return pl.pallas_call(
        _merge_kernel,
        grid=(m // tm, n // tn),
        in_specs=[pl.BlockSpec((tm, ka), lambda i, j: (i, 0)),
                  pl.BlockSpec((tm, ka), lambda i, j: (i, 0)),
                  pl.BlockSpec((ka, tn), lambda i, j: (0, j)),
                  pl.BlockSpec((ka, tn), lambda i, j: (0, j)),
                  pl.BlockSpec((tm, tn), lambda i, j: (i, ga_blk + j)),
                  pl.BlockSpec((tm, tn), lambda i, j: (i, gb_blk + j))],
        out_specs=pl.BlockSpec((tm, tn), lambda i, j: (i, j)),
        out_shape=jax.ShapeDtypeStruct((m, n), BF16),
        compiler_params=_cparams(("parallel", "parallel")),
        name="branch_merge",
    )(out_a, out_b, wa, wb, proj, proj)


def _matmul_residual_kernel(x_ref, w_ref, r_ref, o_ref):
    o_ref[...] = r_ref[...] + _dot(x_ref[...], w_ref[...])


def matmul_residual(x, w, res, tm, tn):
    m, k = x.shape
    n = w.shape[1]
    return pl.pallas_call(
        _matmul_residual_kernel,
        grid=(m // tm, n // tn),
        in_specs=[pl.BlockSpec((tm, k), lambda i, j: (i, 0)),
                  pl.BlockSpec((k, tn), lambda i, j: (0, j)),
                  pl.BlockSpec((tm, tn), lambda i, j: (i, j))],
        out_specs=pl.BlockSpec((tm, tn), lambda i, j: (i, j)),
        out_shape=jax.ShapeDtypeStruct((m, n), F32),
        compiler_params=_cparams(("parallel", "parallel")),
        name="out_proj",
    )(x, w, res)


MOE_TA = 256


def _lane_rotate(x, shift):
    parts = [pltpu.roll(x[:, c:c + 128], shift, 1) for c in range(0, x.shape[1], 128)]
    return parts[0] if len(parts) == 1 else jnp.concatenate(parts, axis=1)


def _new_expert(be_ref, b):
    return (b == 0) | (be_ref[b] != be_ref[jnp.maximum(b - 1, 0)])


def _moe_up_kernel(be_ref, nu_ref, x_ref, wa_ref, wb_ref, ba_ref, bb_ref, o_ref, wab_ref, wbb_ref):
    b = pl.program_id(1)

    @pl.when(b < nu_ref[0])
    def _():
        @pl.when(_new_expert(be_ref, b))
        def _():
            wab_ref[...] = wa_ref[0].astype(BF16)
            wbb_ref[...] = wb_ref[0].astype(BF16)

        x = x_ref[...]
        ga = _dot(x, wab_ref[...]) + ba_ref[0]
        gb = _dot(x, wbb_ref[...]) + bb_ref[0]
        even = (lax.broadcasted_iota(jnp.int32, ga.shape, 1) % 2) == 0
        gate = jnp.where(even, ga, _lane_rotate(gb, 1))
        up = jnp.where(even, _lane_rotate(ga, 127), gb)
        gl = jnp.minimum(gate, SWIGLU_LIMIT)
        up = jnp.clip(up, -SWIGLU_LIMIT, SWIGLU_LIMIT)
        o_ref[...] = (gl * jax.nn.sigmoid(SWIGLU_ALPHA * gl) * (up + 1.0)).astype(o_ref.dtype)


def _bf16_bits(w):
    return lax.bitcast_convert_type(w.astype(BF16).astype(F32), jnp.uint32)


def _moe_down_kernel(be_ref, nu_ref, a_ref, wd_ref, bd_ref, o_ref, wdb_ref):
    b = pl.program_id(1)

    @pl.when(b < nu_ref[0])
    def _():
        @pl.when(_new_expert(be_ref, b))
        def _():
            f = wd_ref.shape[1]
            for n in range(f // (2 * MOE_TA)):
                wa = wd_ref[0, n * MOE_TA:(n + 1) * MOE_TA, :]
                wb = wd_ref[0, f // 2 + n * MOE_TA:f // 2 + (n + 1) * MOE_TA, :]
                pair = (_bf16_bits(wa) >> 16) | _bf16_bits(wb)
                wdb_ref[n * 2 * MOE_TA:(n + 1) * 2 * MOE_TA, :] = pltpu.bitcast(pair, BF16)

        o_ref[...] = _dot(a_ref[...], wdb_ref[...]) + bd_ref[0]


def moe_experts(x_rows, block_e, n_used, w_gate_up, b_gate_up, w_down, b_down, tn_down):
    rows, d = x_rows.shape
    n_exp, _, f2 = w_gate_up.shape
    f = f2 // 2
    dm = w_down.shape[2]
    nblk = rows // MOE_ROWS
    nt = f // (2 * MOE_TA)
    tw = 2 * MOE_TA

    def rowmap(n, b, be, nu):
        return (jnp.minimum(b, nu[0] - 1), 0)

    def outmap(n, b, be, nu):
        return (jnp.minimum(b, nu[0] - 1), n)

    def wmap(off):
        return lambda n, b, be, nu: (be[jnp.minimum(b, nu[0] - 1)], 0, off + n)

    bgu = b_gate_up.reshape(n_exp, 1, f2)
    act = pl.pallas_call(
        _moe_up_kernel,
        grid_spec=pltpu.PrefetchScalarGridSpec(
            num_scalar_prefetch=2,
            grid=(nt, nblk),
            in_specs=[pl.BlockSpec((MOE_ROWS, d), rowmap),
                      pl.BlockSpec((1, d, tw), wmap(0)),
                      pl.BlockSpec((1, d, tw), wmap(nt)),
                      pl.BlockSpec((1, 1, tw), wmap(0)),
                      pl.BlockSpec((1, 1, tw), wmap(nt))],
            out_specs=pl.BlockSpec((MOE_ROWS, tw), outmap),
            scratch_shapes=[pltpu.VMEM((d, tw), BF16), pltpu.VMEM((d, tw), BF16)]),
        out_shape=jax.ShapeDtypeStruct((rows, f), BF16),
        compiler_params=_cparams(("arbitrary", "arbitrary")),
        name="moe_up",
    )(block_e, n_used, x_rows, w_gate_up, w_gate_up, bgu, bgu)
    return pl.pallas_call(
        _moe_down_kernel,
        grid_spec=pltpu.PrefetchScalarGridSpec(
            num_scalar_prefetch=2,
            grid=(dm // tn_down, nblk),
            in_specs=[pl.BlockSpec((MOE_ROWS, f), rowmap),
                      pl.BlockSpec((1, f, tn_down), wmap(0)),
                      pl.BlockSpec((1, 1, tn_down), wmap(0))],
            out_specs=pl.BlockSpec((MOE_ROWS, tn_down), outmap),
            scratch_shapes=[pltpu.VMEM((f, tn_down), BF16)]),
        out_shape=jax.ShapeDtypeStruct((rows, dm), F32),
        compiler_params=_cparams(("arbitrary", "arbitrary")),
        name="moe_down",
    )(block_e, n_used, act, w_down, b_down.reshape(n_exp, 1, dm))


def _final_kernel(y_ref, e_ref, gate_ref, g_ref, o_ref):
    d = y_ref.shape[1]
    x = y_ref[...]
    for kk in range(TOP_K):
        x = x + e_ref[:, kk * d:(kk + 1) * d] * gate_ref[:, kk:kk + 1]
    o_ref[...] = x * lax.rsqrt(jnp.mean(x * x, axis=-1, keepdims=True) + RMS_EPS) * g_ref[...]


def combine_residual_norm(y, expert_out, gate, g, tm):
    m, d = y.shape
    return pl.pallas_call(
        _final_kernel,
        grid=(m // tm,),
        in_specs=[pl.BlockSpec((tm, d), lambda i: (i, 0)),
                  pl.BlockSpec((tm, TOP_K * d), lambda i: (i, 0)),
                  pl.BlockSpec((tm, TOP_K), lambda i: (i, 0)),
                  pl.BlockSpec((1, d), lambda i: (0, 0))],
        out_specs=pl.BlockSpec((tm, d), lambda i: (i, 0)),
        out_shape=jax.ShapeDtypeStruct((m, d), F32),
        compiler_params=_cparams(("parallel",)),
        name="combine_norm",
    )(y, expert_out, gate, g.reshape(1, d))


def _rel_bias_table(rel_bias, dist):
    n = jnp.maximum(dist, 0)
    max_exact = N_BUCKETS // 2
    nf = jnp.maximum(n, max_exact).astype(F32)
    large = max_exact + (jnp.log(nf / max_exact) / math.log(MAX_DISTANCE / max_exact)
                         * (N_BUCKETS - max_exact)).astype(jnp.int32)
    bucket = jnp.where(n < max_exact, n, jnp.minimum(large, N_BUCKETS - 1))
    return rel_bias[:, bucket].astype(F32)


def _toeplitz_bias(rel_bias, nb):
    blk = MOBA_BLOCK
    n_heads = rel_bias.shape[0]
    span = nb * blk
    period = span + blk
    by_dist = _rel_bias_table(rel_bias, jnp.arange(period) - (blk - 1))
    strip = jnp.tile(by_dist, (1, blk + 1))[:, :blk * (period + 1)]
    strip = strip.reshape(n_heads, blk, period + 1)[:, :, :span]
    tiles = strip.reshape(n_heads, blk, nb, blk)[:, :, :, ::-1]
    return tiles.transpose(0, 2, 1, 3)


def _route(logits, n_tok):
    top_v, top_e = lax.top_k(logits, TOP_K)
    gate = jax.nn.softmax(top_v, axis=-1)
    n_exp = logits.shape[1]
    n_assign = n_tok * TOP_K
    flat_e = top_e.reshape(-1)
    order = jnp.argsort(flat_e, stable=True)
    e_sorted = flat_e[order]
    tok_sorted = (order // TOP_K).astype(jnp.int32)
    counts = jnp.zeros((n_exp,), jnp.int32).at[flat_e].add(1)
    padded = (counts + MOE_ROWS - 1) // MOE_ROWS * MOE_ROWS
    pad_end = jnp.cumsum(padded)
    pad_start = pad_end - padded
    start = jnp.cumsum(counts) - counts
    dest = (pad_start[e_sorted] + jnp.arange(n_assign) - start[e_sorted]).astype(jnp.int32)
    n_blocks = -(-n_assign // MOE_ROWS) + n_exp
    row_tok = jnp.full((n_blocks * MOE_ROWS,), n_tok, jnp.int32).at[dest].set(tok_sorted)
    block_e = jnp.minimum(jnp.searchsorted(pad_end, jnp.arange(n_blocks) * MOE_ROWS, side='right'),
                          n_exp - 1).astype(jnp.int32)
    n_used = (pad_end[-1] // MOE_ROWS).astype(jnp.int32).reshape(1)
    pos = jnp.zeros((n_assign,), jnp.int32).at[order].set(dest).reshape(n_tok, TOP_K)
    return gate, row_tok, block_e, n_used, pos


def kernel(x_prompt, x_sample, cache_k, cache_v, state_gdn, state_conv, page_table, norm_mix_g, w_in, rel_bias, conv_w, a_log, dt_bias, gdn_norm_g, w_branch_a, w_branch_b, w_out, norm_ffn_g, router_w, router_b, w_gate_up, b_gate_up, w_down, b_down, norm_final_g):
    n_p, seq, d = x_prompt.shape
    n_s, dseq, _ = x_sample.shape
    assert w_in.shape[0] == 1
    l = 0
    wa = H_A * HD_A
    wk = H_B * DK_B
    qkv_b = 3 * wk
    n_prow = n_p * seq
    n_srow = n_s * dseq
    n_tok = n_prow + n_srow
    past = page_table.shape[1] * PAGE_SIZE
    assert past % MOBA_BLOCK == 0 and seq % MOBA_BLOCK == 0

    x = jnp.concatenate([x_prompt.reshape(n_prow, d), x_sample.reshape(n_srow, d)], axis=0)

    w = w_in[l]
    c_z = 3 * wa + qkv_b
    c_ab = c_z + wk
    c_g = c_ab + 2 * H_B
    w_main = jnp.concatenate([w[:, :c_ab], w[:, c_g:]], axis=1).astype(BF16)
    w_ab = jnp.concatenate([w[:, c_ab:c_g], jnp.zeros((d, 128 - 2 * H_B), F32)], axis=1).astype(BF16)
    tm = 768
    proj = norm_matmul(x, norm_mix_g[l], w_main, tm, 1024)
    ab = norm_matmul(x, norm_mix_g[l], w_ab, tm, 128)

    k_all = proj[:, wa:2 * wa]
    v_all = proj[:, 2 * wa:3 * wa]

    nb = seq // MOBA_BLOCK
    bias_p = _toeplitz_bias(rel_bias, nb)
    out_a_p = moba_prompt(proj, bias_p, n_p, seq, 0, H_A, 2 * H_A)

    q_s = proj[n_prow:, :wa].reshape(n_s, dseq, H_A, HD_A) * (HD_A ** -0.5)
    qbd = jnp.einsum('blhd,hg->bhlgd', q_s, jnp.eye(H_A, dtype=F32)).reshape(n_s, H_A * dseq, wa)
    by_dist = _rel_bias_table(rel_bias, jnp.arange(past + dseq))
    bias_past = jnp.stack([by_dist[:, t + 1:t + 1 + past][:, ::-1] for t in range(dseq)], axis=1)
    bias_past = bias_past.reshape(H_A * dseq, past)
    bias_own = _rel_bias_table(rel_bias, jnp.arange(dseq)[:, None] - jnp.arange(dseq)[None, :])
    bias_own = bias_own.reshape(H_A * dseq, dseq)
    out_a_s = moba_sample(proj, n_prow // dseq, 1, 2, qbd, cache_k, cache_v, page_table,
                          bias_past, bias_own, dseq)
    out_a = jnp.concatenate([out_a_p, out_a_s], axis=0)

    qkv_colblk = 3 * wa // qkv_b
    assert qkv_colblk * qkv_b == 3 * wa
    tt = 512
    raw_p = proj[:n_prow, 3 * wa:3 * wa + qkv_b].reshape(n_p, seq // tt, tt, qkv_b)
    halo0 = jnp.zeros((n_p, 1, 8, qkv_b), F32)
    halo_p = jnp.concatenate([halo0, raw_p[:, :-1, tt - 8:, :]], axis=1).reshape(n_p * (seq // tt), 8, qkv_b)
    qp, kp, vp, gbp = gdn_prep(proj, ab, halo_p, conv_w[l], a_log[l], dt_bias[l],
                               0, n_p, seq, tt, qkv_colblk)
    raw_s = proj[n_prow:, 3 * wa:3 * wa + qkv_b].reshape(n_s, dseq, qkv_b)
    halo_s = jnp.concatenate([jnp.zeros((n_s, 8 - (CONV_WIDTH - 1), qkv_b), F32), state_conv[l]], axis=1)
    qs_, ks_, vs_, gbs = gdn_prep(proj, ab, halo_s, conv_w[l], a_log[l], dt_bias[l],
                                  n_prow, n_s, dseq, dseq, qkv_colblk)
    z_colblk = c_z // wk
    assert z_colblk * wk == c_z
    c_p = math.gcd(seq, GDN_CHUNK)
    out_b_p, s_p = gdn_chunks(qp, kp, vp, gbp, proj, z_colblk, 0, gdn_norm_g[l],
                              jnp.zeros((n_p, H_B, DK_B, DK_B), F32), n_p, c_p)
    c_s = math.gcd(dseq, GDN_CHUNK)
    out_b_s, s_s = gdn_chunks(qs_, ks_, vs_, gbs, proj, z_colblk, n_prow, gdn_norm_g[l],
                              state_gdn[l], 2, c_s)
    out_b = jnp.concatenate([out_b_p.reshape(n_prow, wk), out_b_s.reshape(n_srow, wk)], axis=0)
    conv_p = raw_p.reshape(n_p, seq, qkv_b)[:, seq - (CONV_WIDTH - 1):, :]
    conv_s = jnp.concatenate([state_conv[l], raw_s], axis=1)[:, dseq:, :]

    merged = branch_merge(out_a, out_b, w_branch_a[l].astype(BF16), w_branch_b[l].astype(BF16),
                          proj, c_ab, c_ab + d, tm, 1024)
    y1 = matmul_residual(merged, w_out[l].astype(BF16), x, tm, 1024)

    n_exp = router_w.shape[2]
    w_r = jnp.concatenate([router_w[l], jnp.zeros((d, 128 - n_exp), F32)], axis=1).astype(BF16)
    r_logits, h2 = norm_matmul(y1, norm_ffn_g[l], w_r, tm, 128, emit_h=True)
    logits = r_logits[:, :n_exp] + router_b[l].astype(F32)
    gate, row_tok, block_e, n_used, pos = _route(logits, n_tok)
    x_rows = jnp.concatenate([h2, jnp.zeros((1, d), BF16)], axis=0)[row_tok]
    y_rows = moe_experts(x_rows, block_e, n_used, w_gate_up.reshape(w_gate_up.shape[1:]),
                         b_gate_up.reshape(b_gate_up.shape[1:]), w_down.reshape(w_down.shape[1:]),
                         b_down.reshape(b_down.shape[1:]), 1024)
    picked = lax.optimization_barrier(y_rows[pos.reshape(-1)])
    y = combine_residual_norm(y1, picked.reshape(n_tok, TOP_K * d), gate, norm_final_g, 384)

    y_prompt = y[:n_prow].reshape(n_p, seq, d)
    y_sample = y[n_prow:].reshape(n_s, dseq, d)
    k_p = k_all[:n_prow].reshape(1, n_p, seq, H_A, HD_A)
    v_p = v_all[:n_prow].reshape(1, n_p, seq, H_A, HD_A)
    k_s = k_all[n_prow:].reshape(1, n_s, dseq, H_A, HD_A)
    v_s = v_all[n_prow:].reshape(1, n_s, dseq, H_A, HD_A)
    return (y_prompt, y_sample, k_p, v_p, s_p[None], conv_p[None],
            k_s, v_s, s_s[None], conv_s[None])
```

```python
import functools
import math

import jax
import jax.numpy as jnp
from jax import lax
from jax.experimental import pallas as pl
from jax.experimental.pallas import tpu as pltpu

F32 = jnp.float32
BF16 = jnp.bfloat16
HIGHEST = lax.Precision.HIGHEST

RMS_EPS = 1e-6
NEG_INF = -1e30

H_A = 8
HD_A = 128
MOBA_BLOCK = 256
MOBA_TOPK = 3
PAGE_SIZE = 128
N_BUCKETS = 32
MAX_DISTANCE = 4096

H_B = 8
DK_B = 128
CONV_WIDTH = 4
GDN_CHUNK = 64

TOP_K = 4
SWIGLU_LIMIT = 7.0
SWIGLU_ALPHA = 1.702
MOE_ROWS = 512

VMEM_LIMIT = 56 * 1024 * 1024


def _cparams(sem):
    return pltpu.CompilerParams(dimension_semantics=sem, vmem_limit_bytes=VMEM_LIMIT)


def _dot(a, b, precision=None):
    return jnp.dot(a, b, preferred_element_type=F32, precision=precision)


def _dot_nt(a, b, precision=None):
    return lax.dot_general(a, b, (((1,), (1,)), ((), ())),
                           preferred_element_type=F32, precision=precision)


def _dot_tn(a, b, precision=None):
    return lax.dot_general(a, b, (((0,), (0,)), ((), ())),
                           preferred_element_type=F32, precision=precision)


def _norm_matmul_kernel(x_ref, g_ref, w_ref, o_ref, *rest, emit_h):
    if emit_h:
        h_out_ref, h_ref = rest
    else:
        (h_ref,) = rest

    @pl.when(pl.program_id(1) == 0)
    def _():
        x = x_ref[...]
        y = x * lax.rsqrt(jnp.mean(x * x, axis=-1, keepdims=True) + RMS_EPS)
        h_ref[...] = (y * g_ref[...]).astype(BF16)

    if emit_h:
        h_out_ref[...] = h_ref[...]
    o_ref[...] = _dot(h_ref[...], w_ref[...])


def norm_matmul(x, g, w, tm, tn, emit_h=False):
    m, k = x.shape
    n = w.shape[1]
    out_shape = [jax.ShapeDtypeStruct((m, n), F32)]
    out_specs = [pl.BlockSpec((tm, tn), lambda i, j: (i, j))]
    if emit_h:
        out_shape.append(jax.ShapeDtypeStruct((m, k), BF16))
        out_specs.append(pl.BlockSpec((tm, k), lambda i, j: (i, 0)))
    res = pl.pallas_call(
        functools.partial(_norm_matmul_kernel, emit_h=emit_h),
        grid=(m // tm, n // tn),
        in_specs=[pl.BlockSpec((tm, k), lambda i, j: (i, 0)),
                  pl.BlockSpec((1, k), lambda i, j: (0, 0)),
                  pl.BlockSpec((k, tn), lambda i, j: (0, j))],
        out_specs=out_specs,
        out_shape=out_shape,
        scratch_shapes=[pltpu.VMEM((tm, k), BF16)],
        compiler_params=_cparams(("parallel", "arbitrary")),
        name="norm_matmul",
    )(x, g.reshape(1, k), w)
    return res if emit_h else res[0]


def _moba_prompt_kernel(q_ref, k_ref, v_ref, rv_ref, o_ref,
                        bias_ref, kb_ref, vb_ref, kmean_ref, m_ref, l_ref, acc_ref,
                        *, n_blocks):
    i = pl.program_id(2)
    blk = MOBA_BLOCK
    seq_len = n_blocks * blk

    @pl.when((pl.program_id(1) == 0) & (i == 0))
    def _():
        for m in range(n_blocks):
            st = blk * (n_blocks - 1 - m)
            win = jnp.broadcast_to(rv_ref[0, :, st:st + 2 * blk], (blk, 2 * blk))
            bias_ref[m] = pltpu.roll(win, 0, 1, stride=1, stride_axis=0)[:, blk:]

    @pl.when(i == 0)
    def _():
        k = k_ref[...]
        kb_ref[:, :HD_A] = k.astype(BF16)
        rblk = lax.broadcasted_iota(jnp.int32, (seq_len, HD_A), 0) // blk
        lane = lax.broadcasted_iota(jnp.int32, (seq_len, HD_A), 1)
        kb_ref[:, HD_A:] = jnp.where(lane == rblk, 1.0, 0.0).astype(BF16)
        vb_ref[...] = v_ref[...].astype(BF16)
        kmean_ref[...] = jnp.mean(k.reshape(n_blocks, blk, HD_A), axis=1)

    qs = q_ref[...] * (HD_A ** -0.5)
    qb = qs.astype(BF16)

    st = _dot_nt(kmean_ref[...], qs, precision=HIGHEST)
    rowi = lax.broadcasted_iota(jnp.int32, st.shape, 0)
    st = jnp.where(rowi < i, st, NEG_INF)
    sel = jnp.zeros(st.shape, F32)
    for c in range(n_blocks - 1):
        sc = st[c:c + 1, :]
        ahead = jnp.where((st > sc) | ((st == sc) & (rowi < c)), 1.0, 0.0)
        rank = jnp.sum(ahead, axis=0, keepdims=True)
        sel = jnp.where((rowi == c) & (rank < MOBA_TOPK), 1.0, sel)
    sel = jnp.where(rowi < i, sel, 0.0)
    er = lax.broadcasted_iota(jnp.int32, (n_blocks, HD_A), 0)
    ec = lax.broadcasted_iota(jnp.int32, (n_blocks, HD_A), 1)
    selq = _dot_tn(sel, jnp.where(er == ec, 1.0, 0.0))
    lane = lax.broadcasted_iota(jnp.int32, selq.shape, 1)
    negm = jnp.where((lane < n_blocks) & (selq < 0.5), NEG_INF, 0.0)
    q_aug = jnp.concatenate([qb, negm.astype(BF16)], axis=1)

    row0 = pl.multiple_of(i * blk, blk)
    r = lax.broadcasted_iota(jnp.int32, (blk, blk), 0)
    c_ = lax.broadcasted_iota(jnp.int32, (blk, blk), 1)
    logits = _dot_nt(qb, kb_ref[pl.ds(row0, blk), :HD_A]) + bias_ref[0]
    logits = jnp.where(c_ <= r, logits, NEG_INF)
    m0 = jnp.max(logits, axis=-1, keepdims=True)
    p = jnp.exp(logits - m0)
    m_ref[...] = m0
    l_ref[...] = jnp.sum(p, axis=-1, keepdims=True)
    acc_ref[...] = _dot(p.astype(BF16), vb_ref[pl.ds(row0, blk), :])

    def pair(t, carry):
        j0 = 2 * t
        rows = pl.ds(pl.multiple_of(j0 * blk, 2 * blk), 2 * blk)
        lg = _dot_nt(q_aug, kb_ref[rows, :])
        lg = lg + jnp.concatenate([bias_ref[i - j0], bias_ref[i - j0 - 1]], axis=1)
        m_old = m_ref[...]
        m_new = jnp.maximum(m_old, jnp.max(lg, axis=-1, keepdims=True))
        alpha = jnp.exp(m_old - m_new)
        pj = jnp.exp(lg - m_new)
        m_ref[...] = m_new
        l_ref[...] = alpha * l_ref[...] + jnp.sum(pj, axis=-1, keepdims=True)
        acc_ref[...] = alpha * acc_ref[...] + _dot(pj.astype(BF16), vb_ref[rows, :])
        return carry

    lax.fori_loop(0, (i + 1) // 2, pair, 0)
    o_ref[...] = (acc_ref[...] / l_ref[...]).astype(o_ref.dtype)


def moba_prompt(proj, rel_bias, n_seq, seq_len, q_col, k_col, v_col):
    blk = MOBA_BLOCK
    nb = seq_len // blk
    n_dist = (nb + 1) * blk
    rv = _rel_bias_table(rel_bias, nb * blk - jnp.arange(n_dist)).reshape(H_A, 1, n_dist)
    return pl.pallas_call(
        functools.partial(_moba_prompt_kernel, n_blocks=nb),
        grid=(H_A, n_seq, nb),
        in_specs=[pl.BlockSpec((blk, HD_A), lambda h, n, i: (n * nb + i, q_col + h)),
                  pl.BlockSpec((seq_len, HD_A), lambda h, n, i: (n, k_col + h)),
                  pl.BlockSpec((seq_len, HD_A), lambda h, n, i: (n, v_col + h)),
                  pl.BlockSpec((1, 1, n_dist), lambda h, n, i: (h, 0, 0))],
        out_specs=pl.BlockSpec((blk, HD_A), lambda h, n, i: (n * nb + i, h)),
        out_shape=jax.ShapeDtypeStruct((n_seq * seq_len, H_A * HD_A), BF16),
        scratch_shapes=[pltpu.VMEM((nb, blk, blk), F32),
                        pltpu.VMEM((seq_len, 2 * HD_A), BF16),
                        pltpu.VMEM((seq_len, HD_A), BF16),
                        pltpu.VMEM((nb, HD_A), F32),
                        pltpu.VMEM((blk, 1), F32),
                        pltpu.VMEM((blk, 1), F32),
                        pltpu.VMEM((blk, HD_A), F32)],
        compiler_params=_cparams(("arbitrary", "arbitrary", "arbitrary")),
        name="moba_prompt",
    )(proj, proj, proj, rv)


PAGES_PER_STEP = 16
BLOCKS_PER_STEP = PAGES_PER_STEP * PAGE_SIZE // MOBA_BLOCK


def _moba_sample_keys_kernel(pt_ref, qbd_ref, knew_ref, bias_ref, bown_ref, *rest,
                             n_steps, n_tok):
    pps = PAGES_PER_STEP
    bps = BLOCKS_PER_STEP
    ppb = pps // bps
    kp = rest[:pps]
    p_ref, pown_ref, lg_ref, ksum_ref, kb_ref = rest[pps:]
    s = pl.program_id(1)
    qbd = qbd_ref[0]
    qbd_b = qbd.astype(BF16)

    def lg_block(b):
        return b // bps, slice((b % bps) * MOBA_BLOCK, (b % bps + 1) * MOBA_BLOCK)

    sums = []
    for k in range(pps):
        hsum = []
        for h in range(H_A):
            kh = kp[k][0, 0, pl.ds(h, PAGE_SIZE, stride=H_A), :]
            kb_ref[k, :, h * HD_A:(h + 1) * HD_A] = kh.astype(BF16)
            hsum.append(jnp.sum(kh, axis=0, keepdims=True))
        lg = _dot_nt(qbd_b, kb_ref[k]) + bias_ref[:, k * PAGE_SIZE:(k + 1) * PAGE_SIZE]
        lg_ref[s, :, k * PAGE_SIZE:(k + 1) * PAGE_SIZE] = lg
        sums.append(jnp.concatenate(hsum, axis=1))
    blocks = [sum(sums[ppb * b:ppb * (b + 1)]) * (1.0 / MOBA_BLOCK) for b in range(bps)]
    ksum_ref[pl.ds(pl.multiple_of(s * bps, bps), bps), :] = jnp.concatenate(blocks, axis=0)

    @pl.when(s == n_steps - 1)
    def _():
        n_blk = n_steps * bps
        sc = _dot_nt(qbd, ksum_ref[...], precision=HIGHEST)
        bidx = lax.broadcasted_iota(jnp.int32, sc.shape, 1)
        sel = jnp.zeros(sc.shape, F32)
        for _ in range(MOBA_TOPK):
            mx = jnp.max(sc, axis=-1, keepdims=True)
            first = jnp.min(jnp.where(sc == mx, bidx, n_blk), axis=-1, keepdims=True)
            hit = bidx == first
            sel = jnp.where(hit, 1.0, sel)
            sc = jnp.where(hit, 2.0 * NEG_INF, sc)

        lo = _dot_nt(qbd, knew_ref[...]) + bown_ref[...]
        kc = lax.broadcasted_iota(jnp.int32, lo.shape, 1)
        qr = lax.broadcasted_iota(jnp.int32, lo.shape, 0) % n_tok
        lo = jnp.where(kc <= qr, lo, NEG_INF)
        m = jnp.max(lo, axis=-1, keepdims=True)

        for b in range(n_blk):
            st, sl = lg_block(b)
            x = jnp.where(sel[:, b:b + 1] > 0.0, lg_ref[st, :, sl], NEG_INF)
            lg_ref[st, :, sl] = x
            m = jnp.maximum(m, jnp.max(x, axis=-1, keepdims=True))
        e_own = jnp.exp(lo - m)
        l = jnp.sum(e_own, axis=-1, keepdims=True)
        for b in range(n_blk):
            st, sl = lg_block(b)
            e = jnp.exp(lg_ref[st, :, sl] - m)
            lg_ref[st, :, sl] = e
            l = l + jnp.sum(e, axis=-1, keepdims=True)
        inv = 1.0 / l
        pown_ref[0] = e_own * inv
        for b in range(n_blk):
            st, sl = lg_block(b)
            p_ref[0, :, b * MOBA_BLOCK:(b + 1) * MOBA_BLOCK] = (lg_ref[st, :, sl] * inv).astype(BF16)


def _moba_sample_values_kernel(pt_ref, p_ref, pown_ref, vnew_ref, *rest, n_steps, n_tok):
    pps = PAGES_PER_STEP
    vp = rest[:pps]
    o_ref, acc_ref, vb_ref = rest[pps:]
    s = pl.program_id(1)

    @pl.when(s == 0)
    def _():
        acc_ref[...] = _dot(pown_ref[0], vnew_ref[...])

    acc = acc_ref[...]
    for k in range(pps):
        for h in range(H_A):
            vh = vp[k][0, 0, pl.ds(h, PAGE_SIZE, stride=H_A), :]
            vb_ref[k, :, h * HD_A:(h + 1) * HD_A] = vh.astype(BF16)
        acc = acc + _dot(p_ref[0, :, k * PAGE_SIZE:(k + 1) * PAGE_SIZE], vb_ref[k])
    acc_ref[...] = acc

    @pl.when(s == n_steps - 1)
    def _():
        for h in range(H_A):
            o_ref[:, h * HD_A:(h + 1) * HD_A] = acc[h * n_tok:(h + 1) * n_tok,
                                                    h * HD_A:(h + 1) * HD_A].astype(o_ref.dtype)


def moba_sample(proj, row_blk0, k_colblk, v_colblk, qbd, cache_k, cache_v, layer, page_table,
                bias_past, bias_own, n_tok):
    n_seq, n_pages = page_table.shape
    pps = PAGES_PER_STEP
    n_steps = n_pages // pps
    n_past = n_pages * PAGE_SIZE
    wa = H_A * HD_A
    nrow = H_A * n_tok
    page_block = (1, 1, PAGE_SIZE * H_A, HD_A)
    cache_k = cache_k.reshape(cache_k.shape[:2] + (PAGE_SIZE * H_A, HD_A))
    cache_v = cache_v.reshape(cache_v.shape[:2] + (PAGE_SIZE * H_A, HD_A))

    def pmap(k):
        return lambda b, s, pt: (layer, pt[b, s * pps + k], 0, 0)

    p, p_own = pl.pallas_call(
        functools.partial(_moba_sample_keys_kernel, n_steps=n_steps, n_tok=n_tok),
        grid_spec=pltpu.PrefetchScalarGridSpec(
            num_scalar_prefetch=1,
            grid=(n_seq, n_steps),
            in_specs=[pl.BlockSpec((1, nrow, wa), lambda b, s, pt: (b, 0, 0)),
                      pl.BlockSpec((n_tok, wa), lambda b, s, pt: (row_blk0 + b, k_colblk)),
                      pl.BlockSpec((nrow, pps * PAGE_SIZE), lambda b, s, pt: (0, s)),
                      pl.BlockSpec((nrow, n_tok), lambda b, s, pt: (0, 0))]
            + [pl.BlockSpec(page_block, pmap(k)) for k in range(pps)],
            out_specs=[pl.BlockSpec((1, nrow, n_past), lambda b, s, pt: (b, 0, 0)),
                       pl.BlockSpec((1, nrow, n_tok), lambda b, s, pt: (b, 0, 0))],
            scratch_shapes=[pltpu.VMEM((n_steps, nrow, pps * PAGE_SIZE), F32),
                            pltpu.VMEM((n_past // MOBA_BLOCK, wa), F32),
                            pltpu.VMEM((pps, PAGE_SIZE, wa), BF16)]),
        out_shape=[jax.ShapeDtypeStruct((n_seq, nrow, n_past), BF16),
                   jax.ShapeDtypeStruct((n_seq, nrow, n_tok), F32)],
        compiler_params=_cparams(("arbitrary", "arbitrary")),
        name="moba_sample_keys",
    )(page_table, qbd, proj, bias_past, bias_own, *([cache_k] * pps))

    return pl.pallas_call(
        functools.partial(_moba_sample_values_kernel, n_steps=n_steps, n_tok=n_tok),
        grid_spec=pltpu.PrefetchScalarGridSpec(
            num_scalar_prefetch=1,
            grid=(n_seq, n_steps),
            in_specs=[pl.BlockSpec((1, nrow, pps * PAGE_SIZE), lambda b, s, pt: (b, 0, s)),
                      pl.BlockSpec((1, nrow, n_tok), lambda b, s, pt: (b, 0, 0)),
                      pl.BlockSpec((n_tok, wa), lambda b, s, pt: (row_blk0 + b, v_colblk))]
            + [pl.BlockSpec(page_block, pmap(k)) for k in range(pps)],
            out_specs=pl.BlockSpec((n_tok, wa), lambda b, s, pt: (b, 0)),
            scratch_shapes=[pltpu.VMEM((nrow, wa), F32),
                            pltpu.VMEM((pps, PAGE_SIZE, wa), BF16)]),
        out_shape=jax.ShapeDtypeStruct((n_seq * n_tok, wa), BF16),
        compiler_params=_cparams(("arbitrary", "arbitrary")),
        name="moba_sample_values",
    )(page_table, p, p_own, proj, *([cache_v] * pps))


def _gdn_prep_kernel(x_ref, halo_ref, ab_ref, cw_ref, alog_ref, dtb_ref,
                     q_ref, k_ref, v_ref, gb_ref, *, tt):
    x = x_ref[...]
    xf = jnp.concatenate([halo_ref[0], x], axis=0)
    cw = cw_ref[...]
    conv = xf[5:5 + tt] * cw[0:1]
    for w in range(1, CONV_WIDTH):
        conv = conv + xf[5 + w:5 + w + tt] * cw[w:w + 1]
    act = conv * jax.nn.sigmoid(conv)
    wk = H_B * DK_B
    for h in range(H_B):
        q = act[:, h * DK_B:(h + 1) * DK_B]
        k = act[:, wk + h * DK_B:wk + (h + 1) * DK_B]
        q_ref[0, h] = q * lax.rsqrt(jnp.sum(q * q, axis=-1, keepdims=True) + RMS_EPS) * (DK_B ** -0.5)
        k_ref[0, h] = k * lax.rsqrt(jnp.sum(k * k, axis=-1, keepdims=True) + RMS_EPS)
        v_ref[0, h] = act[:, 2 * wk + h * DK_B:2 * wk + (h + 1) * DK_B]
    ab = ab_ref[...]
    t = ab + dtb_ref[...]
    sp = jnp.maximum(t, 0.0) + jnp.log(1.0 + jnp.exp(-jnp.abs(t)))
    g = -jnp.exp(alog_ref[...]) * sp
    lane = lax.broadcasted_iota(jnp.int32, ab.shape, 1)
    gb_ref[0] = jnp.where(lane < H_B, g, jax.nn.sigmoid(ab))


def gdn_prep(proj, ab, halo, conv_w, a_log, dt_bias, row0, n_seq, seq_len, tt, qkv_colblk):
    w3 = conv_w.shape[1]
    nt = seq_len // tt
    rb0 = row0 // tt
    pad = jnp.zeros((1, 128 - H_B), F32)
    alog = jnp.concatenate([a_log.reshape(1, H_B), pad], axis=1)
    dtb = jnp.concatenate([dt_bias.reshape(1, H_B), pad], axis=1)
    hm = jax.ShapeDtypeStruct((n_seq, H_B, seq_len, DK_B), F32)
    hspec = pl.BlockSpec((1, H_B, tt, DK_B), lambda n, t: (n, 0, t, 0))
    return pl.pallas_call(
        functools.partial(_gdn_prep_kernel, tt=tt),
        grid=(n_seq, nt),
        in_specs=[pl.BlockSpec((tt, w3), lambda n, t: (rb0 + n * nt + t, qkv_colblk)),
                  pl.BlockSpec((1, 8, w3), lambda n, t: (n * nt + t, 0, 0)),
                  pl.BlockSpec((tt, 128), lambda n, t: (rb0 + n * nt + t, 0)),
                  pl.BlockSpec((CONV_WIDTH, w3), lambda n, t: (0, 0)),
                  pl.BlockSpec((1, 128), lambda n, t: (0, 0)),
                  pl.BlockSpec((1, 128), lambda n, t: (0, 0))],
        out_specs=[hspec, hspec, hspec,
                   pl.BlockSpec((1, tt, 128), lambda n, t: (n, t, 0))],
        out_shape=[hm, hm, hm, jax.ShapeDtypeStruct((n_seq, seq_len, 128), F32)],
        compiler_params=_cparams(("parallel", "parallel")),
        name="gdn_prep",
    )(proj, halo, ab, conv_w, alog, dtb)


def _split2(a):
    hi = a.astype(BF16)
    return hi, (a - hi.astype(F32)).astype(BF16)


def _split3(a):
    hi = a.astype(BF16)
    r1 = a - hi.astype(F32)
    mid = r1.astype(BF16)
    return hi, mid, (r1 - mid.astype(F32)).astype(BF16)


def _dot3(a, b):
    ah, al = _split2(a)
    bh, bl = _split2(b)
    return _dot(ah, bh) + (_dot(ah, bl) + _dot(al, bh))


def _dot_exact_lhs(a_bf16, b):
    b1, b2, b3 = _split3(b)
    return _dot(a_bf16, b1) + (_dot(a_bf16, b2) + _dot(a_bf16, b3))


def _unit_lower_inverse(a_list, c):
    r = lax.broadcasted_iota(jnp.int32, (c, c), 0)
    cc = lax.broadcasted_iota(jnp.int32, (c, c), 1)
    eye = jnp.where(r == cc, 1.0, 0.0)
    base = min(c, 8)
    n1 = [jnp.where((r // base) == (cc // base), -a, 0.0) for a in a_list]
    n2 = [_dot3(x, x) for x in n1]
    n4 = [_dot3(x, x) for x in n2]
    t = [_dot3(eye + x, eye + y) for x, y in zip(n1, n2)]
    t = [_dot3(x, eye + y) for x, y in zip(t, n4)]
    b = base
    while b < c:
        inner = ((r // (2 * b)) == (cc // (2 * b))) & ((r // b) != (cc // b))
        left = [_dot3(x, jnp.where(inner, a, 0.0)) for x, a in zip(t, a_list)]
        t = [x - _dot3(y, x) for x, y in zip(t, left)]
        b *= 2
    return t


def _gdn_chunk_kernel(*refs, nb, c):
    q_ref, k_ref, v_ref, gb_ref = refs[:4]
    z_refs = refs[4:4 + nb]
    ng_ref, s0_ref, o_ref, sout_ref, s_ref = refs[4 + nb:]
    ci = pl.program_id(1)

    @pl.when(ci == 0)
    def _():
        s_ref[...] = s0_ref[...]

    r = lax.broadcasted_iota(jnp.int32, (c, c), 0)
    cc = lax.broadcasted_iota(jnp.int32, (c, c), 1)
    incl = r >= cc
    strict = r > cc
    ltri = jnp.where(incl, 1.0, 0.0).astype(BF16)
    eye = jnp.where(r == cc, 1.0, 0.0)
    ones = jnp.ones((c, c), BF16)
    ng = ng_ref[...]
    ch = [(n, h) for n in range(nb) for h in range(H_B)]
    gbv = [gb_ref[n] for n in range(nb)]
    q = [q_ref[n, h] for n, h in ch]
    k = [k_ref[n, h] for n, h in ch]
    v = [v_ref[n, h] for n, h in ch]
    gcol = [jnp.broadcast_to(gbv[n][:, h:h + 1], (c, DK_B)) for n, h in ch]
    bcol = [jnp.broadcast_to(gbv[n][:, H_B + h:H_B + h + 1], (c, DK_B)) for n, h in ch]
    big_g = [_dot_exact_lhs(ltri, g) for g in gcol]
    gj = [_dot_exact_lhs(ones, g[:, :c] * eye) for g in big_g]
    decay = [jnp.where(incl, jnp.exp(jnp.minimum(g[:, :c] - x, 0.0)), 0.0) for g, x in zip(big_g, gj)]
    kk = [_dot_nt(x, x) for x in k]
    a = [jnp.where(strict, b[:, :c] * x * d, 0.0) for b, x, d in zip(bcol, kk, decay)]
    t = _unit_lower_inverse(a, c)
    gam = [jnp.exp(g) for g in big_g]
    u = [_dot3(x, b * y) for x, b, y in zip(t, bcol, v)]
    w = [_dot3(x, b * g * y) for x, b, g, y in zip(t, bcol, gam, k)]
    qk = [_dot_nt(x, y) * d for x, y, d in zip(q, k, decay)]
    g_last = [g[c - 1:c, :] for g in big_g]
    k_tail = [jnp.exp(gl - g) * y for gl, g, y in zip(g_last, big_g, k)]
    s = [s_ref[n, h] for n, h in ch]
    uu = [x - _dot(y, z_) for x, y, z_ in zip(u, w, s)]
    o = [_dot(g * x, z_) + _dot(y, x2) for g, x, z_, y, x2 in zip(gam, q, s, qk, uu)]
    s_new = [jnp.exp(gl) * z_ + _dot_tn(y, x) for gl, z_, y, x in zip(g_last, s, k_tail, uu)]
    for (n, h), x in zip(ch, s_new):
        s_ref[n, h] = x
    for (n, h), x in zip(ch, o):
        x = x * lax.rsqrt(jnp.mean(x * x, axis=-1, keepdims=True) + RMS_EPS) * ng
        zh = z_refs[n][:, h * DK_B:(h + 1) * DK_B]
        o_ref[n, :, h * DK_B:(h + 1) * DK_B] = (x * (zh * jax.nn.sigmoid(zh))).astype(o_ref.dtype)

    @pl.when(ci == pl.num_programs(1) - 1)
    def _():
        sout_ref[...] = s_ref[...]


def gdn_chunks(q, k, v, gb, proj, z_colblk, row0, norm_g, s0, nb, c):
    n_seq, _, seq_len, _ = q.shape
    nc = seq_len // c
    wv = H_B * DK_B
    rb0 = row0 // c
    hspec = pl.BlockSpec((nb, H_B, c, DK_B), lambda n, ci: (n, 0, ci, 0))
    sspec = pl.BlockSpec((nb, H_B, DK_B, DK_B), lambda n, ci: (n, 0, 0, 0))

    def zmap(j):
        return lambda n, ci: (rb0 + (n * nb + j) * nc + ci, z_colblk)

    return pl.pallas_call(
        functools.partial(_gdn_chunk_kernel, nb=nb, c=c),
        grid=(n_seq // nb, nc),
        in_specs=[hspec, hspec, hspec,
                  pl.BlockSpec((nb, c, 128), lambda n, ci: (n, ci, 0))]
        + [pl.BlockSpec((c, wv), zmap(j)) for j in range(nb)]
        + [pl.BlockSpec((1, DK_B), lambda n, ci: (0, 0)), sspec],
        out_specs=[pl.BlockSpec((nb, c, wv), lambda n, ci: (n, ci, 0)), sspec],
        out_shape=[jax.ShapeDtypeStruct((n_seq, seq_len, wv), BF16),
                   jax.ShapeDtypeStruct((n_seq, H_B, DK_B, DK_B), F32)],
        scratch_shapes=[pltpu.VMEM((nb, H_B, DK_B, DK_B), F32)],
        compiler_params=_cparams(("parallel", "arbitrary")),
        name="gdn_chunks",
    )(q, k, v, gb, *([proj] * nb), norm_g.reshape(1, DK_B), s0)


def _merge_kernel(oa_ref, ob_ref, wa_ref, wb_ref, ga_ref, gb_ref, o_ref):
    ya = _dot(oa_ref[...], wa_ref[...])
    yb = _dot(ob_ref[...], wb_ref[...])
    o_ref[...] = (jax.nn.sigmoid(ga_ref[...]) * ya
                  + jax.nn.sigmoid(gb_ref[...]) * yb).astype(o_ref.dtype)


def branch_merge(out_a, out_b, wa, wb, proj, ga_col0, gb_col0, tm, tn):
    m, ka = out_a.shape
    n = wa.shape[1]
    ga_blk, gb_blk = ga_col0 // tn, gb_col0 // tn
    assert ga_blk * tn == ga_col0 and gb_blk * tn == gb_col0
    return pl.pallas_call(
        _merge_kernel,
        grid=(m // tm, n // tn),
        in_specs=[pl.BlockSpec((tm, ka), lambda i, j: (i, 0)),
                  pl.BlockSpec((tm, ka), lambda i, j: (i, 0)),
                  pl.BlockSpec((ka, tn), lambda i, j: (0, j)),
                  pl.BlockSpec((ka, tn), lambda i, j: (0, j)),
                  pl.BlockSpec((tm, tn), lambda i, j: (i, ga_blk + j)),
                  pl.BlockSpec((tm, tn), lambda i, j: (i, gb_blk + j))],
        out_specs=pl.BlockSpec((tm, tn), lambda i, j: (i, j)),
        out_shape=jax.ShapeDtypeStruct((m, n), BF16),
        compiler_params=_cparams(("parallel", "parallel")),
        name="branch_merge",
    )(out_a, out_b, wa, wb, proj, proj)


def _matmul_residual_kernel(x_ref, w_ref, r_ref, o_ref):
    o_ref[...] = r_ref[...] + _dot(x_ref[...], w_ref[...])


def matmul_residual(x, w, res, tm, tn):
    m, k = x.shape
    n = w.shape[1]
    return pl.pallas_call(
        _matmul_residual_kernel,
        grid=(m // tm, n // tn),
        in_specs=[pl.BlockSpec((tm, k), lambda i, j: (i, 0)),
                  pl.BlockSpec((k, tn), lambda i, j: (0, j)),
                  pl.BlockSpec((tm, tn), lambda i, j: (i, j))],
        out_specs=pl.BlockSpec((tm, tn), lambda i, j: (i, j)),
        out_shape=jax.ShapeDtypeStruct((m, n), F32),
        compiler_params=_cparams(("parallel", "parallel")),
        name="out_proj",
    )(x, w, res)


MOE_TA = 256


def _lane_rotate(x, shift):
    parts = [pltpu.roll(x[:, c:c + 128], shift, 1) for c in range(0, x.shape[1], 128)]
    return parts[0] if len(parts) == 1 else jnp.concatenate(parts, axis=1)


def _new_expert(be_ref, b):
    return (b == 0) | (be_ref[b] != be_ref[jnp.maximum(b - 1, 0)])


def _moe_up_kernel(be_ref, nu_ref, x_ref, wa_ref, wb_ref, ba_ref, bb_ref, o_ref, wab_ref, wbb_ref):
    b = pl.program_id(1)

    @pl.when(b < nu_ref[0])
    def _():
        @pl.when(_new_expert(be_ref, b))
        def _():
            wab_ref[...] = wa_ref[0].astype(BF16)
            wbb_ref[...] = wb_ref[0].astype(BF16)

        x = x_ref[...]
        ga = _dot(x, wab_ref[...]) + ba_ref[0]
        gb = _dot(x, wbb_ref[...]) + bb_ref[0]
        even = (lax.broadcasted_iota(jnp.int32, ga.shape, 1) % 2) == 0
        gate = jnp.where(even, ga, _lane_rotate(gb, 1))
        up = jnp.where(even, _lane_rotate(ga, 127), gb)
        gl = jnp.minimum(gate, SWIGLU_LIMIT)
        up = jnp.clip(up, -SWIGLU_LIMIT, SWIGLU_LIMIT)
        o_ref[...] = (gl * jax.nn.sigmoid(SWIGLU_ALPHA * gl) * (up + 1.0)).astype(o_ref.dtype)


def _bf16_bits(w):
    return lax.bitcast_convert_type(w.astype(BF16).astype(F32), jnp.uint32)


def _moe_down_kernel(be_ref, nu_ref, a_ref, wd_ref, bd_ref, o_ref, wdb_ref):
    b = pl.program_id(1)

    @pl.when(b < nu_ref[0])
    def _():
        @pl.when(_new_expert(be_ref, b))
        def _():
            f = wd_ref.shape[1]
            for n in range(f // (2 * MOE_TA)):
                wa = wd_ref[0, n * MOE_TA:(n + 1) * MOE_TA, :]
                wb = wd_ref[0, f // 2 + n * MOE_TA:f // 2 + (n + 1) * MOE_TA, :]
                pair = (_bf16_bits(wa) >> 16) | _bf16_bits(wb)
                wdb_ref[n * 2 * MOE_TA:(n + 1) * 2 * MOE_TA, :] = pltpu.bitcast(pair, BF16)

        o_ref[...] = _dot(a_ref[...], wdb_ref[...]) + bd_ref[0]


def moe_experts(x_rows, block_e, n_used, w_gate_up, b_gate_up, w_down, b_down, tn_down):
    rows, d = x_rows.shape
    n_exp, _, f2 = w_gate_up.shape
    f = f2 // 2
    dm = w_down.shape[2]
    nblk = rows // MOE_ROWS
    nt = f // (2 * MOE_TA)
    tw = 2 * MOE_TA

    def rowmap(n, b, be, nu):
        return (jnp.minimum(b, nu[0] - 1), 0)

    def outmap(n, b, be, nu):
        return (jnp.minimum(b, nu[0] - 1), n)

    def wmap(off):
        return lambda n, b, be, nu: (be[jnp.minimum(b, nu[0] - 1)], 0, off + n)

    bgu = b_gate_up.reshape(n_exp, 1, f2)
    act = pl.pallas_call(
        _moe_up_kernel,
        grid_spec=pltpu.PrefetchScalarGridSpec(
            num_scalar_prefetch=2,
            grid=(nt, nblk),
            in_specs=[pl.BlockSpec((MOE_ROWS, d), rowmap),
                      pl.BlockSpec((1, d, tw), wmap(0)),
                      pl.BlockSpec((1, d, tw), wmap(nt)),
                      pl.BlockSpec((1, 1, tw), wmap(0)),
                      pl.BlockSpec((1, 1, tw), wmap(nt))],
            out_specs=pl.BlockSpec((MOE_ROWS, tw), outmap),
            scratch_shapes=[pltpu.VMEM((d, tw), BF16), pltpu.VMEM((d, tw), BF16)]),
        out_shape=jax.ShapeDtypeStruct((rows, f), BF16),
        compiler_params=_cparams(("arbitrary", "arbitrary")),
        name="moe_up",
    )(block_e, n_used, x_rows, w_gate_up, w_gate_up, bgu, bgu)
    return pl.pallas_call(
        _moe_down_kernel,
        grid_spec=pltpu.PrefetchScalarGridSpec(
            num_scalar_prefetch=2,
            grid=(dm // tn_down, nblk),
            in_specs=[pl.BlockSpec((MOE_ROWS, f), rowmap),
                      pl.BlockSpec((1, f, tn_down), wmap(0)),
                      pl.BlockSpec((1, 1, tn_down), wmap(0))],
            out_specs=pl.BlockSpec((MOE_ROWS, tn_down), outmap),
            scratch_shapes=[pltpu.VMEM((f, tn_down), BF16)]),
        out_shape=jax.ShapeDtypeStruct((rows, dm), F32),
        compiler_params=_cparams(("arbitrary", "arbitrary")),
        name="moe_down",
    )(block_e, n_used, act, w_down, b_down.reshape(n_exp, 1, dm))


def _final_kernel(y_ref, e_ref, gate_ref, g_ref, o_ref):
    d = y_ref.shape[1]
    x = y_ref[...]
    for kk in range(TOP_K):
        x = x + e_ref[:, kk * d:(kk + 1) * d] * gate_ref[:, kk:kk + 1]
    o_ref[...] = x * lax.rsqrt(jnp.mean(x * x, axis=-1, keepdims=True) + RMS_EPS) * g_ref[...]


def combine_residual_norm(y, expert_out, gate, g, tm):
    m, d = y.shape
    return pl.pallas_call(
        _final_kernel,
        grid=(m // tm,),
        in_specs=[pl.BlockSpec((tm, d), lambda i: (i, 0)),
                  pl.BlockSpec((tm, TOP_K * d), lambda i: (i, 0)),
                  pl.BlockSpec((tm, TOP_K), lambda i: (i, 0)),
                  pl.BlockSpec((1, d), lambda i: (0, 0))],
        out_specs=pl.BlockSpec((tm, d), lambda i: (i, 0)),
        out_shape=jax.ShapeDtypeStruct((m, d), F32),
        compiler_params=_cparams(("parallel",)),
        name="combine_norm",
    )(y, expert_out, gate, g.reshape(1, d))


def _rel_bias_table(rel_bias, dist):
    n = jnp.maximum(dist, 0)
    max_exact = N_BUCKETS // 2
    nf = jnp.maximum(n, max_exact).astype(F32)
    large = max_exact + (jnp.log(nf / max_exact) / math.log(MAX_DISTANCE / max_exact)
                         * (N_BUCKETS - max_exact)).astype(jnp.int32)
    bucket = jnp.where(n < max_exact, n, jnp.minimum(large, N_BUCKETS - 1))
    return rel_bias[:, bucket].astype(F32)


def _route(logits, n_tok):
    top_v, top_e = lax.top_k(logits, TOP_K)
    gate = jax.nn.softmax(top_v, axis=-1)
    n_exp = logits.shape[1]
    n_assign = n_tok * TOP_K
    flat_e = top_e.reshape(-1)
    order = jnp.argsort(flat_e, stable=True)
    e_sorted = flat_e[order]
    tok_sorted = (order // TOP_K).astype(jnp.int32)
    counts = jnp.zeros((n_exp,), jnp.int32).at[flat_e].add(1)
    padded = (counts + MOE_ROWS - 1) // MOE_ROWS * MOE_ROWS
    pad_end = jnp.cumsum(padded)
    pad_start = pad_end - padded
    start = jnp.cumsum(counts) - counts
    dest = (pad_start[e_sorted] + jnp.arange(n_assign) - start[e_sorted]).astype(jnp.int32)
    n_blocks = -(-n_assign // MOE_ROWS) + n_exp
    row_tok = jnp.zeros((n_blocks * MOE_ROWS,), jnp.int32).at[dest].set(tok_sorted)
    block_e = jnp.minimum(jnp.searchsorted(pad_end, jnp.arange(n_blocks) * MOE_ROWS, side='right'),
                          n_exp - 1).astype(jnp.int32)
    n_used = (pad_end[-1] // MOE_ROWS).astype(jnp.int32).reshape(1)
    pos = jnp.zeros((n_assign,), jnp.int32).at[order].set(dest).reshape(n_tok, TOP_K)
    return gate, row_tok, block_e, n_used, pos


def kernel(x_prompt, x_sample, cache_k, cache_v, state_gdn, state_conv, page_table, norm_mix_g, w_in, rel_bias, conv_w, a_log, dt_bias, gdn_norm_g, w_branch_a, w_branch_b, w_out, norm_ffn_g, router_w, router_b, w_gate_up, b_gate_up, w_down, b_down, norm_final_g):
    n_p, seq, d = x_prompt.shape
    n_s, dseq, _ = x_sample.shape
    assert w_in.shape[0] == 1
    l = 0
    wa = H_A * HD_A
    wk = H_B * DK_B
    qkv_b = 3 * wk
    n_prow = n_p * seq
    n_srow = n_s * dseq
    n_tok = n_prow + n_srow
    past = page_table.shape[1] * PAGE_SIZE
    assert past % MOBA_BLOCK == 0 and seq % MOBA_BLOCK == 0

    x = jnp.concatenate([x_prompt.reshape(n_prow, d), x_sample.reshape(n_srow, d)], axis=0)

    w = w_in[l]
    c_z = 3 * wa + qkv_b
    c_ab = c_z + wk
    c_g = c_ab + 2 * H_B
    w_main = jnp.concatenate([w[:, :c_ab], w[:, c_g:]], axis=1).astype(BF16)
    w_ab = jnp.concatenate([w[:, c_ab:c_g], jnp.zeros((d, 128 - 2 * H_B), F32)], axis=1).astype(BF16)
    tm = 768
    proj = norm_matmul(x, norm_mix_g[l], w_main, tm, 1024)
    ab = norm_matmul(x, norm_mix_g[l], w_ab, tm, 128)

    k_all = proj[:, wa:2 * wa]
    v_all = proj[:, 2 * wa:3 * wa]

    out_a_p = moba_prompt(proj, rel_bias, n_p, seq, 0, H_A, 2 * H_A)

    q_s = proj[n_prow:, :wa].reshape(n_s, dseq, H_A, HD_A) * (HD_A ** -0.5)
    qbd = jnp.einsum('blhd,hg->bhlgd', q_s, jnp.eye(H_A, dtype=F32)).reshape(n_s, H_A * dseq, wa)
    n_dist = past + dseq
    far_first = _rel_bias_table(rel_bias, n_dist - 1 - jnp.arange(n_dist))
    bias_past = jnp.stack([far_first[:, dseq - 1 - t:dseq - 1 - t + past] for t in range(dseq)], axis=1)
    bias_past = bias_past.reshape(H_A * dseq, past)
    bias_own = _rel_bias_table(rel_bias, jnp.arange(dseq)[:, None] - jnp.arange(dseq)[None, :])
    bias_own = bias_own.reshape(H_A * dseq, dseq)
    out_a_s = moba_sample(proj, n_prow // dseq, 1, 2, qbd, cache_k, cache_v, l, page_table,
                          bias_past, bias_own, dseq)
    out_a = jnp.concatenate([out_a_p, out_a_s], axis=0)

    qkv_colblk = 3 * wa // qkv_b
    assert qkv_colblk * qkv_b == 3 * wa
    tt = 512
    raw_p = proj[:n_prow, 3 * wa:3 * wa + qkv_b].reshape(n_p, seq // tt, tt, qkv_b)
    halo0 = jnp.zeros((n_p, 1, 8, qkv_b), F32)
    halo_p = jnp.concatenate([halo0, raw_p[:, :-1, tt - 8:, :]], axis=1).reshape(n_p * (seq // tt), 8, qkv_b)
    qp, kp, vp, gbp = gdn_prep(proj, ab, halo_p, conv_w[l], a_log[l], dt_bias[l],
                               0, n_p, seq, tt, qkv_colblk)
    raw_s = proj[n_prow:, 3 * wa:3 * wa + qkv_b].reshape(n_s, dseq, qkv_b)
    halo_s = jnp.concatenate([jnp.zeros((n_s, 8 - (CONV_WIDTH - 1), qkv_b), F32), state_conv[l]], axis=1)
    qs_, ks_, vs_, gbs = gdn_prep(proj, ab, halo_s, conv_w[l], a_log[l], dt_bias[l],
                                  n_prow, n_s, dseq, dseq, qkv_colblk)
    z_colblk = c_z // wk
    assert z_colblk * wk == c_z
    c_p = math.gcd(seq, GDN_CHUNK)
    out_b_p, s_p = gdn_chunks(qp, kp, vp, gbp, proj, z_colblk, 0, gdn_norm_g[l],
                              jnp.zeros((n_p, H_B, DK_B, DK_B), F32), n_p, c_p)
    c_s = math.gcd(dseq, GDN_CHUNK)
    out_b_s, s_s = gdn_chunks(qs_, ks_, vs_, gbs, proj, z_colblk, n_prow, gdn_norm_g[l],
                              state_gdn[l], 2, c_s)
    out_b = jnp.concatenate([out_b_p.reshape(n_prow, wk), out_b_s.reshape(n_srow, wk)], axis=0)
    conv_p = raw_p.reshape(n_p, seq, qkv_b)[:, seq - (CONV_WIDTH - 1):, :]
    conv_s = jnp.concatenate([state_conv[l], raw_s], axis=1)[:, dseq:, :]

    merged = branch_merge(out_a, out_b, w_branch_a[l].astype(BF16), w_branch_b[l].astype(BF16),
                          proj, c_ab, c_ab + d, tm, 1024)
    y1 = matmul_residual(merged, w_out[l].astype(BF16), x, tm, 1024)

    n_exp = router_w.shape[2]
    w_r = jnp.concatenate([router_w[l], jnp.zeros((d, 128 - n_exp), F32)], axis=1).astype(BF16)
    r_logits, h2 = norm_matmul(y1, norm_ffn_g[l], w_r, tm, 128, emit_h=True)
    logits = r_logits[:, :n_exp] + router_b[l].astype(F32)
    gate, row_tok, block_e, n_used, pos = _route(logits, n_tok)
    x_rows = h2[row_tok]
    y_rows = moe_experts(x_rows, block_e, n_used, w_gate_up.reshape(w_gate_up.shape[1:]),
                         b_gate_up.reshape(b_gate_up.shape[1:]), w_down.reshape(w_down.shape[1:]),
                         b_down.reshape(b_down.shape[1:]), 1024)
    picked = lax.optimization_barrier(y_rows[pos.reshape(-1)])
    y = combine_residual_norm(y1, picked.reshape(n_tok, TOP_K * d), gate, norm_final_g, 384)

    y_prompt = y[:n_prow].reshape(n_p, seq, d)
    y_sample = y[n_prow:].reshape(n_s, dseq, d)
    k_p = k_all[:n_prow].reshape(1, n_p, seq, H_A, HD_A)
    v_p = v_all[:n_prow].reshape(1, n_p, seq, H_A, HD_A)
    k_s = k_all[n_prow:].reshape(1, n_s, dseq, H_A, HD_A)
    v_s = v_all[n_prow:].reshape(1, n_s, dseq, H_A, HD_A)
    return (y_prompt, y_sample, k_p, v_p, s_p[None], conv_p[None],
            k_s, v_s, s_s[None], conv_s[None])
```

```python
import functools
import math

import jax
import jax.numpy as jnp
from jax import lax
from jax.experimental import pallas as pl
from jax.experimental.pallas import tpu as pltpu

F32 = jnp.float32
BF16 = jnp.bfloat16
HIGHEST = lax.Precision.HIGHEST

RMS_EPS = 1e-6
NEG_INF = -1e30

H_A = 8
HD_A = 128
MOBA_BLOCK = 256
MOBA_TOPK = 3
MOBA_GROUP = 4
PAGE_SIZE = 128
N_BUCKETS = 32
MAX_DISTANCE = 4096

H_B = 8
DK_B = 128
CONV_WIDTH = 4
GDN_CHUNK = 64

TOP_K = 4
SWIGLU_LIMIT = 7.0
SWIGLU_ALPHA = 1.702
MOE_ROWS = 512

VMEM_LIMIT = 56 * 1024 * 1024


def _cparams(sem):
    return pltpu.CompilerParams(dimension_semantics=sem, vmem_limit_bytes=VMEM_LIMIT)


def _dot(a, b, precision=None):
    return jnp.dot(a, b, preferred_element_type=F32, precision=precision)


def _dot_nt(a, b, precision=None):
    return lax.dot_general(a, b, (((1,), (1,)), ((), ())),
                           preferred_element_type=F32, precision=precision)


def _dot_tn(a, b, precision=None):
    return lax.dot_general(a, b, (((0,), (0,)), ((), ())),
                           preferred_element_type=F32, precision=precision)


def _norm_matmul_kernel(x_ref, g_ref, w_ref, o_ref, *rest, emit_h):
    if emit_h:
        h_out_ref, h_ref = rest
    else:
        (h_ref,) = rest

    @pl.when(pl.program_id(1) == 0)
    def _():
        x = x_ref[...]
        y = x * lax.rsqrt(jnp.mean(x * x, axis=-1, keepdims=True) + RMS_EPS)
        h_ref[...] = (y * g_ref[...]).astype(BF16)

    if emit_h:
        half = h_ref.shape[1] // 2
        h_out_ref[...] = _pack_bf16_pair(h_ref[:, :half], h_ref[:, half:])
    o_ref[...] = _dot(h_ref[...], w_ref[...])


def _pack_bf16_pair(hi, lo):
    hi_bits = lax.bitcast_convert_type(hi.astype(F32), jnp.uint32)
    lo_bits = lax.bitcast_convert_type(lo.astype(F32), jnp.uint32)
    return lax.bitcast_convert_type(hi_bits | (lo_bits >> 16), F32)


def _unpack_bf16_pair(words):
    bits = lax.bitcast_convert_type(words, jnp.uint32)
    hi = lax.bitcast_convert_type(bits & jnp.uint32(0xFFFF0000), F32)
    lo = lax.bitcast_convert_type(bits << 16, F32)
    return hi.astype(BF16), lo.astype(BF16)


def norm_matmul(x, g, w, tm, tn, emit_h=False):
    m, k = x.shape
    n = w.shape[1]
    out_shape = [jax.ShapeDtypeStruct((m, n), F32)]
    out_specs = [pl.BlockSpec((tm, tn), lambda i, j: (i, j))]
    if emit_h:
        out_shape.append(jax.ShapeDtypeStruct((m, k // 2), F32))
        out_specs.append(pl.BlockSpec((tm, k // 2), lambda i, j: (i, 0)))
    res = pl.pallas_call(
        functools.partial(_norm_matmul_kernel, emit_h=emit_h),
        grid=(m // tm, n // tn),
        in_specs=[pl.BlockSpec((tm, k), lambda i, j: (i, 0)),
                  pl.BlockSpec((1, k), lambda i, j: (0, 0)),
                  pl.BlockSpec((k, tn), lambda i, j: (0, j))],
        out_specs=out_specs,
        out_shape=out_shape,
        scratch_shapes=[pltpu.VMEM((tm, k), BF16)],
        compiler_params=_cparams(("parallel", "arbitrary")),
        name="norm_matmul",
    )(x, g.reshape(1, k), w)
    return res if emit_h else res[0]


def _moba_prompt_kernel(q_ref, k_ref, v_ref, rv_ref, o_ref,
                        bias_ref, kb_ref, vb_ref, kmean_ref, m_ref, l_ref, acc_ref,
                        *, n_blocks):
    i = pl.program_id(2)
    blk = MOBA_BLOCK
    seq_len = n_blocks * blk

    @pl.when((pl.program_id(1) == 0) & (i == 0))
    def _():
        for m in range(n_blocks):
            st = blk * (n_blocks - 1 - m)
            win = jnp.broadcast_to(rv_ref[0, :, st:st + 2 * blk], (blk, 2 * blk))
            bias_ref[m] = pltpu.roll(win, 0, 1, stride=1, stride_axis=0)[:, blk:]

    @pl.when(i == 0)
    def _():
        k = k_ref[...]
        kb_ref[:, :HD_A] = k.astype(BF16)
        rblk = lax.broadcasted_iota(jnp.int32, (seq_len, HD_A), 0) // blk
        lane = lax.broadcasted_iota(jnp.int32, (seq_len, HD_A), 1)
        kb_ref[:, HD_A:] = jnp.where(lane == rblk, 1.0, 0.0).astype(BF16)
        vb_ref[...] = v_ref[...].astype(BF16)
        kmean_ref[...] = jnp.mean(k.reshape(n_blocks, blk, HD_A), axis=1)

    qs = q_ref[...] * (HD_A ** -0.5)
    qb = qs.astype(BF16)

    st = _dot_nt(kmean_ref[...], qs, precision=HIGHEST)
    rowi = lax.broadcasted_iota(jnp.int32, st.shape, 0)
    st = jnp.where(rowi < i, st, NEG_INF)
    rank = jnp.zeros(st.shape, F32)
    for c in range(n_blocks - 1):
        sc = st[c:c + 1, :]
        rank = rank + jnp.where((sc > st) | ((sc == st) & (c < rowi)), 1.0, 0.0)
    sel = jnp.where((rank < MOBA_TOPK) & (rowi < i), 1.0, 0.0)
    er = lax.broadcasted_iota(jnp.int32, (n_blocks, HD_A), 0)
    ec = lax.broadcasted_iota(jnp.int32, (n_blocks, HD_A), 1)
    selq = _dot_tn(sel, jnp.where(er == ec, 1.0, 0.0))
    lane = lax.broadcasted_iota(jnp.int32, selq.shape, 1)
    negm = jnp.where((lane < n_blocks) & (selq < 0.5), NEG_INF, 0.0)
    q_aug = jnp.concatenate([qb, negm.astype(BF16)], axis=1)

    row0 = pl.multiple_of(i * blk, blk)
    r = lax.broadcasted_iota(jnp.int32, (blk, blk), 0)
    c_ = lax.broadcasted_iota(jnp.int32, (blk, blk), 1)
    logits = _dot_nt(qb, kb_ref[pl.ds(row0, blk), :HD_A]) + bias_ref[0]
    logits = jnp.where(c_ <= r, logits, NEG_INF)
    m0 = jnp.max(logits, axis=-1, keepdims=True)
    p = jnp.exp(logits - m0)
    m_ref[...] = m0
    l_ref[...] = jnp.sum(p, axis=-1, keepdims=True)
    acc_ref[...] = _dot(p.astype(BF16), vb_ref[pl.ds(row0, blk), :])

    grp = MOBA_GROUP

    def group(t, carry):
        j0 = grp * t
        rows = pl.ds(pl.multiple_of(j0 * blk, grp * blk), grp * blk)
        lg = _dot_nt(q_aug, kb_ref[rows, :])
        lg = lg + jnp.concatenate([bias_ref[jnp.maximum(i - j0 - g, 0)] for g in range(grp)], axis=1)
        m_old = m_ref[...]
        m_new = jnp.maximum(m_old, jnp.max(lg, axis=-1, keepdims=True))
        alpha = jnp.exp(m_old - m_new)
        pj = jnp.exp(lg - m_new)
        m_ref[...] = m_new
        l_ref[...] = alpha * l_ref[...] + jnp.sum(pj, axis=-1, keepdims=True)
        acc_ref[...] = alpha * acc_ref[...] + _dot(pj.astype(BF16), vb_ref[rows, :])
        return carry

    lax.fori_loop(0, (i + grp - 1) // grp, group, 0)
    o_ref[...] = (acc_ref[...] / l_ref[...]).astype(o_ref.dtype)


def moba_prompt(proj, rel_bias, n_seq, seq_len, q_col, k_col, v_col):
    blk = MOBA_BLOCK
    nb = seq_len // blk
    assert nb % MOBA_GROUP == 0
    n_dist = (nb + 1) * blk
    rv = _rel_bias_table(rel_bias, nb * blk - jnp.arange(n_dist)).reshape(H_A, 1, n_dist)
    return pl.pallas_call(
        functools.partial(_moba_prompt_kernel, n_blocks=nb),
        grid=(H_A, n_seq, nb),
        in_specs=[pl.BlockSpec((blk, HD_A), lambda h, n, i: (n * nb + i, q_col + h)),
                  pl.BlockSpec((seq_len, HD_A), lambda h, n, i: (n, k_col + h)),
                  pl.BlockSpec((seq_len, HD_A), lambda h, n, i: (n, v_col + h)),
                  pl.BlockSpec((1, 1, n_dist), lambda h, n, i: (h, 0, 0))],
        out_specs=pl.BlockSpec((blk, HD_A), lambda h, n, i: (n * nb + i, h)),
        out_shape=jax.ShapeDtypeStruct((n_seq * seq_len, H_A * HD_A), BF16),
        scratch_shapes=[pltpu.VMEM((nb, blk, blk), F32),
                        pltpu.VMEM((seq_len, 2 * HD_A), BF16),
                        pltpu.VMEM((seq_len, HD_A), BF16),
                        pltpu.VMEM((nb, HD_A), F32),
                        pltpu.VMEM((blk, 1), F32),
                        pltpu.VMEM((blk, 1), F32),
                        pltpu.VMEM((blk, HD_A), F32)],
        compiler_params=_cparams(("arbitrary", "arbitrary", "arbitrary")),
        name="moba_prompt",
    )(proj, proj, proj, rv)


PAGES_PER_STEP = 16
BLOCKS_PER_STEP = PAGES_PER_STEP * PAGE_SIZE // MOBA_BLOCK


def _moba_sample_keys_kernel(pt_ref, qbd_ref, knew_ref, bias_ref, bown_ref, *rest,
                             n_steps, n_tok):
    pps = PAGES_PER_STEP
    bps = BLOCKS_PER_STEP
    ppb = pps // bps
    kp = rest[:pps]
    p_ref, pown_ref, lg_ref, ksum_ref = rest[pps:]
    s = pl.program_id(1)
    qbd = qbd_ref[0]
    qbd_b = qbd.astype(BF16)

    def lg_block(b):
        return b // bps, slice((b % bps) * MOBA_BLOCK, (b % bps + 1) * MOBA_BLOCK)

    sums = []
    for k in range(pps):
        hsum = []
        heads = []
        for h in range(H_A):
            kh = kp[k][0, 0, pl.ds(h, PAGE_SIZE, stride=H_A), :]
            heads.append(kh.astype(BF16))
            hsum.append(jnp.sum(kh, axis=0, keepdims=True))
        page = jnp.concatenate(heads, axis=1)
        lg = _dot_nt(qbd_b, page) + bias_ref[:, k * PAGE_SIZE:(k + 1) * PAGE_SIZE]
        lg_ref[s, :, k * PAGE_SIZE:(k + 1) * PAGE_SIZE] = lg
        sums.append(jnp.concatenate(hsum, axis=1))
    blocks = [sum(sums[ppb * b:ppb * (b + 1)]) * (1.0 / MOBA_BLOCK) for b in range(bps)]
    ksum_ref[pl.ds(pl.multiple_of(s * bps, bps), bps), :] = jnp.concatenate(blocks, axis=0)

    @pl.when(s == n_steps - 1)
    def _():
        n_blk = n_steps * bps
        sc = _dot_nt(qbd, ksum_ref[...], precision=HIGHEST)
        bidx = lax.broadcasted_iota(jnp.int32, sc.shape, 1)
        sel = jnp.zeros(sc.shape, F32)
        for _ in range(MOBA_TOPK):
            mx = jnp.max(sc, axis=-1, keepdims=True)
            first = jnp.min(jnp.where(sc == mx, bidx, n_blk), axis=-1, keepdims=True)
            hit = bidx == first
            sel = jnp.where(hit, 1.0, sel)
            sc = jnp.where(hit, 2.0 * NEG_INF, sc)

        lo = _dot_nt(qbd, knew_ref[...]) + bown_ref[...]
        kc = lax.broadcasted_iota(jnp.int32, lo.shape, 1)
        qr = lax.broadcasted_iota(jnp.int32, lo.shape, 0) % n_tok
        lo = jnp.where(kc <= qr, lo, NEG_INF)

        def masked(b):
            st, sl = lg_block(b)
            return jnp.where(sel[:, b:b + 1] > 0.0, lg_ref[st, :, sl], NEG_INF)

        run = masked(0)
        for b in range(1, n_blk):
            run = jnp.maximum(run, masked(b))
        m = jnp.maximum(jnp.max(lo, axis=-1, keepdims=True), jnp.max(run, axis=-1, keepdims=True))
        e_own = jnp.exp(lo - m)
        tot = jnp.zeros(run.shape, F32)
        for b in range(n_blk):
            st, sl = lg_block(b)
            e = jnp.exp(masked(b) - m)
            lg_ref[st, :, sl] = e
            tot = tot + e
        l = jnp.sum(e_own, axis=-1, keepdims=True) + jnp.sum(tot, axis=-1, keepdims=True)
        inv = 1.0 / l
        pown_ref[0] = e_own * inv
        for b in range(n_blk):
            st, sl = lg_block(b)
            p_ref[0, :, b * MOBA_BLOCK:(b + 1) * MOBA_BLOCK] = (lg_ref[st, :, sl] * inv).astype(BF16)


def _moba_sample_values_kernel(pt_ref, p_ref, pown_ref, vnew_ref, *rest, n_steps, n_tok):
    pps = PAGES_PER_STEP
    vp = rest[:pps]
    o_ref, acc_ref = rest[pps:]
    s = pl.program_id(1)

    @pl.when(s == 0)
    def _():
        acc_ref[...] = _dot(pown_ref[0], vnew_ref[...])

    acc = acc_ref[...]
    for k in range(pps):
        page = jnp.concatenate(
            [vp[k][0, 0, pl.ds(h, PAGE_SIZE, stride=H_A), :].astype(BF16) for h in range(H_A)], axis=1)
        acc = acc + _dot(p_ref[0, :, k * PAGE_SIZE:(k + 1) * PAGE_SIZE], page)
    acc_ref[...] = acc

    @pl.when(s == n_steps - 1)
    def _():
        for h in range(H_A):
            o_ref[:, h * HD_A:(h + 1) * HD_A] = acc[h * n_tok:(h + 1) * n_tok,
                                                    h * HD_A:(h + 1) * HD_A].astype(o_ref.dtype)


def moba_sample(proj, row_blk0, k_colblk, v_colblk, qbd, cache_k, cache_v, layer, page_table,
                bias_past, bias_own, n_tok):
    n_seq, n_pages = page_table.shape
    pps = PAGES_PER_STEP
    n_steps = n_pages // pps
    n_past = n_pages * PAGE_SIZE
    wa = H_A * HD_A
    nrow = H_A * n_tok
    page_block = (1, 1, PAGE_SIZE * H_A, HD_A)
    cache_k = cache_k.reshape(cache_k.shape[:2] + (PAGE_SIZE * H_A, HD_A))
    cache_v = cache_v.reshape(cache_v.shape[:2] + (PAGE_SIZE * H_A, HD_A))

    def pmap(k):
        return lambda b, s, pt: (layer, pt[b, s * pps + k], 0, 0)

    p, p_own = pl.pallas_call(
        functools.partial(_moba_sample_keys_kernel, n_steps=n_steps, n_tok=n_tok),
        grid_spec=pltpu.PrefetchScalarGridSpec(
            num_scalar_prefetch=1,
            grid=(n_seq, n_steps),
            in_specs=[pl.BlockSpec((1, nrow, wa), lambda b, s, pt: (b, 0, 0)),
                      pl.BlockSpec((n_tok, wa), lambda b, s, pt: (row_blk0 + b, k_colblk)),
                      pl.BlockSpec((nrow, pps * PAGE_SIZE), lambda b, s, pt: (0, s)),
                      pl.BlockSpec((nrow, n_tok), lambda b, s, pt: (0, 0))]
            + [pl.BlockSpec(page_block, pmap(k)) for k in range(pps)],
            out_specs=[pl.BlockSpec((1, nrow, n_past), lambda b, s, pt: (b, 0, 0)),
                       pl.BlockSpec((1, nrow, n_tok), lambda b, s, pt: (b, 0, 0))],
            scratch_shapes=[pltpu.VMEM((n_steps, nrow, pps * PAGE_SIZE), F32),
                            pltpu.VMEM((n_past // MOBA_BLOCK, wa), F32)]),
        out_shape=[jax.ShapeDtypeStruct((n_seq, nrow, n_past), BF16),
                   jax.ShapeDtypeStruct((n_seq, nrow, n_tok), F32)],
        compiler_params=_cparams(("arbitrary", "arbitrary")),
        name="moba_sample_keys",
    )(page_table, qbd, proj, bias_past, bias_own, *([cache_k] * pps))

    return pl.pallas_call(
        functools.partial(_moba_sample_values_kernel, n_steps=n_steps, n_tok=n_tok),
        grid_spec=pltpu.PrefetchScalarGridSpec(
            num_scalar_prefetch=1,
            grid=(n_seq, n_steps),
            in_specs=[pl.BlockSpec((1, nrow, pps * PAGE_SIZE), lambda b, s, pt: (b, 0, s)),
                      pl.BlockSpec((1, nrow, n_tok), lambda b, s, pt: (b, 0, 0)),
                      pl.BlockSpec((n_tok, wa), lambda b, s, pt: (row_blk0 + b, v_colblk))]
            + [pl.BlockSpec(page_block, pmap(k)) for k in range(pps)],
            out_specs=pl.BlockSpec((n_tok, wa), lambda b, s, pt: (b, 0)),
            scratch_shapes=[pltpu.VMEM((nrow, wa), F32)]),
        out_shape=jax.ShapeDtypeStruct((n_seq * n_tok, wa), BF16),
        compiler_params=_cparams(("arbitrary", "arbitrary")),
        name="moba_sample_values",
    )(page_table, p, p_own, proj, *([cache_v] * pps))


def _gdn_prep_kernel(x_ref, halo_ref, ab_ref, cw_ref, alog_ref, dtb_ref,
                     q_ref, k_ref, v_ref, gb_ref, *, tt):
    x = x_ref[...]
    xf = jnp.concatenate([halo_ref[0], x], axis=0)
    cw = cw_ref[...]
    conv = xf[5:5 + tt] * cw[0:1]
    for w in range(1, CONV_WIDTH):
        conv = conv + xf[5 + w:5 + w + tt] * cw[w:w + 1]
    act = conv * jax.nn.sigmoid(conv)
    wk = H_B * DK_B
    for h in range(H_B):
        q = act[:, h * DK_B:(h + 1) * DK_B]
        k = act[:, wk + h * DK_B:wk + (h + 1) * DK_B]
        q_ref[0, h] = q * lax.rsqrt(jnp.sum(q * q, axis=-1, keepdims=True) + RMS_EPS) * (DK_B ** -0.5)
        k_ref[0, h] = k * lax.rsqrt(jnp.sum(k * k, axis=-1, keepdims=True) + RMS_EPS)
        v_ref[0, h] = act[:, 2 * wk + h * DK_B:2 * wk + (h + 1) * DK_B]
    ab = ab_ref[...]
    t = ab + dtb_ref[...]
    sp = jnp.maximum(t, 0.0) + jnp.log(1.0 + jnp.exp(-jnp.abs(t)))
    g = -jnp.exp(alog_ref[...]) * sp
    lane = lax.broadcasted_iota(jnp.int32, ab.shape, 1)
    gb_ref[0] = jnp.where(lane < H_B, g, jax.nn.sigmoid(ab))


def gdn_prep(proj, ab, halo, conv_w, a_log, dt_bias, row0, n_seq, seq_len, tt, qkv_colblk):
    w3 = conv_w.shape[1]
    nt = seq_len // tt
    rb0 = row0 // tt
    pad = jnp.zeros((1, 128 - H_B), F32)
    alog = jnp.concatenate([a_log.reshape(1, H_B), pad], axis=1)
    dtb = jnp.concatenate([dt_bias.reshape(1, H_B), pad], axis=1)
    hm = jax.ShapeDtypeStruct((n_seq, H_B, seq_len, DK_B), F32)
    hspec = pl.BlockSpec((1, H_B, tt, DK_B), lambda n, t: (n, 0, t, 0))
    return pl.pallas_call(
        functools.partial(_gdn_prep_kernel, tt=tt),
        grid=(n_seq, nt),
        in_specs=[pl.BlockSpec((tt, w3), lambda n, t: (rb0 + n * nt + t, qkv_colblk)),
                  pl.BlockSpec((1, 8, w3), lambda n, t: (n * nt + t, 0, 0)),
                  pl.BlockSpec((tt, 128), lambda n, t: (rb0 + n * nt + t, 0)),
                  pl.BlockSpec((CONV_WIDTH, w3), lambda n, t: (0, 0)),
                  pl.BlockSpec((1, 128), lambda n, t: (0, 0)),
                  pl.BlockSpec((1, 128), lambda n, t: (0, 0))],
        out_specs=[hspec, hspec, hspec,
                   pl.BlockSpec((1, tt, 128), lambda n, t: (n, t, 0))],
        out_shape=[hm, hm, hm, jax.ShapeDtypeStruct((n_seq, seq_len, 128), F32)],
        compiler_params=_cparams(("parallel", "parallel")),
        name="gdn_prep",
    )(proj, halo, ab, conv_w, alog, dtb)


def _split2(a):
    hi = a.astype(BF16)
    return hi, (a - hi.astype(F32)).astype(BF16)


def _split3(a):
    hi = a.astype(BF16)
    r1 = a - hi.astype(F32)
    mid = r1.astype(BF16)
    return hi, mid, (r1 - mid.astype(F32)).astype(BF16)


def _dot3(a, b):
    ah, al = _split2(a)
    bh, bl = _split2(b)
    return _dot(ah, bh) + (_dot(ah, bl) + _dot(al, bh))


def _dot_exact_lhs(a_bf16, b):
    b1, b2, b3 = _split3(b)
    return _dot(a_bf16, b1) + (_dot(a_bf16, b2) + _dot(a_bf16, b3))


def _unit_lower_inverse(a_list, c):
    r = lax.broadcasted_iota(jnp.int32, (c, c), 0)
    cc = lax.broadcasted_iota(jnp.int32, (c, c), 1)
    eye = jnp.where(r == cc, 1.0, 0.0)
    base = min(c, 8)
    n1 = [jnp.where((r // base) == (cc // base), -a, 0.0) for a in a_list]
    n2 = [_dot3(x, x) for x in n1]
    n4 = [_dot3(x, x) for x in n2]
    t = [_dot3(eye + x, eye + y) for x, y in zip(n1, n2)]
    t = [_dot3(x, eye + y) for x, y in zip(t, n4)]
    b = base
    while b < c:
        inner = ((r // (2 * b)) == (cc // (2 * b))) & ((r // b) != (cc // b))
        left = [_dot3(x, jnp.where(inner, a, 0.0)) for x, a in zip(t, a_list)]
        t = [x - _dot3(y, x) for x, y in zip(t, left)]
        b *= 2
    return t


def _gdn_chunk_kernel(*refs, nb, c):
    q_ref, k_ref, v_ref, gb_ref = refs[:4]
    z_refs = refs[4:4 + nb]
    ng_ref, s0_ref, o_ref, sout_ref, s_ref = refs[4 + nb:]
    ci = pl.program_id(1)

    @pl.when(ci == 0)
    def _():
        s_ref[...] = s0_ref[...]

    r = lax.broadcasted_iota(jnp.int32, (c, c), 0)
    cc = lax.broadcasted_iota(jnp.int32, (c, c), 1)
    incl = r >= cc
    strict = r > cc
    ltri = jnp.where(incl, 1.0, 0.0).astype(BF16)
    eye = jnp.where(r == cc, 1.0, 0.0)
    ones = jnp.ones((c, c), BF16)
    ng = ng_ref[...]
    ch = [(n, h) for n in range(nb) for h in range(H_B)]
    gbv = [gb_ref[n] for n in range(nb)]
    q = [q_ref[n, h] for n, h in ch]
    k = [k_ref[n, h] for n, h in ch]
    v = [v_ref[n, h] for n, h in ch]
    gcol = [jnp.broadcast_to(gbv[n][:, h:h + 1], (c, DK_B)) for n, h in ch]
    bcol = [jnp.broadcast_to(gbv[n][:, H_B + h:H_B + h + 1], (c, DK_B)) for n, h in ch]
    big_g = [_dot_exact_lhs(ltri, g) for g in gcol]
    gj = [_dot_exact_lhs(ones, g[:, :c] * eye) for g in big_g]
    decay = [jnp.where(incl, jnp.exp(jnp.minimum(g[:, :c] - x, 0.0)), 0.0) for g, x in zip(big_g, gj)]
    kk = [_dot_nt(x, x) for x in k]
    a = [jnp.where(strict, b[:, :c] * x * d, 0.0) for b, x, d in zip(bcol, kk, decay)]
    t = _unit_lower_inverse(a, c)
    gam = [jnp.exp(g) for g in big_g]
    u = [_dot3(x, b * y) for x, b, y in zip(t, bcol, v)]
    w = [_dot3(x, b * g * y) for x, b, g, y in zip(t, bcol, gam, k)]
    qk = [_dot_nt(x, y) * d for x, y, d in zip(q, k, decay)]
    g_last = [g[c - 1:c, :] for g in big_g]
    k_tail = [jnp.exp(gl - g) * y for gl, g, y in zip(g_last, big_g, k)]
    s = [s_ref[n, h] for n, h in ch]
    uu = [x - _dot(y, z_) for x, y, z_ in zip(u, w, s)]
    o = [_dot(g * x, z_) + _dot(y, x2) for g, x, z_, y, x2 in zip(gam, q, s, qk, uu)]
    s_new = [jnp.exp(gl) * z_ + _dot_tn(y, x) for gl, z_, y, x in zip(g_last, s, k_tail, uu)]
    for (n, h), x in zip(ch, s_new):
        s_ref[n, h] = x
    for (n, h), x in zip(ch, o):
        x = x * lax.rsqrt(jnp.mean(x * x, axis=-1, keepdims=True) + RMS_EPS) * ng
        zh = z_refs[n][:, h * DK_B:(h + 1) * DK_B]
        o_ref[n, :, h * DK_B:(h + 1) * DK_B] = (x * (zh * jax.nn.sigmoid(zh))).astype(o_ref.dtype)

    @pl.when(ci == pl.num_programs(1) - 1)
    def _():
        sout_ref[...] = s_ref[...]


def gdn_chunks(q, k, v, gb, proj, z_colblk, row0, norm_g, s0, nb, c):
    n_seq, _, seq_len, _ = q.shape
    nc = seq_len // c
    wv = H_B * DK_B
    rb0 = row0 // c
    hspec = pl.BlockSpec((nb, H_B, c, DK_B), lambda n, ci: (n, 0, ci, 0))
    sspec = pl.BlockSpec((nb, H_B, DK_B, DK_B), lambda n, ci: (n, 0, 0, 0))

    def zmap(j):
        return lambda n, ci: (rb0 + (n * nb + j) * nc + ci, z_colblk)

    return pl.pallas_call(
        functools.partial(_gdn_chunk_kernel, nb=nb, c=c),
        grid=(n_seq // nb, nc),
        in_specs=[hspec, hspec, hspec,
                  pl.BlockSpec((nb, c, 128), lambda n, ci: (n, ci, 0))]
        + [pl.BlockSpec((c, wv), zmap(j)) for j in range(nb)]
        + [pl.BlockSpec((1, DK_B), lambda n, ci: (0, 0)), sspec],
        out_specs=[pl.BlockSpec((nb, c, wv), lambda n, ci: (n, ci, 0)), sspec],
        out_shape=[jax.ShapeDtypeStruct((n_seq, seq_len, wv), BF16),
                   jax.ShapeDtypeStruct((n_seq, H_B, DK_B, DK_B), F32)],
        scratch_shapes=[pltpu.VMEM((nb, H_B, DK_B, DK_B), F32)],
        compiler_params=_cparams(("parallel", "arbitrary")),
        name="gdn_chunks",
    )(q, k, v, gb, *([proj] * nb), norm_g.reshape(1, DK_B), s0)


def _merge_kernel(oa_ref, ob_ref, wa_ref, wb_ref, ga_ref, gb_ref, o_ref):
    ya = _dot(oa_ref[...], wa_ref[...])
    yb = _dot(ob_ref[...], wb_ref[...])
    o_ref[...] = (jax.nn.sigmoid(ga_ref[...]) * ya
                  + jax.nn.sigmoid(gb_ref[...]) * yb).astype(o_ref.dtype)


def branch_merge(out_a, out_b, wa, wb, proj, ga_col0, gb_col0, tm, tn):
    m, ka = out_a.shape
    n = wa.shape[1]
    ga_blk, gb_blk = ga_col0 // tn, gb_col0 // tn
    assert ga_blk * tn == ga_col0 and gb_blk * tn == gb_col0
    return pl.pallas_call(
        _merge_kernel,
        grid=(m // tm, n // tn),
        in_specs=[pl.BlockSpec((tm, ka), lambda i, j: (i, 0)),
                  pl.BlockSpec((tm, ka), lambda i, j: (i, 0)),
                  pl.BlockSpec((ka, tn), lambda i, j: (0, j)),
                  pl.BlockSpec((ka, tn), lambda i, j: (0, j)),
                  pl.BlockSpec((tm, tn), lambda i, j: (i, ga_blk + j)),
                  pl.BlockSpec((tm, tn), lambda i, j: (i, gb_blk + j))],
        out_specs=pl.BlockSpec((tm, tn), lambda i, j: (i, j)),
        out_shape=jax.ShapeDtypeStruct((m, n), BF16),
        compiler_params=_cparams(("parallel", "parallel")),
        name="branch_merge",
    )(out_a, out_b, wa, wb, proj, proj)


def _matmul_residual_kernel(x_ref, w_ref, r_ref, o_ref):
    o_ref[...] = r_ref[...] + _dot(x_ref[...], w_ref[...])


def matmul_residual(x, w, res, tm, tn):
    m, k = x.shape
    n = w.shape[1]
    return pl.pallas_call(
        _matmul_residual_kernel,
        grid=(m // tm, n // tn),
        in_specs=[pl.BlockSpec((tm, k), lambda i, j: (i, 0)),
                  pl.BlockSpec((k, tn), lambda i, j: (0, j)),
                  pl.BlockSpec((tm, tn), lambda i, j: (i, j))],
        out_specs=pl.BlockSpec((tm, tn), lambda i, j: (i, j)),
        out_shape=jax.ShapeDtypeStruct((m, n), F32),
        compiler_params=_cparams(("parallel", "parallel")),
        name="out_proj",
    )(x, w, res)


MOE_TA = 256


def _lane_rotate(x, shift):
    parts = [pltpu.roll(x[:, c:c + 128], shift, 1) for c in range(0, x.shape[1], 128)]
    return parts[0] if len(parts) == 1 else jnp.concatenate(parts, axis=1)


def _new_expert(be_ref, b):
    return (b == 0) | (be_ref[b] != be_ref[jnp.maximum(b - 1, 0)])


def _moe_up_kernel(be_ref, nu_ref, x_ref, wa_ref, wb_ref, ba_ref, bb_ref, o_ref, wab_ref, wbb_ref):
    b = pl.program_id(1)

    @pl.when(b < nu_ref[0])
    def _():
        @pl.when(_new_expert(be_ref, b))
        def _():
            wab_ref[...] = wa_ref[0].astype(BF16)
            wbb_ref[...] = wb_ref[0].astype(BF16)

        half = x_ref.shape[1]
        x_hi, x_lo = _unpack_bf16_pair(x_ref[...])
        ga = (_dot(x_hi, wab_ref[:half, :]) + _dot(x_lo, wab_ref[half:, :])
              + ba_ref[0])
        gb = _dot(x_hi, wbb_ref[:half, :]) + _dot(x_lo, wbb_ref[half:, :]) + bb_ref[0]
        even = (lax.broadcasted_iota(jnp.int32, ga.shape, 1) % 2) == 0
        gate = jnp.where(even, ga, _lane_rotate(gb, 1))
        up = jnp.where(even, _lane_rotate(ga, 127), gb)
        gl = jnp.minimum(gate, SWIGLU_LIMIT)
        up = jnp.clip(up, -SWIGLU_LIMIT, SWIGLU_LIMIT)
        o_ref[...] = (gl * jax.nn.sigmoid(SWIGLU_ALPHA * gl) * (up + 1.0)).astype(o_ref.dtype)


def _bf16_bits(w):
    return lax.bitcast_convert_type(w.astype(BF16).astype(F32), jnp.uint32)


def _moe_down_kernel(be_ref, nu_ref, a_ref, wd_ref, bd_ref, o_ref, wdb_ref):
    b = pl.program_id(1)

    @pl.when(b < nu_ref[0])
    def _():
        @pl.when(_new_expert(be_ref, b))
        def _():
            f = wd_ref.shape[1]
            for n in range(f // (2 * MOE_TA)):
                wa = wd_ref[0, n * MOE_TA:(n + 1) * MOE_TA, :]
                wb = wd_ref[0, f // 2 + n * MOE_TA:f // 2 + (n + 1) * MOE_TA, :]
                pair = (_bf16_bits(wa) >> 16) | _bf16_bits(wb)
                wdb_ref[n * 2 * MOE_TA:(n + 1) * 2 * MOE_TA, :] = pltpu.bitcast(pair, BF16)

        o_ref[...] = _dot(a_ref[...], wdb_ref[...]) + bd_ref[0]


def moe_experts(x_rows, block_e, n_used, w_gate_up, b_gate_up, w_down, b_down, tn_down):
    rows = x_rows.shape[0]
    n_exp, d, f2 = w_gate_up.shape
    f = f2 // 2
    dm = w_down.shape[2]
    nblk = rows // MOE_ROWS
    nt = f // (2 * MOE_TA)
    tw = 2 * MOE_TA

    def rowmap(n, b, be, nu):
        return (jnp.minimum(b, nu[0] - 1), 0)

    def outmap(n, b, be, nu):
        return (jnp.minimum(b, nu[0] - 1), n)

    def wmap(off):
        return lambda n, b, be, nu: (be[jnp.minimum(b, nu[0] - 1)], 0, off + n)

    bgu = b_gate_up.reshape(n_exp, 1, f2)
    act = pl.pallas_call(
        _moe_up_kernel,
        grid_spec=pltpu.PrefetchScalarGridSpec(
            num_scalar_prefetch=2,
            grid=(nt, nblk),
            in_specs=[pl.BlockSpec((MOE_ROWS, d // 2), rowmap),
                      pl.BlockSpec((1, d, tw), wmap(0)),
                      pl.BlockSpec((1, d, tw), wmap(nt)),
                      pl.BlockSpec((1, 1, tw), wmap(0)),
                      pl.BlockSpec((1, 1, tw), wmap(nt))],
            out_specs=pl.BlockSpec((MOE_ROWS, tw), outmap),
            scratch_shapes=[pltpu.VMEM((d, tw), BF16), pltpu.VMEM((d, tw), BF16)]),
        out_shape=jax.ShapeDtypeStruct((rows, f), BF16),
        compiler_params=_cparams(("arbitrary", "arbitrary")),
        name="moe_up",
    )(block_e, n_used, x_rows, w_gate_up, w_gate_up, bgu, bgu)
    return pl.pallas_call(
        _moe_down_kernel,
        grid_spec=pltpu.PrefetchScalarGridSpec(
            num_scalar_prefetch=2,
            grid=(dm // tn_down, nblk),
            in_specs=[pl.BlockSpec((MOE_ROWS, f), rowmap),
                      pl.BlockSpec((1, f, tn_down), wmap(0)),
                      pl.BlockSpec((1, 1, tn_down), wmap(0))],
            out_specs=pl.BlockSpec((MOE_ROWS, tn_down), outmap),
            scratch_shapes=[pltpu.VMEM((f, tn_down), BF16)]),
        out_shape=jax.ShapeDtypeStruct((rows, dm), F32),
        compiler_params=_cparams(("arbitrary", "arbitrary")),
        name="moe_down",
    )(block_e, n_used, act, w_down, b_down.reshape(n_exp, 1, dm))


def _final_kernel(y_ref, *rest):
    e_refs = rest[:TOP_K]
    gate_ref, g_ref, o_ref = rest[TOP_K:]
    x = y_ref[...]
    for kk in range(TOP_K):
        x = x + e_refs[kk][...] * gate_ref[:, kk:kk + 1]
    o_ref[...] = x * lax.rsqrt(jnp.mean(x * x, axis=-1, keepdims=True) + RMS_EPS) * g_ref[...]


def combine_residual_norm(y, expert_out, gate, g, tm):
    m, d = y.shape
    nt = m // tm
    return pl.pallas_call(
        _final_kernel,
        grid=(nt,),
        in_specs=[pl.BlockSpec((tm, d), lambda i: (i, 0))]
        + [pl.BlockSpec((tm, d), functools.partial(lambda i, kk: (kk * nt + i, 0), kk=kk))
           for kk in range(TOP_K)]
        + [pl.BlockSpec((tm, TOP_K), lambda i: (i, 0)),
           pl.BlockSpec((1, d), lambda i: (0, 0))],
        out_specs=pl.BlockSpec((tm, d), lambda i: (i, 0)),
        out_shape=jax.ShapeDtypeStruct((m, d), F32),
        compiler_params=_cparams(("parallel",)),
        name="combine_norm",
    )(y, *([expert_out] * TOP_K), gate, g.reshape(1, d))


def _rel_bias_table(rel_bias, dist):
    n = jnp.maximum(dist, 0)
    max_exact = N_BUCKETS // 2
    nf = jnp.maximum(n, max_exact).astype(F32)
    large = max_exact + (jnp.log(nf / max_exact) / math.log(MAX_DISTANCE / max_exact)
                         * (N_BUCKETS - max_exact)).astype(jnp.int32)
    bucket = jnp.where(n < max_exact, n, jnp.minimum(large, N_BUCKETS - 1))
    return rel_bias[:, bucket].astype(F32)


def _route(logits, n_tok):
    top_v, top_e = lax.top_k(logits, TOP_K)
    gate = jax.nn.softmax(top_v, axis=-1)
    n_exp = logits.shape[1]
    n_assign = n_tok * TOP_K
    flat_e = top_e.reshape(-1)
    order = jnp.argsort(flat_e, stable=True)
    e_sorted = flat_e[order]
    tok_sorted = (order // TOP_K).astype(jnp.int32)
    counts = jnp.zeros((n_exp,), jnp.int32).at[flat_e].add(1)
    padded = (counts + MOE_ROWS - 1) // MOE_ROWS * MOE_ROWS
    pad_end = jnp.cumsum(padded)
    pad_start = pad_end - padded
    start = jnp.cumsum(counts) - counts
    dest = (pad_start[e_sorted] + jnp.arange(n_assign) - start[e_sorted]).astype(jnp.int32)
    n_blocks = -(-n_assign // MOE_ROWS) + n_exp
    row_tok = jnp.zeros((n_blocks * MOE_ROWS,), jnp.int32).at[dest].set(tok_sorted)
    block_e = jnp.minimum(jnp.searchsorted(pad_end, jnp.arange(n_blocks) * MOE_ROWS, side='right'),
                          n_exp - 1).astype(jnp.int32)
    n_used = (pad_end[-1] // MOE_ROWS).astype(jnp.int32).reshape(1)
    pos = jnp.zeros((n_assign,), jnp.int32).at[order].set(dest).reshape(n_tok, TOP_K)
    return gate, row_tok, block_e, n_used, pos


def kernel(x_prompt, x_sample, cache_k, cache_v, state_gdn, state_conv, page_table, norm_mix_g, w_in, rel_bias, conv_w, a_log, dt_bias, gdn_norm_g, w_branch_a, w_branch_b, w_out, norm_ffn_g, router_w, router_b, w_gate_up, b_gate_up, w_down, b_down, norm_final_g):
    n_p, seq, d = x_prompt.shape
    n_s, dseq, _ = x_sample.shape
    assert w_in.shape[0] == 1
    l = 0
    wa = H_A * HD_A
    wk = H_B * DK_B
    qkv_b = 3 * wk
    n_prow = n_p * seq
    n_srow = n_s * dseq
    n_tok = n_prow + n_srow
    past = page_table.shape[1] * PAGE_SIZE
    assert past % MOBA_BLOCK == 0 and seq % MOBA_BLOCK == 0

    x = jnp.concatenate([x_prompt.reshape(n_prow, d), x_sample.reshape(n_srow, d)], axis=0)

    w = w_in[l]
    c_z = 3 * wa + qkv_b
    c_ab = c_z + wk
    c_g = c_ab + 2 * H_B
    w_main = jnp.concatenate([w[:, :c_ab], w[:, c_g:]], axis=1).astype(BF16)
    w_ab = jnp.concatenate([w[:, c_ab:c_g], jnp.zeros((d, 128 - 2 * H_B), F32)], axis=1).astype(BF16)
    tm = 768
    proj = norm_matmul(x, norm_mix_g[l], w_main, tm, 1024)
    ab = norm_matmul(x, norm_mix_g[l], w_ab, tm, 128)

    k_all = proj[:, wa:2 * wa]
    v_all = proj[:, 2 * wa:3 * wa]

    out_a_p = moba_prompt(proj, rel_bias, n_p, seq, 0, H_A, 2 * H_A)

    q_s = proj[n_prow:, :wa].reshape(n_s, dseq, H_A, HD_A) * (HD_A ** -0.5)
    qbd = jnp.einsum('blhd,hg->bhlgd', q_s, jnp.eye(H_A, dtype=F32)).reshape(n_s, H_A * dseq, wa)
    n_dist = past + dseq
    far_first = _rel_bias_table(rel_bias, n_dist - 1 - jnp.arange(n_dist))
    bias_past = jnp.stack([far_first[:, dseq - 1 - t:dseq - 1 - t + past] for t in range(dseq)], axis=1)
    bias_past = bias_past.reshape(H_A * dseq, past)
    bias_own = _rel_bias_table(rel_bias, jnp.arange(dseq)[:, None] - jnp.arange(dseq)[None, :])
    bias_own = bias_own.reshape(H_A * dseq, dseq)
    out_a_s = moba_sample(proj, n_prow // dseq, 1, 2, qbd, cache_k, cache_v, l, page_table,
                          bias_past, bias_own, dseq)
    out_a = jnp.concatenate([out_a_p, out_a_s], axis=0)

    qkv_colblk = 3 * wa // qkv_b
    assert qkv_colblk * qkv_b == 3 * wa
    tt = 512
    raw_p = proj[:n_prow, 3 * wa:3 * wa + qkv_b].reshape(n_p, seq // tt, tt, qkv_b)
    halo0 = jnp.zeros((n_p, 1, 8, qkv_b), F32)
    halo_p = jnp.concatenate([halo0, raw_p[:, :-1, tt - 8:, :]], axis=1).reshape(n_p * (seq // tt), 8, qkv_b)
    qp, kp, vp, gbp = gdn_prep(proj, ab, halo_p, conv_w[l], a_log[l], dt_bias[l],
                               0, n_p, seq, tt, qkv_colblk)
    raw_s = proj[n_prow:, 3 * wa:3 * wa + qkv_b].reshape(n_s, dseq, qkv_b)
    halo_s = jnp.concatenate([jnp.zeros((n_s, 8 - (CONV_WIDTH - 1), qkv_b), F32), state_conv[l]], axis=1)
    qs_, ks_, vs_, gbs = gdn_prep(proj, ab, halo_s, conv_w[l], a_log[l], dt_bias[l],
                                  n_prow, n_s, dseq, dseq, qkv_colblk)
    z_colblk = c_z // wk
    assert z_colblk * wk == c_z
    c_p = math.gcd(seq, GDN_CHUNK)
    out_b_p, s_p = gdn_chunks(qp, kp, vp, gbp, proj, z_colblk, 0, gdn_norm_g[l],
                              jnp.zeros((n_p, H_B, DK_B, DK_B), F32), n_p, c_p)
    c_s = math.gcd(dseq, GDN_CHUNK)
    out_b_s, s_s = gdn_chunks(qs_, ks_, vs_, gbs, proj, z_colblk, n_prow, gdn_norm_g[l],
                              state_gdn[l], 2, c_s)
    out_b = jnp.concatenate([out_b_p.reshape(n_prow, wk), out_b_s.reshape(n_srow, wk)], axis=0)
    conv_p = raw_p.reshape(n_p, seq, qkv_b)[:, seq - (CONV_WIDTH - 1):, :]
    conv_s = jnp.concatenate([state_conv[l], raw_s], axis=1)[:, dseq:, :]

    merged = branch_merge(out_a, out_b, w_branch_a[l].astype(BF16), w_branch_b[l].astype(BF16),
                          proj, c_ab, c_ab + d, tm, 1024)
    y1 = matmul_residual(merged, w_out[l].astype(BF16), x, tm, 1024)

    n_exp = router_w.shape[2]
    w_r = jnp.concatenate([router_w[l], jnp.zeros((d, 128 - n_exp), F32)], axis=1).astype(BF16)
    r_logits, h2 = norm_matmul(y1, norm_ffn_g[l], w_r, tm, 128, emit_h=True)
    logits = r_logits[:, :n_exp] + router_b[l].astype(F32)
    gate, row_tok, block_e, n_used, pos = _route(logits, n_tok)
    x_rows = h2[row_tok]
    y_rows = moe_experts(x_rows, block_e, n_used, w_gate_up.reshape(w_gate_up.shape[1:]),
                         b_gate_up.reshape(b_gate_up.shape[1:]), w_down.reshape(w_down.shape[1:]),
                         b_down.reshape(b_down.shape[1:]), 1024)
    picked = y_rows[pos.T.reshape(-1)]
    y = combine_residual_norm(y1, picked, gate, norm_final_g, 384)

    y_prompt = y[:n_prow].reshape(n_p, seq, d)
    y_sample = y[n_prow:].reshape(n_s, dseq, d)
    k_p = k_all[:n_prow].reshape(1, n_p, seq, H_A, HD_A)
    v_p = v_all[:n_prow].reshape(1, n_p, seq, H_A, HD_A)
    k_s = k_all[n_prow:].reshape(1, n_s, dseq, H_A, HD_A)
    v_s = v_all[n_prow:].reshape(1, n_s, dseq, H_A, HD_A)
    return (y_prompt, y_sample, k_p, v_p, s_p[None], conv_p[None],
            k_s, v_s, s_s[None], conv_s[None])
```

```python
import functools
import math

import jax
import jax.numpy as jnp
from jax import lax
from jax.experimental import pallas as pl
from jax.experimental.pallas import tpu as pltpu

F32 = jnp.float32
BF16 = jnp.bfloat16
HIGHEST = lax.Precision.HIGHEST

RMS_EPS = 1e-6
NEG_INF = -1e30

H_A = 8
HD_A = 128
MOBA_BLOCK = 256
MOBA_TOPK = 3
MOBA_GROUP = 4
PAGE_SIZE = 128
N_BUCKETS = 32
MAX_DISTANCE = 4096

H_B = 8
DK_B = 128
CONV_WIDTH = 4
GDN_CHUNK = 64

TOP_K = 4
SWIGLU_LIMIT = 7.0
SWIGLU_ALPHA = 1.702
MOE_ROWS = 512

VMEM_LIMIT = 56 * 1024 * 1024


def _cparams(sem):
    return pltpu.CompilerParams(dimension_semantics=sem, vmem_limit_bytes=VMEM_LIMIT)


def _dot(a, b, precision=None):
    return jnp.dot(a, b, preferred_element_type=F32, precision=precision)


def _dot_nt(a, b, precision=None):
    return lax.dot_general(a, b, (((1,), (1,)), ((), ())),
                           preferred_element_type=F32, precision=precision)


def _dot_tn(a, b, precision=None):
    return lax.dot_general(a, b, (((0,), (0,)), ((), ())),
                           preferred_element_type=F32, precision=precision)


def _norm_matmul_kernel(x_ref, g_ref, w_ref, o_ref, *rest, emit_h):
    if emit_h:
        h_out_ref, h_ref = rest
    else:
        (h_ref,) = rest

    @pl.when(pl.program_id(1) == 0)
    def _():
        x = x_ref[...]
        y = x * lax.rsqrt(jnp.mean(x * x, axis=-1, keepdims=True) + RMS_EPS)
        h_ref[...] = (y * g_ref[...]).astype(BF16)

    if emit_h:
        half = h_ref.shape[1] // 2
        h_out_ref[...] = _pack_bf16_pair(h_ref[:, :half], h_ref[:, half:])
    o_ref[...] = _dot(h_ref[...], w_ref[...])


def _pack_bf16_pair(hi, lo):
    hi_bits = lax.bitcast_convert_type(hi.astype(F32), jnp.uint32)
    lo_bits = lax.bitcast_convert_type(lo.astype(F32), jnp.uint32)
    return lax.bitcast_convert_type(hi_bits | (lo_bits >> 16), F32)


def _unpack_bf16_pair(words):
    bits = lax.bitcast_convert_type(words, jnp.uint32)
    hi = lax.bitcast_convert_type(bits & jnp.uint32(0xFFFF0000), F32)
    lo = lax.bitcast_convert_type(bits << 16, F32)
    return hi.astype(BF16), lo.astype(BF16)


def norm_matmul(x, g, w, tm, tn, emit_h=False):
    m, k = x.shape
    n = w.shape[1]
    out_shape = [jax.ShapeDtypeStruct((m, n), F32)]
    out_specs = [pl.BlockSpec((tm, tn), lambda i, j: (i, j))]
    if emit_h:
        out_shape.append(jax.ShapeDtypeStruct((m, k // 2), F32))
        out_specs.append(pl.BlockSpec((tm, k // 2), lambda i, j: (i, 0)))
    res = pl.pallas_call(
        functools.partial(_norm_matmul_kernel, emit_h=emit_h),
        grid=(m // tm, n // tn),
        in_specs=[pl.BlockSpec((tm, k), lambda i, j: (i, 0)),
                  pl.BlockSpec((1, k), lambda i, j: (0, 0)),
                  pl.BlockSpec((k, tn), lambda i, j: (0, j))],
        out_specs=out_specs,
        out_shape=out_shape,
        scratch_shapes=[pltpu.VMEM((tm, k), BF16)],
        compiler_params=_cparams(("parallel", "arbitrary")),
        name="norm_matmul",
    )(x, g.reshape(1, k), w)
    return res if emit_h else res[0]


def _moba_prompt_kernel(q_ref, k_ref, v_ref, rv_ref, o_ref,
                        bias_ref, kb_ref, vb_ref, kmean_ref, m_ref, l_ref, acc_ref,
                        *, n_blocks):
    i = pl.program_id(2)
    blk = MOBA_BLOCK
    seq_len = n_blocks * blk

    @pl.when((pl.program_id(1) == 0) & (i == 0))
    def _():
        for m in range(n_blocks):
            st = blk * (n_blocks - 1 - m)
            win = jnp.broadcast_to(rv_ref[0, :, st:st + 2 * blk], (blk, 2 * blk))
            bias_ref[m] = pltpu.roll(win, 0, 1, stride=1, stride_axis=0)[:, blk:]

    @pl.when(i == 0)
    def _():
        k = k_ref[...]
        kb_ref[:, :HD_A] = k.astype(BF16)
        rblk = lax.broadcasted_iota(jnp.int32, (seq_len, HD_A), 0) // blk
        lane = lax.broadcasted_iota(jnp.int32, (seq_len, HD_A), 1)
        kb_ref[:, HD_A:] = jnp.where(lane == rblk, 1.0, 0.0).astype(BF16)
        vb_ref[...] = v_ref[...].astype(BF16)
        kmean_ref[...] = jnp.mean(k.reshape(n_blocks, blk, HD_A), axis=1)

    qs = q_ref[...] * (HD_A ** -0.5)
    qb = qs.astype(BF16)

    st = _dot_nt(kmean_ref[...], qs, precision=HIGHEST)
    rowi = lax.broadcasted_iota(jnp.int32, st.shape, 0)
    st = jnp.where(rowi < i, st, NEG_INF)
    rank = jnp.zeros(st.shape, F32)
    for c in range(n_blocks - 1):
        sc = st[c:c + 1, :]
        rank = rank + jnp.where((sc > st) | ((sc == st) & (c < rowi)), 1.0, 0.0)
    sel = jnp.where((rank < MOBA_TOPK) & (rowi < i), 1.0, 0.0)
    er = lax.broadcasted_iota(jnp.int32, (n_blocks, HD_A), 0)
    ec = lax.broadcasted_iota(jnp.int32, (n_blocks, HD_A), 1)
    selq = _dot_tn(sel, jnp.where(er == ec, 1.0, 0.0))
    lane = lax.broadcasted_iota(jnp.int32, selq.shape, 1)
    negm = jnp.where((lane < n_blocks) & (selq < 0.5), NEG_INF, 0.0)
    q_aug = jnp.concatenate([qb, negm.astype(BF16)], axis=1)

    row0 = pl.multiple_of(i * blk, blk)
    r = lax.broadcasted_iota(jnp.int32, (blk, blk), 0)
    c_ = lax.broadcasted_iota(jnp.int32, (blk, blk), 1)
    logits = _dot_nt(qb, kb_ref[pl.ds(row0, blk), :HD_A]) + bias_ref[0]
    logits = jnp.where(c_ <= r, logits, NEG_INF)
    m0 = jnp.max(logits, axis=-1, keepdims=True)
    p = jnp.exp(logits - m0)
    m_ref[...] = m0
    l_ref[...] = jnp.sum(p, axis=-1, keepdims=True)
    acc_ref[...] = _dot(p.astype(BF16), vb_ref[pl.ds(row0, blk), :])

    grp = MOBA_GROUP

    def group(t, carry):
        j0 = grp * t
        rows = pl.ds(pl.multiple_of(j0 * blk, grp * blk), grp * blk)
        lg = _dot_nt(q_aug, kb_ref[rows, :])
        lg = lg + jnp.concatenate([bias_ref[jnp.maximum(i - j0 - g, 0)] for g in range(grp)], axis=1)
        m_old = m_ref[...]
        m_new = jnp.maximum(m_old, jnp.max(lg, axis=-1, keepdims=True))
        alpha = jnp.exp(m_old - m_new)
        pj = jnp.exp(lg - m_new)
        m_ref[...] = m_new
        l_ref[...] = alpha * l_ref[...] + jnp.sum(pj, axis=-1, keepdims=True)
        acc_ref[...] = alpha * acc_ref[...] + _dot(pj.astype(BF16), vb_ref[rows, :])
        return carry

    lax.fori_loop(0, (i + grp - 1) // grp, group, 0)
    o_ref[...] = (acc_ref[...] / l_ref[...]).astype(o_ref.dtype)


def moba_prompt(proj, rel_bias, n_seq, seq_len, q_col, k_col, v_col):
    blk = MOBA_BLOCK
    nb = seq_len // blk
    assert nb % MOBA_GROUP == 0
    n_dist = (nb + 1) * blk
    rv = _rel_bias_table(rel_bias, nb * blk - jnp.arange(n_dist)).reshape(H_A, 1, n_dist)
    return pl.pallas_call(
        functools.partial(_moba_prompt_kernel, n_blocks=nb),
        grid=(H_A, n_seq, nb),
        in_specs=[pl.BlockSpec((blk, HD_A), lambda h, n, i: (n * nb + i, q_col + h)),
                  pl.BlockSpec((seq_len, HD_A), lambda h, n, i: (n, k_col + h)),
                  pl.BlockSpec((seq_len, HD_A), lambda h, n, i: (n, v_col + h)),
                  pl.BlockSpec((1, 1, n_dist), lambda h, n, i: (h, 0, 0))],
        out_specs=pl.BlockSpec((blk, HD_A), lambda h, n, i: (n * nb + i, h)),
        out_shape=jax.ShapeDtypeStruct((n_seq * seq_len, H_A * HD_A), BF16),
        scratch_shapes=[pltpu.VMEM((nb, blk, blk), F32),
                        pltpu.VMEM((seq_len, 2 * HD_A), BF16),
                        pltpu.VMEM((seq_len, HD_A), BF16),
                        pltpu.VMEM((nb, HD_A), F32),
                        pltpu.VMEM((blk, 1), F32),
                        pltpu.VMEM((blk, 1), F32),
                        pltpu.VMEM((blk, HD_A), F32)],
        compiler_params=_cparams(("arbitrary", "arbitrary", "arbitrary")),
        name="moba_prompt",
    )(proj, proj, proj, rv)


PAGES_PER_STEP = 16
BLOCKS_PER_STEP = PAGES_PER_STEP * PAGE_SIZE // MOBA_BLOCK


def _moba_sample_keys_kernel(pt_ref, qbd_ref, knew_ref, bias_ref, bown_ref, *rest,
                             n_steps, n_tok):
    pps = PAGES_PER_STEP
    bps = BLOCKS_PER_STEP
    ppb = pps // bps
    kp = rest[:pps]
    p_ref, pown_ref, lg_ref, ksum_ref = rest[pps:]
    s = pl.program_id(1)
    qbd = qbd_ref[0]
    qbd_b = qbd.astype(BF16)

    def lg_block(b):
        return b // bps, slice((b % bps) * MOBA_BLOCK, (b % bps + 1) * MOBA_BLOCK)

    sums = []
    for k in range(pps):
        hsum = []
        heads = []
        for h in range(H_A):
            kh = kp[k][0, 0, pl.ds(h, PAGE_SIZE, stride=H_A), :]
            heads.append(kh.astype(BF16))
            hsum.append(jnp.sum(kh, axis=0, keepdims=True))
        page = jnp.concatenate(heads, axis=1)
        lg = _dot_nt(qbd_b, page) + bias_ref[:, k * PAGE_SIZE:(k + 1) * PAGE_SIZE]
        lg_ref[s, :, k * PAGE_SIZE:(k + 1) * PAGE_SIZE] = lg
        sums.append(jnp.concatenate(hsum, axis=1))
    blocks = [sum(sums[ppb * b:ppb * (b + 1)]) * (1.0 / MOBA_BLOCK) for b in range(bps)]
    ksum_ref[pl.ds(pl.multiple_of(s * bps, bps), bps), :] = jnp.concatenate(blocks, axis=0)

    @pl.when(s == n_steps - 1)
    def _():
        n_blk = n_steps * bps
        sc = _dot_nt(qbd, ksum_ref[...], precision=HIGHEST)
        bidx = lax.broadcasted_iota(jnp.int32, sc.shape, 1)
        sel = jnp.zeros(sc.shape, F32)
        for _ in range(MOBA_TOPK):
            mx = jnp.max(sc, axis=-1, keepdims=True)
            first = jnp.min(jnp.where(sc == mx, bidx, n_blk), axis=-1, keepdims=True)
            hit = bidx == first
            sel = jnp.where(hit, 1.0, sel)
            sc = jnp.where(hit, 2.0 * NEG_INF, sc)

        lo = _dot_nt(qbd, knew_ref[...]) + bown_ref[...]
        kc = lax.broadcasted_iota(jnp.int32, lo.shape, 1)
        qr = lax.broadcasted_iota(jnp.int32, lo.shape, 0) % n_tok
        lo = jnp.where(kc <= qr, lo, NEG_INF)

        def masked(b):
            st, sl = lg_block(b)
            return jnp.where(sel[:, b:b + 1] > 0.0, lg_ref[st, :, sl], NEG_INF)

        run = masked(0)
        for b in range(1, n_blk):
            run = jnp.maximum(run, masked(b))
        m = jnp.maximum(jnp.max(lo, axis=-1, keepdims=True), jnp.max(run, axis=-1, keepdims=True))
        e_own = jnp.exp(lo - m)
        tot = jnp.zeros(run.shape, F32)
        for b in range(n_blk):
            st, sl = lg_block(b)
            e = jnp.exp(masked(b) - m)
            lg_ref[st, :, sl] = e
            tot = tot + e
        l = jnp.sum(e_own, axis=-1, keepdims=True) + jnp.sum(tot, axis=-1, keepdims=True)
        inv = 1.0 / l
        pown_ref[0] = e_own * inv
        for b in range(n_blk):
            st, sl = lg_block(b)
            p_ref[0, :, b * MOBA_BLOCK:(b + 1) * MOBA_BLOCK] = (lg_ref[st, :, sl] * inv).astype(BF16)


def _moba_sample_values_kernel(pt_ref, p_ref, pown_ref, vnew_ref, *rest, n_steps, n_tok):
    pps = PAGES_PER_STEP
    vp = rest[:pps]
    o_ref, acc_ref = rest[pps:]
    s = pl.program_id(1)

    @pl.when(s == 0)
    def _():
        acc_ref[...] = _dot(pown_ref[0], vnew_ref[...])

    acc = acc_ref[...]
    for k in range(pps):
        page = jnp.concatenate(
            [vp[k][0, 0, pl.ds(h, PAGE_SIZE, stride=H_A), :].astype(BF16) for h in range(H_A)], axis=1)
        acc = acc + _dot(p_ref[0, :, k * PAGE_SIZE:(k + 1) * PAGE_SIZE], page)
    acc_ref[...] = acc

    @pl.when(s == n_steps - 1)
    def _():
        for h in range(H_A):
            o_ref[:, h * HD_A:(h + 1) * HD_A] = acc[h * n_tok:(h + 1) * n_tok,
                                                    h * HD_A:(h + 1) * HD_A].astype(o_ref.dtype)


def moba_sample(proj, row_blk0, k_colblk, v_colblk, qbd, cache_k, cache_v, layer, page_table,
                bias_past, bias_own, n_tok):
    n_seq, n_pages = page_table.shape
    pps = PAGES_PER_STEP
    n_steps = n_pages // pps
    n_past = n_pages * PAGE_SIZE
    wa = H_A * HD_A
    nrow = H_A * n_tok
    page_block = (1, 1, PAGE_SIZE * H_A, HD_A)
    cache_k = cache_k.reshape(cache_k.shape[:2] + (PAGE_SIZE * H_A, HD_A))
    cache_v = cache_v.reshape(cache_v.shape[:2] + (PAGE_SIZE * H_A, HD_A))

    def pmap(k):
        return lambda b, s, pt: (layer, pt[b, s * pps + k], 0, 0)

    p, p_own = pl.pallas_call(
        functools.partial(_moba_sample_keys_kernel, n_steps=n_steps, n_tok=n_tok),
        grid_spec=pltpu.PrefetchScalarGridSpec(
            num_scalar_prefetch=1,
            grid=(n_seq, n_steps),
            in_specs=[pl.BlockSpec((1, nrow, wa), lambda b, s, pt: (b, 0, 0)),
                      pl.BlockSpec((n_tok, wa), lambda b, s, pt: (row_blk0 + b, k_colblk)),
                      pl.BlockSpec((nrow, pps * PAGE_SIZE), lambda b, s, pt: (0, s)),
                      pl.BlockSpec((nrow, n_tok), lambda b, s, pt: (0, 0))]
            + [pl.BlockSpec(page_block, pmap(k)) for k in range(pps)],
            out_specs=[pl.BlockSpec((1, nrow, n_past), lambda b, s, pt: (b, 0, 0)),
                       pl.BlockSpec((1, nrow, n_tok), lambda b, s, pt: (b, 0, 0))],
            scratch_shapes=[pltpu.VMEM((n_steps, nrow, pps * PAGE_SIZE), F32),
                            pltpu.VMEM((n_past // MOBA_BLOCK, wa), F32)]),
        out_shape=[jax.ShapeDtypeStruct((n_seq, nrow, n_past), BF16),
                   jax.ShapeDtypeStruct((n_seq, nrow, n_tok), F32)],
        compiler_params=_cparams(("arbitrary", "arbitrary")),
        name="moba_sample_keys",
    )(page_table, qbd, proj, bias_past, bias_own, *([cache_k] * pps))

    return pl.pallas_call(
        functools.partial(_moba_sample_values_kernel, n_steps=n_steps, n_tok=n_tok),
        grid_spec=pltpu.PrefetchScalarGridSpec(
            num_scalar_prefetch=1,
            grid=(n_seq, n_steps),
            in_specs=[pl.BlockSpec((1, nrow, pps * PAGE_SIZE), lambda b, s, pt: (b, 0, s)),
                      pl.BlockSpec((1, nrow, n_tok), lambda b, s, pt: (b, 0, 0)),
                      pl.BlockSpec((n_tok, wa), lambda b, s, pt: (row_blk0 + b, v_colblk))]
            + [pl.BlockSpec(page_block, pmap(k)) for k in range(pps)],
            out_specs=pl.BlockSpec((n_tok, wa), lambda b, s, pt: (b, 0)),
            scratch_shapes=[pltpu.VMEM((nrow, wa), F32)]),
        out_shape=jax.ShapeDtypeStruct((n_seq * n_tok, wa), BF16),
        compiler_params=_cparams(("arbitrary", "arbitrary")),
        name="moba_sample_values",
    )(page_table, p, p_own, proj, *([cache_v] * pps))


def _gdn_prep_kernel(x_ref, halo_ref, ab_ref, cw_ref, alog_ref, dtb_ref,
                     q_ref, k_ref, v_ref, gb_ref, *, tt):
    x = x_ref[...]
    xf = jnp.concatenate([halo_ref[0], x], axis=0)
    cw = cw_ref[...]
    conv = xf[5:5 + tt] * cw[0:1]
    for w in range(1, CONV_WIDTH):
        conv = conv + xf[5 + w:5 + w + tt] * cw[w:w + 1]
    act = conv * jax.nn.sigmoid(conv)
    wk = H_B * DK_B
    for h in range(H_B):
        q = act[:, h * DK_B:(h + 1) * DK_B]
        k = act[:, wk + h * DK_B:wk + (h + 1) * DK_B]
        q_ref[0, h] = q * lax.rsqrt(jnp.sum(q * q, axis=-1, keepdims=True) + RMS_EPS) * (DK_B ** -0.5)
        k_ref[0, h] = k * lax.rsqrt(jnp.sum(k * k, axis=-1, keepdims=True) + RMS_EPS)
        v_ref[0, h] = act[:, 2 * wk + h * DK_B:2 * wk + (h + 1) * DK_B]
    ab = ab_ref[...]
    t = ab + dtb_ref[...]
    sp = jnp.maximum(t, 0.0) + jnp.log(1.0 + jnp.exp(-jnp.abs(t)))
    g = -jnp.exp(alog_ref[...]) * sp
    lane = lax.broadcasted_iota(jnp.int32, ab.shape, 1)
    gb_ref[0] = jnp.where(lane < H_B, g, jax.nn.sigmoid(ab))


def gdn_prep(proj, ab, halo, conv_w, a_log, dt_bias, row0, n_seq, seq_len, tt, qkv_colblk):
    w3 = conv_w.shape[1]
    nt = seq_len // tt
    rb0 = row0 // tt
    pad = jnp.zeros((1, 128 - H_B), F32)
    alog = jnp.concatenate([a_log.reshape(1, H_B), pad], axis=1)
    dtb = jnp.concatenate([dt_bias.reshape(1, H_B), pad], axis=1)
    hm = jax.ShapeDtypeStruct((n_seq, H_B, seq_len, DK_B), F32)
    hspec = pl.BlockSpec((1, H_B, tt, DK_B), lambda n, t: (n, 0, t, 0))
    return pl.pallas_call(
        functools.partial(_gdn_prep_kernel, tt=tt),
        grid=(n_seq, nt),
        in_specs=[pl.BlockSpec((tt, w3), lambda n, t: (rb0 + n * nt + t, qkv_colblk)),
                  pl.BlockSpec((1, 8, w3), lambda n, t: (n * nt + t, 0, 0)),
                  pl.BlockSpec((tt, 128), lambda n, t: (rb0 + n * nt + t, 0)),
                  pl.BlockSpec((CONV_WIDTH, w3), lambda n, t: (0, 0)),
                  pl.BlockSpec((1, 128), lambda n, t: (0, 0)),
                  pl.BlockSpec((1, 128), lambda n, t: (0, 0))],
        out_specs=[hspec, hspec, hspec,
                   pl.BlockSpec((1, tt, 128), lambda n, t: (n, t, 0))],
        out_shape=[hm, hm, hm, jax.ShapeDtypeStruct((n_seq, seq_len, 128), F32)],
        compiler_params=_cparams(("parallel", "parallel")),
        name="gdn_prep",
    )(proj, halo, ab, conv_w, alog, dtb)


def _split2(a):
    hi = a.astype(BF16)
    return hi, (a - hi.astype(F32)).astype(BF16)


def _split3(a):
    hi = a.astype(BF16)
    r1 = a - hi.astype(F32)
    mid = r1.astype(BF16)
    return hi, mid, (r1 - mid.astype(F32)).astype(BF16)


def _dot3(a, b):
    ah, al = _split2(a)
    bh, bl = _split2(b)
    return _dot(ah, bh) + (_dot(ah, bl) + _dot(al, bh))


def _dot_exact_lhs(a_bf16, b):
    b1, b2, b3 = _split3(b)
    return _dot(a_bf16, b1) + (_dot(a_bf16, b2) + _dot(a_bf16, b3))


def _unit_lower_inverse(a_list, c):
    r = lax.broadcasted_iota(jnp.int32, (c, c), 0)
    cc = lax.broadcasted_iota(jnp.int32, (c, c), 1)
    eye = jnp.where(r == cc, 1.0, 0.0)
    base = min(c, 8)
    n1 = [jnp.where((r // base) == (cc // base), -a, 0.0) for a in a_list]
    n2 = [_dot3(x, x) for x in n1]
    n4 = [_dot3(x, x) for x in n2]
    t = [_dot3(eye + x, eye + y) for x, y in zip(n1, n2)]
    t = [_dot3(x, eye + y) for x, y in zip(t, n4)]
    b = base
    while b < c:
        inner = ((r // (2 * b)) == (cc // (2 * b))) & ((r // b) != (cc // b))
        left = [_dot3(x, jnp.where(inner, a, 0.0)) for x, a in zip(t, a_list)]
        t = [x - _dot3(y, x) for x, y in zip(t, left)]
        b *= 2
    return t


def _gdn_chunk_kernel(*refs, nb, c):
    q_ref, k_ref, v_ref, gb_ref = refs[:4]
    z_refs = refs[4:4 + nb]
    ng_ref, s0_ref, o_ref, sout_ref, s_ref = refs[4 + nb:]
    ci = pl.program_id(1)

    @pl.when(ci == 0)
    def _():
        s_ref[...] = s0_ref[...]

    r = lax.broadcasted_iota(jnp.int32, (c, c), 0)
    cc = lax.broadcasted_iota(jnp.int32, (c, c), 1)
    incl = r >= cc
    strict = r > cc
    ltri = jnp.where(incl, 1.0, 0.0).astype(BF16)
    eye = jnp.where(r == cc, 1.0, 0.0)
    ones = jnp.ones((c, c), BF16)
    ng = ng_ref[...]
    ch = [(n, h) for n in range(nb) for h in range(H_B)]
    gbv = [gb_ref[n] for n in range(nb)]
    q = [q_ref[n, h] for n, h in ch]
    k = [k_ref[n, h] for n, h in ch]
    v = [v_ref[n, h] for n, h in ch]
    gcol = [jnp.broadcast_to(gbv[n][:, h:h + 1], (c, DK_B)) for n, h in ch]
    bcol = [jnp.broadcast_to(gbv[n][:, H_B + h:H_B + h + 1], (c, DK_B)) for n, h in ch]
    big_g = [_dot_exact_lhs(ltri, g) for g in gcol]
    gj = [_dot_exact_lhs(ones, g[:, :c] * eye) for g in big_g]
    decay = [jnp.where(incl, jnp.exp(jnp.minimum(g[:, :c] - x, 0.0)), 0.0) for g, x in zip(big_g, gj)]
    kk = [_dot_nt(x, x) for x in k]
    a = [jnp.where(strict, b[:, :c] * x * d, 0.0) for b, x, d in zip(bcol, kk, decay)]
    t = _unit_lower_inverse(a, c)
    gam = [jnp.exp(g) for g in big_g]
    u = [_dot3(x, b * y) for x, b, y in zip(t, bcol, v)]
    w = [_dot3(x, b * g * y) for x, b, g, y in zip(t, bcol, gam, k)]
    qk = [_dot_nt(x, y) * d for x, y, d in zip(q, k, decay)]
    g_last = [g[c - 1:c, :] for g in big_g]
    k_tail = [jnp.exp(gl - g) * y for gl, g, y in zip(g_last, big_g, k)]
    s = [s_ref[n, h] for n, h in ch]
    uu = [x - _dot(y, z_) for x, y, z_ in zip(u, w, s)]
    o = [_dot(g * x, z_) + _dot(y, x2) for g, x, z_, y, x2 in zip(gam, q, s, qk, uu)]
    s_new = [jnp.exp(gl) * z_ + _dot_tn(y, x) for gl, z_, y, x in zip(g_last, s, k_tail, uu)]
    for (n, h), x in zip(ch, s_new):
        s_ref[n, h] = x
    for (n, h), x in zip(ch, o):
        x = x * lax.rsqrt(jnp.mean(x * x, axis=-1, keepdims=True) + RMS_EPS) * ng
        zh = z_refs[n][:, h * DK_B:(h + 1) * DK_B]
        o_ref[n, :, h * DK_B:(h + 1) * DK_B] = (x * (zh * jax.nn.sigmoid(zh))).astype(o_ref.dtype)

    @pl.when(ci == pl.num_programs(1) - 1)
    def _():
        sout_ref[...] = s_ref[...]


def gdn_chunks(q, k, v, gb, proj, z_colblk, row0, norm_g, s0, nb, c):
    n_seq, _, seq_len, _ = q.shape
    nc = seq_len // c
    wv = H_B * DK_B
    rb0 = row0 // c
    hspec = pl.BlockSpec((nb, H_B, c, DK_B), lambda n, ci: (n, 0, ci, 0))
    sspec = pl.BlockSpec((nb, H_B, DK_B, DK_B), lambda n, ci: (n, 0, 0, 0))

    def zmap(j):
        return lambda n, ci: (rb0 + (n * nb + j) * nc + ci, z_colblk)

    return pl.pallas_call(
        functools.partial(_gdn_chunk_kernel, nb=nb, c=c),
        grid=(n_seq // nb, nc),
        in_specs=[hspec, hspec, hspec,
                  pl.BlockSpec((nb, c, 128), lambda n, ci: (n, ci, 0))]
        + [pl.BlockSpec((c, wv), zmap(j)) for j in range(nb)]
        + [pl.BlockSpec((1, DK_B), lambda n, ci: (0, 0)), sspec],
        out_specs=[pl.BlockSpec((nb, c, wv), lambda n, ci: (n, ci, 0)), sspec],
        out_shape=[jax.ShapeDtypeStruct((n_seq, seq_len, wv), BF16),
                   jax.ShapeDtypeStruct((n_seq, H_B, DK_B, DK_B), F32)],
        scratch_shapes=[pltpu.VMEM((nb, H_B, DK_B, DK_B), F32)],
        compiler_params=_cparams(("parallel", "arbitrary")),
        name="gdn_chunks",
    )(q, k, v, gb, *([proj] * nb), norm_g.reshape(1, DK_B), s0)


def _merge_kernel(oa_ref, ob_ref, wa_ref, wb_ref, ga_ref, gb_ref, o_ref):
    ya = _dot(oa_ref[...], wa_ref[...])
    yb = _dot(ob_ref[...], wb_ref[...])
    o_ref[...] = (jax.nn.sigmoid(ga_ref[...]) * ya
                  + jax.nn.sigmoid(gb_ref[...]) * yb).astype(o_ref.dtype)


def branch_merge(out_a, out_b, wa, wb, proj, ga_col0, gb_col0, tm, tn):
    m, ka = out_a.shape
    n = wa.shape[1]
    ga_blk, gb_blk = ga_col0 // tn, gb_col0 // tn
    assert ga_blk * tn == ga_col0 and gb_blk * tn == gb_col0
    return pl.pallas_call(
        _merge_kernel,
        grid=(m // tm, n // tn),
        in_specs=[pl.BlockSpec((tm, ka), lambda i, j: (i, 0)),
                  pl.BlockSpec((tm, ka), lambda i, j: (i, 0)),
                  pl.BlockSpec((ka, tn), lambda i, j: (0, j)),
                  pl.BlockSpec((ka, tn), lambda i, j: (0, j)),
                  pl.BlockSpec((tm, tn), lambda i, j: (i, ga_blk + j)),
                  pl.BlockSpec((tm, tn), lambda i, j: (i, gb_blk + j))],
        out_specs=pl.BlockSpec((tm, tn), lambda i, j: (i, j)),
        out_shape=jax.ShapeDtypeStruct((m, n), BF16),
        compiler_params=_cparams(("parallel", "parallel")),
        name="branch_merge",
    )(out_a, out_b, wa, wb, proj, proj)


def _matmul_residual_kernel(x_ref, w_ref, r_ref, o_ref):
    o_ref[...] = r_ref[...] + _dot(x_ref[...], w_ref[...])


def matmul_residual(x, w, res, tm, tn):
    m, k = x.shape
    n = w.shape[1]
    return pl.pallas_call(
        _matmul_residual_kernel,
        grid=(m // tm, n // tn),
        in_specs=[pl.BlockSpec((tm, k), lambda i, j: (i, 0)),
                  pl.BlockSpec((k, tn), lambda i, j: (0, j)),
                  pl.BlockSpec((tm, tn), lambda i, j: (i, j))],
        out_specs=pl.BlockSpec((tm, tn), lambda i, j: (i, j)),
        out_shape=jax.ShapeDtypeStruct((m, n), F32),
        compiler_params=_cparams(("parallel", "parallel")),
        name="out_proj",
    )(x, w, res)


MOE_TA = 256


def _lane_rotate(x, shift):
    parts = [pltpu.roll(x[:, c:c + 128], shift, 1) for c in range(0, x.shape[1], 128)]
    return parts[0] if len(parts) == 1 else jnp.concatenate(parts, axis=1)


MOE_RUN = 4


def _block_out_copy(buf_ref, out_ref, sem_ref, j, first_block, col_tile):
    rows, cols = buf_ref.shape[1:]
    dst = out_ref.at[pl.ds(pl.multiple_of((first_block + j) * rows, rows), rows),
                     pl.ds(pl.multiple_of(col_tile * cols, cols), cols)]
    return pltpu.make_async_copy(buf_ref.at[j], dst, sem_ref.at[j])


def _zero_fill(buf_ref, out_ref, sem_ref, n_fill, first_block, col_tile):
    for j in range(buf_ref.shape[0]):
        @pl.when(j < n_fill)
        def _(j=j):
            buf_ref[j] = jnp.zeros(buf_ref.shape[1:], buf_ref.dtype)
            _block_out_copy(buf_ref, out_ref, sem_ref, j, first_block, col_tile).start()
    for j in range(buf_ref.shape[0]):
        @pl.when(j < n_fill)
        def _(j=j):
            _block_out_copy(buf_ref, out_ref, sem_ref, j, first_block, col_tile).wait()


def _moe_up_kernel(re_ref, rb_ref, rl_ref, nr_ref, *refs):
    r = MOE_RUN
    x_refs = refs[:r]
    wa_ref, wb_ref, ba_ref, bb_ref, out_ref, wab_ref, wbb_ref, buf_ref, sem_ref = refs[r:]
    v = pl.program_id(0)
    n = pl.program_id(1)

    @pl.when(v >= nr_ref[0])
    def _():
        _zero_fill(buf_ref, out_ref, sem_ref, rl_ref[v], rb_ref[v], n)

    @pl.when(v < nr_ref[0])
    def _():
        wab_ref[...] = wa_ref[0].astype(BF16)
        wbb_ref[...] = wb_ref[0].astype(BF16)
        n_live = rl_ref[v]
        for j in range(r):
            @pl.when(j < n_live)
            def _(j=j):
                half = x_refs[j].shape[1]
                x_hi, x_lo = _unpack_bf16_pair(x_refs[j][...])
                ga = (_dot(x_hi, wab_ref[:half, :]) + _dot(x_lo, wab_ref[half:, :])
                      + ba_ref[0])
                gb = _dot(x_hi, wbb_ref[:half, :]) + _dot(x_lo, wbb_ref[half:, :]) + bb_ref[0]
                even = (lax.broadcasted_iota(jnp.int32, ga.shape, 1) % 2) == 0
                gate = jnp.where(even, ga, _lane_rotate(gb, 1))
                up = jnp.where(even, _lane_rotate(ga, 127), gb)
                gl = jnp.minimum(gate, SWIGLU_LIMIT)
                up = jnp.clip(up, -SWIGLU_LIMIT, SWIGLU_LIMIT)
                buf_ref[j] = (gl * jax.nn.sigmoid(SWIGLU_ALPHA * gl) * (up + 1.0)).astype(BF16)
                _block_out_copy(buf_ref, out_ref, sem_ref, j, rb_ref[v], n).start()
        for j in range(r):
            @pl.when(j < n_live)
            def _(j=j):
                _block_out_copy(buf_ref, out_ref, sem_ref, j, rb_ref[v], n).wait()


def _bf16_bits(w):
    return lax.bitcast_convert_type(w.astype(BF16).astype(F32), jnp.uint32)


def _moe_down_kernel(re_ref, rb_ref, rl_ref, nr_ref, *refs):
    r = MOE_RUN
    a_refs = refs[:r]
    wd_ref, bd_ref, out_ref, wdb_ref, buf_ref, sem_ref = refs[r:]
    v = pl.program_id(0)
    n = pl.program_id(1)

    @pl.when(v >= nr_ref[0])
    def _():
        _zero_fill(buf_ref, out_ref, sem_ref, rl_ref[v], rb_ref[v], n)

    @pl.when(v < nr_ref[0])
    def _():
        f = wd_ref.shape[1]
        for t in range(f // (2 * MOE_TA)):
            wa = wd_ref[0, t * MOE_TA:(t + 1) * MOE_TA, :]
            wb = wd_ref[0, f // 2 + t * MOE_TA:f // 2 + (t + 1) * MOE_TA, :]
            pair = (_bf16_bits(wa) >> 16) | _bf16_bits(wb)
            wdb_ref[t * 2 * MOE_TA:(t + 1) * 2 * MOE_TA, :] = pltpu.bitcast(pair, BF16)
        n_live = rl_ref[v]
        for j in range(r):
            @pl.when(j < n_live)
            def _(j=j):
                buf_ref[j] = _dot(a_refs[j][...], wdb_ref[...]) + bd_ref[0]
                _block_out_copy(buf_ref, out_ref, sem_ref, j, rb_ref[v], n).start()
        for j in range(r):
            @pl.when(j < n_live)
            def _(j=j):
                _block_out_copy(buf_ref, out_ref, sem_ref, j, rb_ref[v], n).wait()


def moe_experts(x_rows, runs, w_gate_up, b_gate_up, w_down, b_down, tn_down):
    rows = x_rows.shape[0]
    n_exp, d, f2 = w_gate_up.shape
    f = f2 // 2
    dm = w_down.shape[2]
    n_steps = runs[0].shape[0]
    r = MOE_RUN

    def rowmap(j):
        return lambda v, n, re, rb, rl, nr: (rb[v] + jnp.minimum(j, rl[v] - 1), 0)

    def wmap(off, n_tiles):
        return lambda v, n, re, rb, rl, nr: (re[v], 0, off + jnp.where(v < nr[0], n, n_tiles - 1))

    nt = f // (2 * MOE_TA)
    tw = 2 * MOE_TA
    bgu = b_gate_up.reshape(n_exp, 1, f2)
    act = pl.pallas_call(
        _moe_up_kernel,
        grid_spec=pltpu.PrefetchScalarGridSpec(
            num_scalar_prefetch=4,
            grid=(n_steps, nt),
            in_specs=[pl.BlockSpec((MOE_ROWS, d // 2), rowmap(j)) for j in range(r)]
            + [pl.BlockSpec((1, d, tw), wmap(0, nt)),
               pl.BlockSpec((1, d, tw), wmap(nt, nt)),
               pl.BlockSpec((1, 1, tw), wmap(0, nt)),
               pl.BlockSpec((1, 1, tw), wmap(nt, nt))],
            out_specs=pl.BlockSpec(memory_space=pl.ANY),
            scratch_shapes=[pltpu.VMEM((d, tw), BF16), pltpu.VMEM((d, tw), BF16),
                            pltpu.VMEM((r, MOE_ROWS, tw), BF16), pltpu.SemaphoreType.DMA((r,))]),
        out_shape=jax.ShapeDtypeStruct((rows, f), BF16),
        compiler_params=_cparams(("arbitrary", "arbitrary")),
        name="moe_up",
    )(*runs, *([x_rows] * r), w_gate_up, w_gate_up, bgu, bgu)
    nt_d = dm // tn_down
    return pl.pallas_call(
        _moe_down_kernel,
        grid_spec=pltpu.PrefetchScalarGridSpec(
            num_scalar_prefetch=4,
            grid=(n_steps, nt_d),
            in_specs=[pl.BlockSpec((MOE_ROWS, f), rowmap(j)) for j in range(r)]
            + [pl.BlockSpec((1, f, tn_down), wmap(0, nt_d)),
               pl.BlockSpec((1, 1, tn_down), wmap(0, nt_d))],
            out_specs=pl.BlockSpec(memory_space=pl.ANY),
            scratch_shapes=[pltpu.VMEM((f, tn_down), BF16),
                            pltpu.VMEM((r, MOE_ROWS, tn_down), F32), pltpu.SemaphoreType.DMA((r,))]),
        out_shape=jax.ShapeDtypeStruct((rows, dm), F32),
        compiler_params=_cparams(("arbitrary", "arbitrary")),
        name="moe_down",
    )(*runs, *([act] * r), w_down, b_down.reshape(n_exp, 1, dm))


def _final_kernel(y_ref, *rest):
    e_refs = rest[:TOP_K]
    gate_ref, g_ref, o_ref = rest[TOP_K:]
    x = y_ref[...]
    for kk in range(TOP_K):
        x = x + e_refs[kk][...] * gate_ref[:, kk:kk + 1]
    o_ref[...] = x * lax.rsqrt(jnp.mean(x * x, axis=-1, keepdims=True) + RMS_EPS) * g_ref[...]


def combine_residual_norm(y, expert_out, gate, g, tm):
    m, d = y.shape
    nt = m // tm
    return pl.pallas_call(
        _final_kernel,
        grid=(nt,),
        in_specs=[pl.BlockSpec((tm, d), lambda i: (i, 0))]
        + [pl.BlockSpec((tm, d), functools.partial(lambda i, kk: (kk * nt + i, 0), kk=kk))
           for kk in range(TOP_K)]
        + [pl.BlockSpec((tm, TOP_K), lambda i: (i, 0)),
           pl.BlockSpec((1, d), lambda i: (0, 0))],
        out_specs=pl.BlockSpec((tm, d), lambda i: (i, 0)),
        out_shape=jax.ShapeDtypeStruct((m, d), F32),
        compiler_params=_cparams(("parallel",)),
        name="combine_norm",
    )(y, *([expert_out] * TOP_K), gate, g.reshape(1, d))


def _rel_bias_table(rel_bias, dist):
    n = jnp.maximum(dist, 0)
    max_exact = N_BUCKETS // 2
    nf = jnp.maximum(n, max_exact).astype(F32)
    large = max_exact + (jnp.log(nf / max_exact) / math.log(MAX_DISTANCE / max_exact)
                         * (N_BUCKETS - max_exact)).astype(jnp.int32)
    bucket = jnp.where(n < max_exact, n, jnp.minimum(large, N_BUCKETS - 1))
    return rel_bias[:, bucket].astype(F32)


def _route(logits, n_tok):
    top_v, top_e = lax.top_k(logits, TOP_K)
    gate = jax.nn.softmax(top_v, axis=-1)
    n_exp = logits.shape[1]
    n_assign = n_tok * TOP_K
    flat_e = top_e.reshape(-1)
    order = jnp.argsort(flat_e, stable=True)
    e_sorted = flat_e[order]
    tok_sorted = (order // TOP_K).astype(jnp.int32)
    counts = jnp.zeros((n_exp,), jnp.int32).at[flat_e].add(1)
    padded = (counts + MOE_ROWS - 1) // MOE_ROWS * MOE_ROWS
    pad_end = jnp.cumsum(padded)
    pad_start = pad_end - padded
    start = jnp.cumsum(counts) - counts
    dest = (pad_start[e_sorted] + jnp.arange(n_assign) - start[e_sorted]).astype(jnp.int32)
    n_blocks = -(-n_assign // MOE_ROWS) + n_exp
    row_tok = jnp.zeros((n_blocks * MOE_ROWS,), jnp.int32).at[dest].set(tok_sorted)
    pos = jnp.zeros((n_assign,), jnp.int32).at[order].set(dest).reshape(n_tok, TOP_K)
    nblk_e = padded // MOE_ROWS
    nrun_e = (nblk_e + MOE_RUN - 1) // MOE_RUN
    run_end = jnp.cumsum(nrun_e)
    run_start = run_end - nrun_e
    n_runs = run_end[-1]
    n_steps = n_exp + -(-n_blocks // MOE_RUN)
    steps = jnp.arange(n_steps)
    step = jnp.minimum(steps, n_runs - 1)
    run_e = jnp.sum(run_end[None, :] <= step[:, None], axis=1)
    piece = step - run_start[run_e]
    run_b0 = pad_start[run_e] // MOE_ROWS + piece * MOE_RUN
    run_len = jnp.minimum(MOE_RUN, nblk_e[run_e] - piece * MOE_RUN)
    tail_b0 = jnp.minimum(pad_end[-1] // MOE_ROWS + (steps - n_runs) * MOE_RUN, n_blocks)
    tail_len = jnp.clip(n_blocks - tail_b0, 0, MOE_RUN)
    real = steps < n_runs
    run_b0 = jnp.where(real, run_b0, tail_b0)
    run_len = jnp.where(real, run_len, tail_len)
    runs = (run_e.astype(jnp.int32), run_b0.astype(jnp.int32), run_len.astype(jnp.int32),
            n_runs.astype(jnp.int32).reshape(1))
    return gate, row_tok, runs, pos


def kernel(x_prompt, x_sample, cache_k, cache_v, state_gdn, state_conv, page_table, norm_mix_g, w_in, rel_bias, conv_w, a_log, dt_bias, gdn_norm_g, w_branch_a, w_branch_b, w_out, norm_ffn_g, router_w, router_b, w_gate_up, b_gate_up, w_down, b_down, norm_final_g):
    n_p, seq, d = x_prompt.shape
    n_s, dseq, _ = x_sample.shape
    assert w_in.shape[0] == 1
    l = 0
    wa = H_A * HD_A
    wk = H_B * DK_B
    qkv_b = 3 * wk
    n_prow = n_p * seq
    n_srow = n_s * dseq
    n_tok = n_prow + n_srow
    past = page_table.shape[1] * PAGE_SIZE
    assert past % MOBA_BLOCK == 0 and seq % MOBA_BLOCK == 0

    x = jnp.concatenate([x_prompt.reshape(n_prow, d), x_sample.reshape(n_srow, d)], axis=0)

    w = w_in[l]
    c_z = 3 * wa + qkv_b
    c_ab = c_z + wk
    c_g = c_ab + 2 * H_B
    w_main = jnp.concatenate([w[:, :c_ab], w[:, c_g:]], axis=1).astype(BF16)
    w_ab = jnp.concatenate([w[:, c_ab:c_g], jnp.zeros((d, 128 - 2 * H_B), F32)], axis=1).astype(BF16)
    tm = 768
    proj = norm_matmul(x, norm_mix_g[l], w_main, tm, 1024)
    ab = norm_matmul(x, norm_mix_g[l], w_ab, tm, 128)

    k_all = proj[:, wa:2 * wa]
    v_all = proj[:, 2 * wa:3 * wa]

    out_a_p = moba_prompt(proj, rel_bias, n_p, seq, 0, H_A, 2 * H_A)

    q_s = proj[n_prow:, :wa].reshape(n_s, dseq, H_A, HD_A) * (HD_A ** -0.5)
    qbd = jnp.einsum('blhd,hg->bhlgd', q_s, jnp.eye(H_A, dtype=F32)).reshape(n_s, H_A * dseq, wa)
    n_dist = past + dseq
    far_first = _rel_bias_table(rel_bias, n_dist - 1 - jnp.arange(n_dist))
    bias_past = jnp.stack([far_first[:, dseq - 1 - t:dseq - 1 - t + past] for t in range(dseq)], axis=1)
    bias_past = bias_past.reshape(H_A * dseq, past)
    bias_own = _rel_bias_table(rel_bias, jnp.arange(dseq)[:, None] - jnp.arange(dseq)[None, :])
    bias_own = bias_own.reshape(H_A * dseq, dseq)
    out_a_s = moba_sample(proj, n_prow // dseq, 1, 2, qbd, cache_k, cache_v, l, page_table,
                          bias_past, bias_own, dseq)
    out_a = jnp.concatenate([out_a_p, out_a_s], axis=0)

    qkv_colblk = 3 * wa // qkv_b
    assert qkv_colblk * qkv_b == 3 * wa
    tt = 512
    raw_p = proj[:n_prow, 3 * wa:3 * wa + qkv_b].reshape(n_p, seq // tt, tt, qkv_b)
    halo0 = jnp.zeros((n_p, 1, 8, qkv_b), F32)
    halo_p = jnp.concatenate([halo0, raw_p[:, :-1, tt - 8:, :]], axis=1).reshape(n_p * (seq // tt), 8, qkv_b)
    qp, kp, vp, gbp = gdn_prep(proj, ab, halo_p, conv_w[l], a_log[l], dt_bias[l],
                               0, n_p, seq, tt, qkv_colblk)
    raw_s = proj[n_prow:, 3 * wa:3 * wa + qkv_b].reshape(n_s, dseq, qkv_b)
    halo_s = jnp.concatenate([jnp.zeros((n_s, 8 - (CONV_WIDTH - 1), qkv_b), F32), state_conv[l]], axis=1)
    qs_, ks_, vs_, gbs = gdn_prep(proj, ab, halo_s, conv_w[l], a_log[l], dt_bias[l],
                                  n_prow, n_s, dseq, dseq, qkv_colblk)
    z_colblk = c_z // wk
    assert z_colblk * wk == c_z
    c_p = math.gcd(seq, GDN_CHUNK)
    out_b_p, s_p = gdn_chunks(qp, kp, vp, gbp, proj, z_colblk, 0, gdn_norm_g[l],
                              jnp.zeros((n_p, H_B, DK_B, DK_B), F32), n_p, c_p)
    c_s = math.gcd(dseq, GDN_CHUNK)
    out_b_s, s_s = gdn_chunks(qs_, ks_, vs_, gbs, proj, z_colblk, n_prow, gdn_norm_g[l],
                              state_gdn[l], 2, c_s)
    out_b = jnp.concatenate([out_b_p.reshape(n_prow, wk), out_b_s.reshape(n_srow, wk)], axis=0)
    conv_p = raw_p.reshape(n_p, seq, qkv_b)[:, seq - (CONV_WIDTH - 1):, :]
    conv_s = jnp.concatenate([state_conv[l], raw_s], axis=1)[:, dseq:, :]

    merged = branch_merge(out_a, out_b, w_branch_a[l].astype(BF16), w_branch_b[l].astype(BF16),
                          proj, c_ab, c_ab + d, tm, 1024)
    y1 = matmul_residual(merged, w_out[l].astype(BF16), x, tm, 1024)

    n_exp = router_w.shape[2]
    w_r = jnp.concatenate([router_w[l], jnp.zeros((d, 128 - n_exp), F32)], axis=1).astype(BF16)
    r_logits, h2 = norm_matmul(y1, norm_ffn_g[l], w_r, tm, 128, emit_h=True)
    logits = r_logits[:, :n_exp] + router_b[l].astype(F32)
    gate, row_tok, runs, pos = _route(logits, n_tok)
    x_rows = h2[row_tok]
    y_rows = moe_experts(x_rows, runs, w_gate_up.reshape(w_gate_up.shape[1:]),
                         b_gate_up.reshape(b_gate_up.shape[1:]), w_down.reshape(w_down.shape[1:]),
                         b_down.reshape(b_down.shape[1:]), 512)
    picked = y_rows[pos.T.reshape(-1)]
    y = combine_residual_norm(y1, picked, gate, norm_final_g, 384)

    y_prompt = y[:n_prow].reshape(n_p, seq, d)
    y_sample = y[n_prow:].reshape(n_s, dseq, d)
    k_p = k_all[:n_prow].reshape(1, n_p, seq, H_A, HD_A)
    v_p = v_all[:n_prow].reshape(1, n_p, seq, H_A, HD_A)
    k_s = k_all[n_prow:].reshape(1, n_s, dseq, H_A, HD_A)
    v_s = v_all[n_prow:].reshape(1, n_s, dseq, H_A, HD_A)
    return (y_prompt, y_sample, k_p, v_p, s_p[None], conv_p[None],
            k_s, v_s, s_s[None], conv_s[None])
```

```python
import functools
import math

import jax
import jax.numpy as jnp
from jax import lax
from jax.experimental import pallas as pl
from jax.experimental.pallas import tpu as pltpu

F32 = jnp.float32
BF16 = jnp.bfloat16
HIGHEST = lax.Precision.HIGHEST

RMS_EPS = 1e-6
NEG_INF = -1e30

H_A = 8
HD_A = 128
MOBA_BLOCK = 256
MOBA_TOPK = 3
MOBA_GROUP = 4
PAGE_SIZE = 128
N_BUCKETS = 32
MAX_DISTANCE = 4096

H_B = 8
DK_B = 128
CONV_WIDTH = 4
GDN_CHUNK = 64

TOP_K = 4
SWIGLU_LIMIT = 7.0
SWIGLU_ALPHA = 1.702
MOE_ROWS = 512

VMEM_LIMIT = 56 * 1024 * 1024


def _cparams(sem):
    return pltpu.CompilerParams(dimension_semantics=sem, vmem_limit_bytes=VMEM_LIMIT)


def _dot(a, b, precision=None):
    return jnp.dot(a, b, preferred_element_type=F32, precision=precision)


def _dot_nt(a, b, precision=None):
    return lax.dot_general(a, b, (((1,), (1,)), ((), ())),
                           preferred_element_type=F32, precision=precision)


def _dot_tn(a, b, precision=None):
    return lax.dot_general(a, b, (((0,), (0,)), ((), ())),
                           preferred_element_type=F32, precision=precision)


def _norm_matmul_kernel(x_ref, g_ref, w_ref, o_ref, *rest, emit_h):
    if emit_h:
        h_out_ref, h_ref = rest
    else:
        (h_ref,) = rest

    @pl.when(pl.program_id(1) == 0)
    def _():
        x = x_ref[...]
        y = x * lax.rsqrt(jnp.mean(x * x, axis=-1, keepdims=True) + RMS_EPS)
        h_ref[...] = (y * g_ref[...]).astype(BF16)

    if emit_h:
        half = h_ref.shape[1] // 2
        h_out_ref[...] = _pack_bf16_pair(h_ref[:, :half], h_ref[:, half:])
    o_ref[...] = _dot(h_ref[...], w_ref[...])


def _pack_bf16_pair(hi, lo):
    hi_bits = lax.bitcast_convert_type(hi.astype(F32), jnp.uint32)
    lo_bits = lax.bitcast_convert_type(lo.astype(F32), jnp.uint32)
    return lax.bitcast_convert_type(hi_bits | (lo_bits >> 16), F32)


def _unpack_bf16_pair(words):
    bits = lax.bitcast_convert_type(words, jnp.uint32)
    hi = lax.bitcast_convert_type(bits & jnp.uint32(0xFFFF0000), F32)
    lo = lax.bitcast_convert_type(bits << 16, F32)
    return hi.astype(BF16), lo.astype(BF16)


def norm_matmul(x, g, w, tm, tn, emit_h=False):
    m, k = x.shape
    n = w.shape[1]
    out_shape = [jax.ShapeDtypeStruct((m, n), F32)]
    out_specs = [pl.BlockSpec((tm, tn), lambda i, j: (i, j))]
    if emit_h:
        out_shape.append(jax.ShapeDtypeStruct((m, k // 2), F32))
        out_specs.append(pl.BlockSpec((tm, k // 2), lambda i, j: (i, 0)))
    res = pl.pallas_call(
        functools.partial(_norm_matmul_kernel, emit_h=emit_h),
        grid=(m // tm, n // tn),
        in_specs=[pl.BlockSpec((tm, k), lambda i, j: (i, 0)),
                  pl.BlockSpec((1, k), lambda i, j: (0, 0)),
                  pl.BlockSpec((k, tn), lambda i, j: (0, j))],
        out_specs=out_specs,
        out_shape=out_shape,
        scratch_shapes=[pltpu.VMEM((tm, k), BF16)],
        compiler_params=_cparams(("parallel", "arbitrary")),
        name="norm_matmul",
    )(x, g.reshape(1, k), w)
    return res if emit_h else res[0]


def _moba_prompt_kernel(q_ref, k_ref, v_ref, rv_ref, o_ref,
                        bias_ref, kb_ref, vb_ref, kmean_ref, m_ref, l_ref, acc_ref,
                        *, n_blocks):
    i = pl.program_id(2)
    blk = MOBA_BLOCK
    seq_len = n_blocks * blk

    @pl.when((pl.program_id(1) == 0) & (i == 0))
    def _():
        for m in range(n_blocks):
            st = blk * (n_blocks - 1 - m)
            win = jnp.broadcast_to(rv_ref[0, :, st:st + 2 * blk], (blk, 2 * blk))
            bias_ref[m] = pltpu.roll(win, 0, 1, stride=1, stride_axis=0)[:, blk:]

    @pl.when(i == 0)
    def _():
        k = k_ref[...]
        kb_ref[:, :HD_A] = k.astype(BF16)
        rblk = lax.broadcasted_iota(jnp.int32, (seq_len, HD_A), 0) // blk
        lane = lax.broadcasted_iota(jnp.int32, (seq_len, HD_A), 1)
        kb_ref[:, HD_A:] = jnp.where(lane == rblk, 1.0, 0.0).astype(BF16)
        vb_ref[...] = v_ref[...].astype(BF16)
        kmean_ref[...] = jnp.mean(k.reshape(n_blocks, blk, HD_A), axis=1)

    qs = q_ref[...] * (HD_A ** -0.5)
    qb = qs.astype(BF16)

    st = _dot_nt(kmean_ref[...], qs, precision=HIGHEST)
    rowi = lax.broadcasted_iota(jnp.int32, st.shape, 0)
    st = jnp.where(rowi < i, st, NEG_INF)
    rank = jnp.zeros(st.shape, F32)
    for c in range(n_blocks - 1):
        sc = st[c:c + 1, :]
        rank = rank + jnp.where((sc > st) | ((sc == st) & (c < rowi)), 1.0, 0.0)
    sel = jnp.where((rank < MOBA_TOPK) & (rowi < i), 1.0, 0.0)
    er = lax.broadcasted_iota(jnp.int32, (n_blocks, HD_A), 0)
    ec = lax.broadcasted_iota(jnp.int32, (n_blocks, HD_A), 1)
    selq = _dot_tn(sel, jnp.where(er == ec, 1.0, 0.0))
    lane = lax.broadcasted_iota(jnp.int32, selq.shape, 1)
    negm = jnp.where((lane < n_blocks) & (selq < 0.5), NEG_INF, 0.0)
    q_aug = jnp.concatenate([qb, negm.astype(BF16)], axis=1)

    row0 = pl.multiple_of(i * blk, blk)
    r = lax.broadcasted_iota(jnp.int32, (blk, blk), 0)
    c_ = lax.broadcasted_iota(jnp.int32, (blk, blk), 1)
    logits = _dot_nt(qb, kb_ref[pl.ds(row0, blk), :HD_A]) + bias_ref[0]
    logits = jnp.where(c_ <= r, logits, NEG_INF)
    m0 = jnp.max(logits, axis=-1, keepdims=True)
    p = jnp.exp(logits - m0)
    m_ref[...] = m0
    l_ref[...] = jnp.sum(p, axis=-1, keepdims=True)
    acc_ref[...] = _dot(p.astype(BF16), vb_ref[pl.ds(row0, blk), :])

    grp = MOBA_GROUP

    def group(t, carry):
        j0 = grp * t
        rows = pl.ds(pl.multiple_of(j0 * blk, grp * blk), grp * blk)
        lg = _dot_nt(q_aug, kb_ref[rows, :])
        lg = lg + jnp.concatenate([bias_ref[jnp.maximum(i - j0 - g, 0)] for g in range(grp)], axis=1)
        m_old = m_ref[...]
        m_new = jnp.maximum(m_old, jnp.max(lg, axis=-1, keepdims=True))
        alpha = jnp.exp(m_old - m_new)
        pj = jnp.exp(lg - m_new)
        m_ref[...] = m_new
        l_ref[...] = alpha * l_ref[...] + jnp.sum(pj, axis=-1, keepdims=True)
        acc_ref[...] = alpha * acc_ref[...] + _dot(pj.astype(BF16), vb_ref[rows, :])
        return carry

    lax.fori_loop(0, (i + grp - 1) // grp, group, 0)
    o_ref[...] = (acc_ref[...] / l_ref[...]).astype(o_ref.dtype)


def moba_prompt(proj, rv, n_seq, seq_len, q_col, k_col, v_col):
    blk = MOBA_BLOCK
    nb = seq_len // blk
    assert nb % MOBA_GROUP == 0
    n_dist = (nb + 1) * blk
    rv = rv.reshape(H_A, 1, n_dist)
    return pl.pallas_call(
        functools.partial(_moba_prompt_kernel, n_blocks=nb),
        grid=(H_A, n_seq, nb),
        in_specs=[pl.BlockSpec((blk, HD_A), lambda h, n, i: (n * nb + i, q_col + h)),
                  pl.BlockSpec((seq_len, HD_A), lambda h, n, i: (n, k_col + h)),
                  pl.BlockSpec((seq_len, HD_A), lambda h, n, i: (n, v_col + h)),
                  pl.BlockSpec((1, 1, n_dist), lambda h, n, i: (h, 0, 0))],
        out_specs=pl.BlockSpec((blk, HD_A), lambda h, n, i: (n * nb + i, h)),
        out_shape=jax.ShapeDtypeStruct((n_seq * seq_len, H_A * HD_A), BF16),
        scratch_shapes=[pltpu.VMEM((nb, blk, blk), F32),
                        pltpu.VMEM((seq_len, 2 * HD_A), BF16),
                        pltpu.VMEM((seq_len, HD_A), BF16),
                        pltpu.VMEM((nb, HD_A), F32),
                        pltpu.VMEM((blk, 1), F32),
                        pltpu.VMEM((blk, 1), F32),
                        pltpu.VMEM((blk, HD_A), F32)],
        compiler_params=_cparams(("arbitrary", "arbitrary", "arbitrary")),
        name="moba_prompt",
    )(proj, proj, proj, rv)


PAGES_PER_STEP = 16
BLOCKS_PER_STEP = PAGES_PER_STEP * PAGE_SIZE // MOBA_BLOCK


def _moba_sample_keys_kernel(pt_ref, qbd_ref, knew_ref, bias_ref, bown_ref, *rest,
                             n_steps, n_tok):
    pps = PAGES_PER_STEP
    bps = BLOCKS_PER_STEP
    ppb = pps // bps
    kp = rest[:pps]
    p_ref, pown_ref, lg_ref, ksum_ref = rest[pps:]
    s = pl.program_id(1)
    qbd = qbd_ref[0]
    qbd_b = qbd.astype(BF16)

    def lg_block(b):
        return b // bps, slice((b % bps) * MOBA_BLOCK, (b % bps + 1) * MOBA_BLOCK)

    sums = []
    for k in range(pps):
        hsum = []
        heads = []
        for h in range(H_A):
            kh = kp[k][0, 0, pl.ds(h, PAGE_SIZE, stride=H_A), :]
            heads.append(kh.astype(BF16))
            hsum.append(jnp.sum(kh, axis=0, keepdims=True))
        page = jnp.concatenate(heads, axis=1)
        lg = _dot_nt(qbd_b, page) + bias_ref[:, k * PAGE_SIZE:(k + 1) * PAGE_SIZE]
        lg_ref[s, :, k * PAGE_SIZE:(k + 1) * PAGE_SIZE] = lg
        sums.append(jnp.concatenate(hsum, axis=1))
    blocks = [sum(sums[ppb * b:ppb * (b + 1)]) * (1.0 / MOBA_BLOCK) for b in range(bps)]
    ksum_ref[pl.ds(pl.multiple_of(s * bps, bps), bps), :] = jnp.concatenate(blocks, axis=0)

    @pl.when(s == n_steps - 1)
    def _():
        n_blk = n_steps * bps
        sc = _dot_nt(qbd, ksum_ref[...], precision=HIGHEST)
        bidx = lax.broadcasted_iota(jnp.int32, sc.shape, 1)
        sel = jnp.zeros(sc.shape, F32)
        for _ in range(MOBA_TOPK):
            mx = jnp.max(sc, axis=-1, keepdims=True)
            first = jnp.min(jnp.where(sc == mx, bidx, n_blk), axis=-1, keepdims=True)
            hit = bidx == first
            sel = jnp.where(hit, 1.0, sel)
            sc = jnp.where(hit, 2.0 * NEG_INF, sc)

        lo = _dot_nt(qbd, knew_ref[...]) + bown_ref[...]
        kc = lax.broadcasted_iota(jnp.int32, lo.shape, 1)
        qr = lax.broadcasted_iota(jnp.int32, lo.shape, 0) % n_tok
        lo = jnp.where(kc <= qr, lo, NEG_INF)

        def masked(b):
            st, sl = lg_block(b)
            return jnp.where(sel[:, b:b + 1] > 0.0, lg_ref[st, :, sl], NEG_INF)

        run = masked(0)
        for b in range(1, n_blk):
            run = jnp.maximum(run, masked(b))
        m = jnp.maximum(jnp.max(lo, axis=-1, keepdims=True), jnp.max(run, axis=-1, keepdims=True))
        e_own = jnp.exp(lo - m)
        tot = jnp.zeros(run.shape, F32)
        for b in range(n_blk):
            st, sl = lg_block(b)
            e = jnp.exp(masked(b) - m)
            lg_ref[st, :, sl] = e
            tot = tot + e
        l = jnp.sum(e_own, axis=-1, keepdims=True) + jnp.sum(tot, axis=-1, keepdims=True)
        inv = 1.0 / l
        pown_ref[0] = e_own * inv
        for b in range(n_blk):
            st, sl = lg_block(b)
            p_ref[0, :, b * MOBA_BLOCK:(b + 1) * MOBA_BLOCK] = (lg_ref[st, :, sl] * inv).astype(BF16)


def _moba_sample_values_kernel(pt_ref, p_ref, pown_ref, vnew_ref, *rest, n_steps, n_tok):
    pps = PAGES_PER_STEP
    vp = rest[:pps]
    o_ref, acc_ref = rest[pps:]
    s = pl.program_id(1)

    @pl.when(s == 0)
    def _():
        acc_ref[...] = _dot(pown_ref[0], vnew_ref[...])

    acc = acc_ref[...]
    for k in range(pps):
        page = jnp.concatenate(
            [vp[k][0, 0, pl.ds(h, PAGE_SIZE, stride=H_A), :].astype(BF16) for h in range(H_A)], axis=1)
        acc = acc + _dot(p_ref[0, :, k * PAGE_SIZE:(k + 1) * PAGE_SIZE], page)
    acc_ref[...] = acc

    @pl.when(s == n_steps - 1)
    def _():
        for h in range(H_A):
            o_ref[:, h * HD_A:(h + 1) * HD_A] = acc[h * n_tok:(h + 1) * n_tok,
                                                    h * HD_A:(h + 1) * HD_A].astype(o_ref.dtype)


def moba_sample(proj, row_blk0, k_colblk, v_colblk, qbd, cache_k, cache_v, layer, page_table,
                bias_past, bias_own, n_tok):
    n_seq, n_pages = page_table.shape
    pps = PAGES_PER_STEP
    n_steps = n_pages // pps
    n_past = n_pages * PAGE_SIZE
    wa = H_A * HD_A
    nrow = H_A * n_tok
    page_block = (1, 1, PAGE_SIZE * H_A, HD_A)
    cache_k = cache_k.reshape(cache_k.shape[:2] + (PAGE_SIZE * H_A, HD_A))
    cache_v = cache_v.reshape(cache_v.shape[:2] + (PAGE_SIZE * H_A, HD_A))

    def pmap(k):
        return lambda b, s, pt: (layer, pt[b, s * pps + k], 0, 0)

    p, p_own = pl.pallas_call(
        functools.partial(_moba_sample_keys_kernel, n_steps=n_steps, n_tok=n_tok),
        grid_spec=pltpu.PrefetchScalarGridSpec(
            num_scalar_prefetch=1,
            grid=(n_seq, n_steps),
            in_specs=[pl.BlockSpec((1, nrow, wa), lambda b, s, pt: (b, 0, 0)),
                      pl.BlockSpec((n_tok, wa), lambda b, s, pt: (row_blk0 + b, k_colblk)),
                      pl.BlockSpec((nrow, pps * PAGE_SIZE), lambda b, s, pt: (0, s)),
                      pl.BlockSpec((nrow, n_tok), lambda b, s, pt: (0, 0))]
            + [pl.BlockSpec(page_block, pmap(k)) for k in range(pps)],
            out_specs=[pl.BlockSpec((1, nrow, n_past), lambda b, s, pt: (b, 0, 0)),
                       pl.BlockSpec((1, nrow, n_tok), lambda b, s, pt: (b, 0, 0))],
            scratch_shapes=[pltpu.VMEM((n_steps, nrow, pps * PAGE_SIZE), F32),
                            pltpu.VMEM((n_past // MOBA_BLOCK, wa), F32)]),
        out_shape=[jax.ShapeDtypeStruct((n_seq, nrow, n_past), BF16),
                   jax.ShapeDtypeStruct((n_seq, nrow, n_tok), F32)],
        compiler_params=_cparams(("arbitrary", "arbitrary")),
        name="moba_sample_keys",
    )(page_table, qbd, proj, bias_past, bias_own, *([cache_k] * pps))

    return pl.pallas_call(
        functools.partial(_moba_sample_values_kernel, n_steps=n_steps, n_tok=n_tok),
        grid_spec=pltpu.PrefetchScalarGridSpec(
            num_scalar_prefetch=1,
            grid=(n_seq, n_steps),
            in_specs=[pl.BlockSpec((1, nrow, pps * PAGE_SIZE), lambda b, s, pt: (b, 0, s)),
                      pl.BlockSpec((1, nrow, n_tok), lambda b, s, pt: (b, 0, 0)),
                      pl.BlockSpec((n_tok, wa), lambda b, s, pt: (row_blk0 + b, v_colblk))]
            + [pl.BlockSpec(page_block, pmap(k)) for k in range(pps)],
            out_specs=pl.BlockSpec((n_tok, wa), lambda b, s, pt: (b, 0)),
            scratch_shapes=[pltpu.VMEM((nrow, wa), F32)]),
        out_shape=jax.ShapeDtypeStruct((n_seq * n_tok, wa), BF16),
        compiler_params=_cparams(("arbitrary", "arbitrary")),
        name="moba_sample_values",
    )(page_table, p, p_own, proj, *([cache_v] * pps))


def _gdn_prep_kernel(x_ref, halo_ref, ab_ref, cw_ref, alog_ref, dtb_ref,
                     q_ref, k_ref, v_ref, gb_ref, *, tt):
    x = x_ref[...]
    xf = jnp.concatenate([halo_ref[0], x], axis=0)
    cw = cw_ref[...]
    conv = xf[5:5 + tt] * cw[0:1]
    for w in range(1, CONV_WIDTH):
        conv = conv + xf[5 + w:5 + w + tt] * cw[w:w + 1]
    act = conv * jax.nn.sigmoid(conv)
    wk = H_B * DK_B
    for h in range(H_B):
        q = act[:, h * DK_B:(h + 1) * DK_B]
        k = act[:, wk + h * DK_B:wk + (h + 1) * DK_B]
        q_ref[0, h] = q * lax.rsqrt(jnp.sum(q * q, axis=-1, keepdims=True) + RMS_EPS) * (DK_B ** -0.5)
        k_ref[0, h] = k * lax.rsqrt(jnp.sum(k * k, axis=-1, keepdims=True) + RMS_EPS)
        v_ref[0, h] = act[:, 2 * wk + h * DK_B:2 * wk + (h + 1) * DK_B]
    ab = ab_ref[...]
    t = ab + dtb_ref[...]
    sp = jnp.maximum(t, 0.0) + jnp.log(1.0 + jnp.exp(-jnp.abs(t)))
    g = -jnp.exp(alog_ref[...]) * sp
    lane = lax.broadcasted_iota(jnp.int32, ab.shape, 1)
    gb_ref[0] = jnp.where(lane < H_B, g, jax.nn.sigmoid(ab))


def gdn_prep(proj, ab, halo, conv_w, a_log, dt_bias, row0, n_seq, seq_len, tt, qkv_colblk):
    w3 = conv_w.shape[1]
    nt = seq_len // tt
    rb0 = row0 // tt
    pad = jnp.zeros((1, 128 - H_B), F32)
    alog = jnp.concatenate([a_log.reshape(1, H_B), pad], axis=1)
    dtb = jnp.concatenate([dt_bias.reshape(1, H_B), pad], axis=1)
    hm = jax.ShapeDtypeStruct((n_seq, H_B, seq_len, DK_B), F32)
    hspec = pl.BlockSpec((1, H_B, tt, DK_B), lambda n, t: (n, 0, t, 0))
    return pl.pallas_call(
        functools.partial(_gdn_prep_kernel, tt=tt),
        grid=(n_seq, nt),
        in_specs=[pl.BlockSpec((tt, w3), lambda n, t: (rb0 + n * nt + t, qkv_colblk)),
                  pl.BlockSpec((1, 8, w3), lambda n, t: (n * nt + t, 0, 0)),
                  pl.BlockSpec((tt, 128), lambda n, t: (rb0 + n * nt + t, 0)),
                  pl.BlockSpec((CONV_WIDTH, w3), lambda n, t: (0, 0)),
                  pl.BlockSpec((1, 128), lambda n, t: (0, 0)),
                  pl.BlockSpec((1, 128), lambda n, t: (0, 0))],
        out_specs=[hspec, hspec, hspec,
                   pl.BlockSpec((1, tt, 128), lambda n, t: (n, t, 0))],
        out_shape=[hm, hm, hm, jax.ShapeDtypeStruct((n_seq, seq_len, 128), F32)],
        compiler_params=_cparams(("parallel", "parallel")),
        name="gdn_prep",
    )(proj, halo, ab, conv_w, alog, dtb)


def _split2(a):
    hi = a.astype(BF16)
    return hi, (a - hi.astype(F32)).astype(BF16)


def _split3(a):
    hi = a.astype(BF16)
    r1 = a - hi.astype(F32)
    mid = r1.astype(BF16)
    return hi, mid, (r1 - mid.astype(F32)).astype(BF16)


def _dot3(a, b):
    ah, al = _split2(a)
    bh, bl = _split2(b)
    return _dot(ah, bh) + (_dot(ah, bl) + _dot(al, bh))


def _dot_exact_lhs(a_bf16, b):
    b1, b2, b3 = _split3(b)
    return _dot(a_bf16, b1) + (_dot(a_bf16, b2) + _dot(a_bf16, b3))


def _unit_lower_inverse(a_list, c):
    r = lax.broadcasted_iota(jnp.int32, (c, c), 0)
    cc = lax.broadcasted_iota(jnp.int32, (c, c), 1)
    eye = jnp.where(r == cc, 1.0, 0.0)
    base = min(c, 8)
    n1 = [jnp.where((r // base) == (cc // base), -a, 0.0) for a in a_list]
    n2 = [_dot3(x, x) for x in n1]
    n4 = [_dot3(x, x) for x in n2]
    t = [_dot3(eye + x, eye + y) for x, y in zip(n1, n2)]
    t = [_dot3(x, eye + y) for x, y in zip(t, n4)]
    b = base
    while b < c:
        inner = ((r // (2 * b)) == (cc // (2 * b))) & ((r // b) != (cc // b))
        left = [_dot3(x, jnp.where(inner, a, 0.0)) for x, a in zip(t, a_list)]
        t = [x - _dot3(y, x) for x, y in zip(t, left)]
        b *= 2
    return t


def _gdn_chunk_kernel(*refs, nb, c):
    q_ref, k_ref, v_ref, gb_ref = refs[:4]
    z_refs = refs[4:4 + nb]
    ng_ref, s0_ref, o_ref, sout_ref, s_ref = refs[4 + nb:]
    ci = pl.program_id(1)

    @pl.when(ci == 0)
    def _():
        s_ref[...] = s0_ref[...]

    r = lax.broadcasted_iota(jnp.int32, (c, c), 0)
    cc = lax.broadcasted_iota(jnp.int32, (c, c), 1)
    incl = r >= cc
    strict = r > cc
    ltri = jnp.where(incl, 1.0, 0.0).astype(BF16)
    eye = jnp.where(r == cc, 1.0, 0.0)
    ones = jnp.ones((c, c), BF16)
    ng = ng_ref[...]
    ch = [(n, h) for n in range(nb) for h in range(H_B)]
    gbv = [gb_ref[n] for n in range(nb)]
    q = [q_ref[n, h] for n, h in ch]
    k = [k_ref[n, h] for n, h in ch]
    v = [v_ref[n, h] for n, h in ch]
    gcol = [jnp.broadcast_to(gbv[n][:, h:h + 1], (c, DK_B)) for n, h in ch]
    bcol = [jnp.broadcast_to(gbv[n][:, H_B + h:H_B + h + 1], (c, DK_B)) for n, h in ch]
    big_g = [_dot_exact_lhs(ltri, g) for g in gcol]
    gj = [_dot_exact_lhs(ones, g[:, :c] * eye) for g in big_g]
    decay = [jnp.where(incl, jnp.exp(jnp.minimum(g[:, :c] - x, 0.0)), 0.0) for g, x in zip(big_g, gj)]
    kk = [_dot_nt(x, x) for x in k]
    a = [jnp.where(strict, b[:, :c] * x * d, 0.0) for b, x, d in zip(bcol, kk, decay)]
    t = _unit_lower_inverse(a, c)
    gam = [jnp.exp(g) for g in big_g]
    u = [_dot3(x, b * y) for x, b, y in zip(t, bcol, v)]
    w = [_dot3(x, b * g * y) for x, b, g, y in zip(t, bcol, gam, k)]
    qk = [_dot_nt(x, y) * d for x, y, d in zip(q, k, decay)]
    g_last = [g[c - 1:c, :] for g in big_g]
    k_tail = [jnp.exp(gl - g) * y for gl, g, y in zip(g_last, big_g, k)]
    s = [s_ref[n, h] for n, h in ch]
    uu = [x - _dot(y, z_) for x, y, z_ in zip(u, w, s)]
    o = [_dot(g * x, z_) + _dot(y, x2) for g, x, z_, y, x2 in zip(gam, q, s, qk, uu)]
    s_new = [jnp.exp(gl) * z_ + _dot_tn(y, x) for gl, z_, y, x in zip(g_last, s, k_tail, uu)]
    for (n, h), x in zip(ch, s_new):
        s_ref[n, h] = x
    for (n, h), x in zip(ch, o):
        x = x * lax.rsqrt(jnp.mean(x * x, axis=-1, keepdims=True) + RMS_EPS) * ng
        zh = z_refs[n][:, h * DK_B:(h + 1) * DK_B]
        o_ref[n, :, h * DK_B:(h + 1) * DK_B] = (x * (zh * jax.nn.sigmoid(zh))).astype(o_ref.dtype)

    @pl.when(ci == pl.num_programs(1) - 1)
    def _():
        sout_ref[...] = s_ref[...]


def gdn_chunks(q, k, v, gb, proj, z_colblk, row0, norm_g, s0, nb, c):
    n_seq, _, seq_len, _ = q.shape
    nc = seq_len // c
    wv = H_B * DK_B
    rb0 = row0 // c
    hspec = pl.BlockSpec((nb, H_B, c, DK_B), lambda n, ci: (n, 0, ci, 0))
    sspec = pl.BlockSpec((nb, H_B, DK_B, DK_B), lambda n, ci: (n, 0, 0, 0))

    def zmap(j):
        return lambda n, ci: (rb0 + (n * nb + j) * nc + ci, z_colblk)

    return pl.pallas_call(
        functools.partial(_gdn_chunk_kernel, nb=nb, c=c),
        grid=(n_seq // nb, nc),
        in_specs=[hspec, hspec, hspec,
                  pl.BlockSpec((nb, c, 128), lambda n, ci: (n, ci, 0))]
        + [pl.BlockSpec((c, wv), zmap(j)) for j in range(nb)]
        + [pl.BlockSpec((1, DK_B), lambda n, ci: (0, 0)), sspec],
        out_specs=[pl.BlockSpec((nb, c, wv), lambda n, ci: (n, ci, 0)), sspec],
        out_shape=[jax.ShapeDtypeStruct((n_seq, seq_len, wv), BF16),
                   jax.ShapeDtypeStruct((n_seq, H_B, DK_B, DK_B), F32)],
        scratch_shapes=[pltpu.VMEM((nb, H_B, DK_B, DK_B), F32)],
        compiler_params=_cparams(("parallel", "arbitrary")),
        name="gdn_chunks",
    )(q, k, v, gb, *([proj] * nb), norm_g.reshape(1, DK_B), s0)


def _merge_kernel(oa_ref, ob_ref, wa_ref, wb_ref, ga_ref, gb_ref, o_ref):
    ya = _dot(oa_ref[...], wa_ref[...])
    yb = _dot(ob_ref[...], wb_ref[...])
    o_ref[...] = (jax.nn.sigmoid(ga_ref[...]) * ya
                  + jax.nn.sigmoid(gb_ref[...]) * yb).astype(o_ref.dtype)


def branch_merge(out_a, out_b, wa, wb, proj, ga_col0, gb_col0, tm, tn):
    m, ka = out_a.shape
    n = wa.shape[1]
    ga_blk, gb_blk = ga_col0 // tn, gb_col0 // tn
    assert ga_blk * tn == ga_col0 and gb_blk * tn == gb_col0
    return pl.pallas_call(
        _merge_kernel,
        grid=(m // tm, n // tn),
        in_specs=[pl.BlockSpec((tm, ka), lambda i, j: (i, 0)),
                  pl.BlockSpec((tm, ka), lambda i, j: (i, 0)),
                  pl.BlockSpec((ka, tn), lambda i, j: (0, j)),
                  pl.BlockSpec((ka, tn), lambda i, j: (0, j)),
                  pl.BlockSpec((tm, tn), lambda i, j: (i, ga_blk + j)),
                  pl.BlockSpec((tm, tn), lambda i, j: (i, gb_blk + j))],
        out_specs=pl.BlockSpec((tm, tn), lambda i, j: (i, j)),
        out_shape=jax.ShapeDtypeStruct((m, n), BF16),
        compiler_params=_cparams(("parallel", "parallel")),
        name="branch_merge",
    )(out_a, out_b, wa, wb, proj, proj)


def _matmul_residual_kernel(x_ref, w_ref, r_ref, o_ref):
    o_ref[...] = r_ref[...] + _dot(x_ref[...], w_ref[...])


def matmul_residual(x, w, res, tm, tn):
    m, k = x.shape
    n = w.shape[1]
    return pl.pallas_call(
        _matmul_residual_kernel,
        grid=(m // tm, n // tn),
        in_specs=[pl.BlockSpec((tm, k), lambda i, j: (i, 0)),
                  pl.BlockSpec((k, tn), lambda i, j: (0, j)),
                  pl.BlockSpec((tm, tn), lambda i, j: (i, j))],
        out_specs=pl.BlockSpec((tm, tn), lambda i, j: (i, j)),
        out_shape=jax.ShapeDtypeStruct((m, n), F32),
        compiler_params=_cparams(("parallel", "parallel")),
        name="out_proj",
    )(x, w, res)


MOE_TA = 256


def _lane_rotate(x, shift):
    parts = [pltpu.roll(x[:, c:c + 128], shift, 1) for c in range(0, x.shape[1], 128)]
    return parts[0] if len(parts) == 1 else jnp.concatenate(parts, axis=1)


MOE_RUN = 4


def _block_out_copy(buf_ref, out_ref, sem_ref, j, first_block, col_tile):
    rows, cols = buf_ref.shape[1:]
    dst = out_ref.at[pl.ds(pl.multiple_of((first_block + j) * rows, rows), rows),
                     pl.ds(pl.multiple_of(col_tile * cols, cols), cols)]
    return pltpu.make_async_copy(buf_ref.at[j], dst, sem_ref.at[j])


def _zero_fill(buf_ref, out_ref, sem_ref, n_fill, first_block, col_tile):
    for j in range(buf_ref.shape[0]):
        @pl.when(j < n_fill)
        def _(j=j):
            buf_ref[j] = jnp.zeros(buf_ref.shape[1:], buf_ref.dtype)
            _block_out_copy(buf_ref, out_ref, sem_ref, j, first_block, col_tile).start()
    for j in range(buf_ref.shape[0]):
        @pl.when(j < n_fill)
        def _(j=j):
            _block_out_copy(buf_ref, out_ref, sem_ref, j, first_block, col_tile).wait()


def _moe_up_kernel(re_ref, rb_ref, rl_ref, nr_ref, *refs):
    r = MOE_RUN
    x_refs = refs[:r]
    wa_ref, wb_ref, ba_ref, bb_ref, out_ref, wab_ref, wbb_ref, buf_ref, sem_ref = refs[r:]
    v = pl.program_id(0)
    n = pl.program_id(1)

    @pl.when(v >= nr_ref[0])
    def _():
        _zero_fill(buf_ref, out_ref, sem_ref, rl_ref[v], rb_ref[v], n)

    @pl.when(v < nr_ref[0])
    def _():
        wab_ref[...] = wa_ref[0].astype(BF16)
        wbb_ref[...] = wb_ref[0].astype(BF16)
        n_live = rl_ref[v]
        for j in range(r):
            @pl.when(j < n_live)
            def _(j=j):
                half = x_refs[j].shape[1]
                x_hi, x_lo = _unpack_bf16_pair(x_refs[j][...])
                ga = (_dot(x_hi, wab_ref[:half, :]) + _dot(x_lo, wab_ref[half:, :])
                      + ba_ref[0])
                gb = _dot(x_hi, wbb_ref[:half, :]) + _dot(x_lo, wbb_ref[half:, :]) + bb_ref[0]
                even = (lax.broadcasted_iota(jnp.int32, ga.shape, 1) % 2) == 0
                gate = jnp.where(even, ga, _lane_rotate(gb, 1))
                up = jnp.where(even, _lane_rotate(ga, 127), gb)
                gl = jnp.minimum(gate, SWIGLU_LIMIT)
                up = jnp.clip(up, -SWIGLU_LIMIT, SWIGLU_LIMIT)
                buf_ref[j] = (gl * jax.nn.sigmoid(SWIGLU_ALPHA * gl) * (up + 1.0)).astype(BF16)
                _block_out_copy(buf_ref, out_ref, sem_ref, j, rb_ref[v], n).start()
        for j in range(r):
            @pl.when(j < n_live)
            def _(j=j):
                _block_out_copy(buf_ref, out_ref, sem_ref, j, rb_ref[v], n).wait()


def _bf16_bits(w):
    return lax.bitcast_convert_type(w.astype(BF16).astype(F32), jnp.uint32)


def _moe_down_kernel(re_ref, rb_ref, rl_ref, nr_ref, *refs):
    r = MOE_RUN
    a_refs = refs[:r]
    wd_ref, bd_ref, out_ref, wdb_ref, buf_ref, sem_ref = refs[r:]
    v = pl.program_id(0)
    n = pl.program_id(1)

    @pl.when(v >= nr_ref[0])
    def _():
        _zero_fill(buf_ref, out_ref, sem_ref, rl_ref[v], rb_ref[v], n)

    @pl.when(v < nr_ref[0])
    def _():
        f = wd_ref.shape[1]
        for t in range(f // (2 * MOE_TA)):
            wa = wd_ref[0, t * MOE_TA:(t + 1) * MOE_TA, :]
            wb = wd_ref[0, f // 2 + t * MOE_TA:f // 2 + (t + 1) * MOE_TA, :]
            pair = (_bf16_bits(wa) >> 16) | _bf16_bits(wb)
            wdb_ref[t * 2 * MOE_TA:(t + 1) * 2 * MOE_TA, :] = pltpu.bitcast(pair, BF16)
        n_live = rl_ref[v]
        for j in range(r):
            @pl.when(j < n_live)
            def _(j=j):
                buf_ref[j] = _dot(a_refs[j][...], wdb_ref[...]) + bd_ref[0]
                _block_out_copy(buf_ref, out_ref, sem_ref, j, rb_ref[v], n).start()
        for j in range(r):
            @pl.when(j < n_live)
            def _(j=j):
                _block_out_copy(buf_ref, out_ref, sem_ref, j, rb_ref[v], n).wait()


def moe_experts(x_rows, runs, w_gate_up, b_gate_up, w_down, b_down, tn_down):
    rows = x_rows.shape[0]
    n_exp, d, f2 = w_gate_up.shape
    f = f2 // 2
    dm = w_down.shape[2]
    n_steps = runs[0].shape[0]
    r = MOE_RUN

    def rowmap(j):
        return lambda v, n, re, rb, rl, nr: (rb[v] + jnp.minimum(j, rl[v] - 1), 0)

    def wmap(off, n_tiles):
        return lambda v, n, re, rb, rl, nr: (re[v], 0, off + jnp.where(v < nr[0], n, n_tiles - 1))

    nt = f // (2 * MOE_TA)
    tw = 2 * MOE_TA
    bgu = b_gate_up.reshape(n_exp, 1, f2)
    act = pl.pallas_call(
        _moe_up_kernel,
        grid_spec=pltpu.PrefetchScalarGridSpec(
            num_scalar_prefetch=4,
            grid=(n_steps, nt),
            in_specs=[pl.BlockSpec((MOE_ROWS, d // 2), rowmap(j)) for j in range(r)]
            + [pl.BlockSpec((1, d, tw), wmap(0, nt)),
               pl.BlockSpec((1, d, tw), wmap(nt, nt)),
               pl.BlockSpec((1, 1, tw), wmap(0, nt)),
               pl.BlockSpec((1, 1, tw), wmap(nt, nt))],
            out_specs=pl.BlockSpec(memory_space=pl.ANY),
            scratch_shapes=[pltpu.VMEM((d, tw), BF16), pltpu.VMEM((d, tw), BF16),
                            pltpu.VMEM((r, MOE_ROWS, tw), BF16), pltpu.SemaphoreType.DMA((r,))]),
        out_shape=jax.ShapeDtypeStruct((rows, f), BF16),
        compiler_params=_cparams(("arbitrary", "arbitrary")),
        name="moe_up",
    )(*runs, *([x_rows] * r), w_gate_up, w_gate_up, bgu, bgu)
    nt_d = dm // tn_down
    return pl.pallas_call(
        _moe_down_kernel,
        grid_spec=pltpu.PrefetchScalarGridSpec(
            num_scalar_prefetch=4,
            grid=(n_steps, nt_d),
            in_specs=[pl.BlockSpec((MOE_ROWS, f), rowmap(j)) for j in range(r)]
            + [pl.BlockSpec((1, f, tn_down), wmap(0, nt_d)),
               pl.BlockSpec((1, 1, tn_down), wmap(0, nt_d))],
            out_specs=pl.BlockSpec(memory_space=pl.ANY),
            scratch_shapes=[pltpu.VMEM((f, tn_down), BF16),
                            pltpu.VMEM((r, MOE_ROWS, tn_down), F32), pltpu.SemaphoreType.DMA((r,))]),
        out_shape=jax.ShapeDtypeStruct((rows, dm), F32),
        compiler_params=_cparams(("arbitrary", "arbitrary")),
        name="moe_down",
    )(*runs, *([act] * r), w_down, b_down.reshape(n_exp, 1, dm))


def _final_kernel(y_ref, *rest):
    e_refs = rest[:TOP_K]
    gate_ref, g_ref, o_ref = rest[TOP_K:]
    x = y_ref[...]
    for kk in range(TOP_K):
        x = x + e_refs[kk][...] * gate_ref[:, kk:kk + 1]
    o_ref[...] = x * lax.rsqrt(jnp.mean(x * x, axis=-1, keepdims=True) + RMS_EPS) * g_ref[...]


def combine_residual_norm(y, expert_out, gate, g, tm):
    m, d = y.shape
    nt = m // tm
    return pl.pallas_call(
        _final_kernel,
        grid=(nt,),
        in_specs=[pl.BlockSpec((tm, d), lambda i: (i, 0))]
        + [pl.BlockSpec((tm, d), functools.partial(lambda i, kk: (kk * nt + i, 0), kk=kk))
           for kk in range(TOP_K)]
        + [pl.BlockSpec((tm, TOP_K), lambda i: (i, 0)),
           pl.BlockSpec((1, d), lambda i: (0, 0))],
        out_specs=pl.BlockSpec((tm, d), lambda i: (i, 0)),
        out_shape=jax.ShapeDtypeStruct((m, d), F32),
        compiler_params=_cparams(("parallel",)),
        name="combine_norm",
    )(y, *([expert_out] * TOP_K), gate, g.reshape(1, d))


def _rel_bias_table(rel_bias, dist):
    n = jnp.maximum(dist, 0)
    max_exact = N_BUCKETS // 2
    nf = jnp.maximum(n, max_exact).astype(F32)
    large = max_exact + (jnp.log(nf / max_exact) / math.log(MAX_DISTANCE / max_exact)
                         * (N_BUCKETS - max_exact)).astype(jnp.int32)
    bucket = jnp.where(n < max_exact, n, jnp.minimum(large, N_BUCKETS - 1))
    return rel_bias[:, bucket].astype(F32)


def _route(logits, n_tok):
    top_v, top_e = lax.top_k(logits, TOP_K)
    gate = jax.nn.softmax(top_v, axis=-1)
    n_exp = logits.shape[1]
    n_assign = n_tok * TOP_K
    flat_e = top_e.reshape(-1)
    order = jnp.argsort(flat_e, stable=True)
    e_sorted = flat_e[order]
    tok_sorted = (order // TOP_K).astype(jnp.int32)
    counts = jnp.zeros((n_exp,), jnp.int32).at[flat_e].add(1)
    padded = (counts + MOE_ROWS - 1) // MOE_ROWS * MOE_ROWS
    pad_end = jnp.cumsum(padded)
    pad_start = pad_end - padded
    start = jnp.cumsum(counts) - counts
    dest = (pad_start[e_sorted] + jnp.arange(n_assign) - start[e_sorted]).astype(jnp.int32)
    n_blocks = -(-n_assign // MOE_ROWS) + n_exp
    row_tok = jnp.zeros((n_blocks * MOE_ROWS,), jnp.int32).at[dest].add(tok_sorted)
    pos = jnp.zeros((n_assign,), jnp.int32).at[order].add(dest).reshape(n_tok, TOP_K)
    nblk_e = padded // MOE_ROWS
    nrun_e = (nblk_e + MOE_RUN - 1) // MOE_RUN
    run_end = jnp.cumsum(nrun_e)
    run_start = run_end - nrun_e
    n_runs = run_end[-1]
    n_steps = n_exp + -(-n_blocks // MOE_RUN)
    steps = jnp.arange(n_steps)
    step = jnp.minimum(steps, n_runs - 1)
    run_e = jnp.sum(run_end[None, :] <= step[:, None], axis=1)
    piece = step - run_start[run_e]
    run_b0 = pad_start[run_e] // MOE_ROWS + piece * MOE_RUN
    run_len = jnp.minimum(MOE_RUN, nblk_e[run_e] - piece * MOE_RUN)
    tail_b0 = jnp.minimum(pad_end[-1] // MOE_ROWS + (steps - n_runs) * MOE_RUN, n_blocks)
    tail_len = jnp.clip(n_blocks - tail_b0, 0, MOE_RUN)
    real = steps < n_runs
    run_b0 = jnp.where(real, run_b0, tail_b0)
    run_len = jnp.where(real, run_len, tail_len)
    runs = (run_e.astype(jnp.int32), run_b0.astype(jnp.int32), run_len.astype(jnp.int32),
            n_runs.astype(jnp.int32).reshape(1))
    return gate, row_tok, runs, pos


def kernel(x_prompt, x_sample, cache_k, cache_v, state_gdn, state_conv, page_table, norm_mix_g, w_in, rel_bias, conv_w, a_log, dt_bias, gdn_norm_g, w_branch_a, w_branch_b, w_out, norm_ffn_g, router_w, router_b, w_gate_up, b_gate_up, w_down, b_down, norm_final_g):
    n_p, seq, d = x_prompt.shape
    n_s, dseq, _ = x_sample.shape
    assert w_in.shape[0] == 1
    l = 0
    wa = H_A * HD_A
    wk = H_B * DK_B
    qkv_b = 3 * wk
    n_prow = n_p * seq
    n_srow = n_s * dseq
    n_tok = n_prow + n_srow
    past = page_table.shape[1] * PAGE_SIZE
    assert past % MOBA_BLOCK == 0 and seq % MOBA_BLOCK == 0

    x = jnp.concatenate([x_prompt.reshape(n_prow, d), x_sample.reshape(n_srow, d)], axis=0)

    w = w_in[l]
    c_z = 3 * wa + qkv_b
    c_ab = c_z + wk
    c_g = c_ab + 2 * H_B
    w_main = jnp.concatenate([w[:, :c_ab], w[:, c_g:]], axis=1).astype(BF16)
    w_ab = jnp.concatenate([w[:, c_ab:c_g], jnp.zeros((d, 128 - 2 * H_B), F32)], axis=1).astype(BF16)
    tm = 768
    proj = norm_matmul(x, norm_mix_g[l], w_main, tm, 1024)
    ab = norm_matmul(x, norm_mix_g[l], w_ab, tm, 128)

    k_all = proj[:, wa:2 * wa]
    v_all = proj[:, 2 * wa:3 * wa]

    d_max = max(past + dseq - 1, seq)
    far_first = _rel_bias_table(rel_bias, d_max - jnp.arange(d_max + MOBA_BLOCK))
    out_a_p = moba_prompt(proj, far_first[:, d_max - seq:], n_p, seq, 0, H_A, 2 * H_A)

    q_s = proj[n_prow:, :wa].reshape(n_s, dseq, H_A, HD_A) * (HD_A ** -0.5)
    qbd = jnp.einsum('blhd,hg->bhlgd', q_s, jnp.eye(H_A, dtype=F32)).reshape(n_s, H_A * dseq, wa)
    lead = d_max - (past + dseq - 1)
    bias_past = jnp.stack([far_first[:, lead + dseq - 1 - t:lead + dseq - 1 - t + past]
                           for t in range(dseq)], axis=1)
    bias_past = bias_past.reshape(H_A * dseq, past)
    bias_own = jnp.stack([far_first[:, d_max - t:d_max - t + dseq] for t in range(dseq)], axis=1)
    bias_own = bias_own.reshape(H_A * dseq, dseq)
    out_a_s = moba_sample(proj, n_prow // dseq, 1, 2, qbd, cache_k, cache_v, l, page_table,
                          bias_past, bias_own, dseq)
    out_a = jnp.concatenate([out_a_p, out_a_s], axis=0)

    qkv_colblk = 3 * wa // qkv_b
    assert qkv_colblk * qkv_b == 3 * wa
    tt = 512
    raw_p = proj[:n_prow, 3 * wa:3 * wa + qkv_b].reshape(n_p, seq // tt, tt, qkv_b)
    halo0 = jnp.zeros((n_p, 1, 8, qkv_b), F32)
    halo_p = jnp.concatenate([halo0, raw_p[:, :-1, tt - 8:, :]], axis=1).reshape(n_p * (seq // tt), 8, qkv_b)
    qp, kp, vp, gbp = gdn_prep(proj, ab, halo_p, conv_w[l], a_log[l], dt_bias[l],
                               0, n_p, seq, tt, qkv_colblk)
    raw_s = proj[n_prow:, 3 * wa:3 * wa + qkv_b].reshape(n_s, dseq, qkv_b)
    halo_s = jnp.concatenate([jnp.zeros((n_s, 8 - (CONV_WIDTH - 1), qkv_b), F32), state_conv[l]], axis=1)
    qs_, ks_, vs_, gbs = gdn_prep(proj, ab, halo_s, conv_w[l], a_log[l], dt_bias[l],
                                  n_prow, n_s, dseq, dseq, qkv_colblk)
    z_colblk = c_z // wk
    assert z_colblk * wk == c_z
    c_p = math.gcd(seq, GDN_CHUNK)
    out_b_p, s_p = gdn_chunks(qp, kp, vp, gbp, proj, z_colblk, 0, gdn_norm_g[l],
                              jnp.zeros((n_p, H_B, DK_B, DK_B), F32), n_p, c_p)
    c_s = math.gcd(dseq, GDN_CHUNK)
    out_b_s, s_s = gdn_chunks(qs_, ks_, vs_, gbs, proj, z_colblk, n_prow, gdn_norm_g[l],
                              state_gdn[l], 2, c_s)
    out_b = jnp.concatenate([out_b_p.reshape(n_prow, wk), out_b_s.reshape(n_srow, wk)], axis=0)
    conv_p = raw_p.reshape(n_p, seq, qkv_b)[:, seq - (CONV_WIDTH - 1):, :]
    conv_s = jnp.concatenate([state_conv[l], raw_s], axis=1)[:, dseq:, :]

    merged = branch_merge(out_a, out_b, w_branch_a[l].astype(BF16), w_branch_b[l].astype(BF16),
                          proj, c_ab, c_ab + d, tm, 1024)
    y1 = matmul_residual(merged, w_out[l].astype(BF16), x, tm, 1024)

    n_exp = router_w.shape[2]
    w_r = jnp.concatenate([router_w[l], jnp.zeros((d, 128 - n_exp), F32)], axis=1).astype(BF16)
    r_logits, h2 = norm_matmul(y1, norm_ffn_g[l], w_r, tm, 128, emit_h=True)
    logits = r_logits[:, :n_exp] + router_b[l].astype(F32)
    gate, row_tok, runs, pos = _route(logits, n_tok)
    x_rows = h2[lax.optimization_barrier(row_tok)]
    y_rows = moe_experts(x_rows, runs, w_gate_up.reshape(w_gate_up.shape[1:]),
                         b_gate_up.reshape(b_gate_up.shape[1:]), w_down.reshape(w_down.shape[1:]),
                         b_down.reshape(b_down.shape[1:]), 512)
    picked = y_rows[pos.T.reshape(-1)]
    y = combine_residual_norm(y1, picked, gate, norm_final_g, 384)

    y_prompt = y[:n_prow].reshape(n_p, seq, d)
    y_sample = y[n_prow:].reshape(n_s, dseq, d)
    k_p = k_all[:n_prow].reshape(1, n_p, seq, H_A, HD_A)
    v_p = v_all[:n_prow].reshape(1, n_p, seq, H_A, HD_A)
    k_s = k_all[n_prow:].reshape(1, n_s, dseq, H_A, HD_A)
    v_s = v_all[n_prow:].reshape(1, n_s, dseq, H_A, HD_A)
    return (y_prompt, y_sample, k_p, v_p, s_p[None], conv_p[None],
            k_s, v_s, s_s[None], conv_s[None])
```

```python
import functools
import math

import jax
import jax.numpy as jnp
from jax import lax
from jax.experimental import pallas as pl
from jax.experimental.pallas import tpu as pltpu

F32 = jnp.float32
BF16 = jnp.bfloat16
HIGHEST = lax.Precision.HIGHEST

RMS_EPS = 1e-6
NEG_INF = -1e30

H_A = 8
HD_A = 128
MOBA_BLOCK = 256
MOBA_TOPK = 3
MOBA_GROUP = 4
PAGE_SIZE = 128
N_BUCKETS = 32
MAX_DISTANCE = 4096

H_B = 8
DK_B = 128
CONV_WIDTH = 4
GDN_CHUNK = 64

TOP_K = 4
SWIGLU_LIMIT = 7.0
SWIGLU_ALPHA = 1.702
MOE_ROWS = 512

VMEM_LIMIT = 56 * 1024 * 1024


def _cparams(sem):
    return pltpu.CompilerParams(dimension_semantics=sem, vmem_limit_bytes=VMEM_LIMIT)


def _dot(a, b, precision=None):
    return jnp.dot(a, b, preferred_element_type=F32, precision=precision)


def _dot_nt(a, b, precision=None):
    return lax.dot_general(a, b, (((1,), (1,)), ((), ())),
                           preferred_element_type=F32, precision=precision)


def _dot_tn(a, b, precision=None):
    return lax.dot_general(a, b, (((0,), (0,)), ((), ())),
                           preferred_element_type=F32, precision=precision)


def _norm_matmul_kernel(x_ref, g_ref, w_ref, o_ref, *rest, emit_h):
    if emit_h:
        h_out_ref, h_ref = rest
    else:
        (h_ref,) = rest

    @pl.when(pl.program_id(1) == 0)
    def _():
        x = x_ref[...]
        y = x * lax.rsqrt(jnp.mean(x * x, axis=-1, keepdims=True) + RMS_EPS)
        h_ref[...] = (y * g_ref[...]).astype(BF16)

    if emit_h:
        half = h_ref.shape[1] // 2
        h_out_ref[...] = _pack_bf16_pair(h_ref[:, :half], h_ref[:, half:])
    o_ref[...] = _dot(h_ref[...], w_ref[...])


def _pack_bf16_pair(hi, lo):
    hi_bits = lax.bitcast_convert_type(hi.astype(F32), jnp.uint32)
    lo_bits = lax.bitcast_convert_type(lo.astype(F32), jnp.uint32)
    return lax.bitcast_convert_type(hi_bits | (lo_bits >> 16), F32)


def _unpack_bf16_pair(words):
    bits = lax.bitcast_convert_type(words, jnp.uint32)
    hi = lax.bitcast_convert_type(bits & jnp.uint32(0xFFFF0000), F32)
    lo = lax.bitcast_convert_type(bits << 16, F32)
    return hi.astype(BF16), lo.astype(BF16)


def norm_matmul(x, g, w, tm, tn, emit_h=False):
    m, k = x.shape
    n = w.shape[1]
    out_shape = [jax.ShapeDtypeStruct((m, n), F32)]
    out_specs = [pl.BlockSpec((tm, tn), lambda i, j: (i, j))]
    if emit_h:
        out_shape.append(jax.ShapeDtypeStruct((m, k // 2), F32))
        out_specs.append(pl.BlockSpec((tm, k // 2), lambda i, j: (i, 0)))
    res = pl.pallas_call(
        functools.partial(_norm_matmul_kernel, emit_h=emit_h),
        grid=(m // tm, n // tn),
        in_specs=[pl.BlockSpec((tm, k), lambda i, j: (i, 0)),
                  pl.BlockSpec((1, k), lambda i, j: (0, 0)),
                  pl.BlockSpec((k, tn), lambda i, j: (0, j))],
        out_specs=out_specs,
        out_shape=out_shape,
        scratch_shapes=[pltpu.VMEM((tm, k), BF16)],
        compiler_params=_cparams(("parallel", "arbitrary")),
        name="norm_matmul",
    )(x, g.reshape(1, k), w)
    return res if emit_h else res[0]


def _moba_prompt_kernel(q_ref, k_ref, v_ref, rv_ref, o_ref,
                        bias_ref, kb_ref, vb_ref, kmean_ref, m_ref, l_ref, acc_ref,
                        *, n_blocks):
    i = pl.program_id(2)
    blk = MOBA_BLOCK
    seq_len = n_blocks * blk

    @pl.when((pl.program_id(1) == 0) & (i == 0))
    def _():
        for m in range(n_blocks):
            st = blk * (n_blocks - 1 - m)
            win = jnp.broadcast_to(rv_ref[0, :, st:st + 2 * blk], (blk, 2 * blk))
            bias_ref[m] = pltpu.roll(win, 0, 1, stride=1, stride_axis=0)[:, blk:]

    @pl.when(i == 0)
    def _():
        k = k_ref[...]
        kb_ref[:, :HD_A] = k.astype(BF16)
        rblk = lax.broadcasted_iota(jnp.int32, (seq_len, HD_A), 0) // blk
        lane = lax.broadcasted_iota(jnp.int32, (seq_len, HD_A), 1)
        kb_ref[:, HD_A:] = jnp.where(lane == rblk, 1.0, 0.0).astype(BF16)
        vb_ref[...] = v_ref[...].astype(BF16)
        kmean_ref[...] = jnp.mean(k.reshape(n_blocks, blk, HD_A), axis=1)

    qs = q_ref[...] * (HD_A ** -0.5)
    qb = qs.astype(BF16)

    st = _dot_nt(kmean_ref[...], qs, precision=HIGHEST)
    row0 = pl.multiple_of(i * blk, blk)
    logits = _dot_nt(qb, kb_ref[pl.ds(row0, blk), :HD_A]) + bias_ref[0]
    rowi = lax.broadcasted_iota(jnp.int32, st.shape, 0)
    st = jnp.where(rowi < i, st, NEG_INF)
    r = lax.broadcasted_iota(jnp.int32, (blk, blk), 0)
    c_ = lax.broadcasted_iota(jnp.int32, (blk, blk), 1)
    logits = jnp.where(c_ <= r, logits, NEG_INF)
    m0 = jnp.max(logits, axis=-1, keepdims=True)
    rank = jnp.zeros(st.shape, F32)
    for c in range(n_blocks - 1):
        sc = st[c:c + 1, :]
        rank = rank + jnp.where((sc > st) | ((sc == st) & (c < rowi)), 1.0, 0.0)
    sel = jnp.where((rank < MOBA_TOPK) & (rowi < i), 1.0, 0.0)
    p = jnp.exp(logits - m0)
    er = lax.broadcasted_iota(jnp.int32, (n_blocks, HD_A), 0)
    ec = lax.broadcasted_iota(jnp.int32, (n_blocks, HD_A), 1)
    selq = _dot_tn(sel, jnp.where(er == ec, 1.0, 0.0))
    m_ref[...] = m0
    l_ref[...] = jnp.sum(p, axis=-1, keepdims=True)
    acc_ref[...] = _dot(p.astype(BF16), vb_ref[pl.ds(row0, blk), :])
    lane = lax.broadcasted_iota(jnp.int32, selq.shape, 1)
    negm = jnp.where((lane < n_blocks) & (selq < 0.5), NEG_INF, 0.0)
    q_aug = jnp.concatenate([qb, negm.astype(BF16)], axis=1)

    grp = MOBA_GROUP

    def group(t, carry):
        j0 = grp * t
        rows = pl.ds(pl.multiple_of(j0 * blk, grp * blk), grp * blk)
        lg = _dot_nt(q_aug, kb_ref[rows, :])
        lg = lg + jnp.concatenate([bias_ref[jnp.maximum(i - j0 - g, 0)] for g in range(grp)], axis=1)
        m_old = m_ref[...]
        m_new = jnp.maximum(m_old, jnp.max(lg, axis=-1, keepdims=True))
        alpha = jnp.exp(m_old - m_new)
        pj = jnp.exp(lg - m_new)
        m_ref[...] = m_new
        l_ref[...] = alpha * l_ref[...] + jnp.sum(pj, axis=-1, keepdims=True)
        acc_ref[...] = alpha * acc_ref[...] + _dot(pj.astype(BF16), vb_ref[rows, :])
        return carry

    lax.fori_loop(0, (i + grp - 1) // grp, group, 0)
    o_ref[...] = (acc_ref[...] / l_ref[...]).astype(o_ref.dtype)


def moba_prompt(proj, rv, n_seq, seq_len, q_col, k_col, v_col):
    blk = MOBA_BLOCK
    nb = seq_len // blk
    assert nb % MOBA_GROUP == 0
    n_dist = (nb + 1) * blk
    rv = rv.reshape(H_A, 1, n_dist)
    return pl.pallas_call(
        functools.partial(_moba_prompt_kernel, n_blocks=nb),
        grid=(H_A, n_seq, nb),
        in_specs=[pl.BlockSpec((blk, HD_A), lambda h, n, i: (n * nb + i, q_col + h)),
                  pl.BlockSpec((seq_len, HD_A), lambda h, n, i: (n, k_col + h)),
                  pl.BlockSpec((seq_len, HD_A), lambda h, n, i: (n, v_col + h)),
                  pl.BlockSpec((1, 1, n_dist), lambda h, n, i: (h, 0, 0))],
        out_specs=pl.BlockSpec((blk, HD_A), lambda h, n, i: (n * nb + i, h)),
        out_shape=jax.ShapeDtypeStruct((n_seq * seq_len, H_A * HD_A), BF16),
        scratch_shapes=[pltpu.VMEM((nb, blk, blk), F32),
                        pltpu.VMEM((seq_len, 2 * HD_A), BF16),
                        pltpu.VMEM((seq_len, HD_A), BF16),
                        pltpu.VMEM((nb, HD_A), F32),
                        pltpu.VMEM((blk, 1), F32),
                        pltpu.VMEM((blk, 1), F32),
                        pltpu.VMEM((blk, HD_A), F32)],
        compiler_params=_cparams(("arbitrary", "arbitrary", "arbitrary")),
        name="moba_prompt",
    )(proj, proj, proj, rv)


PAGES_PER_STEP = 16
BLOCKS_PER_STEP = PAGES_PER_STEP * PAGE_SIZE // MOBA_BLOCK


def _moba_sample_keys_kernel(pt_ref, qbd_ref, knew_ref, bias_ref, bown_ref, *rest,
                             n_steps, n_tok):
    pps = PAGES_PER_STEP
    bps = BLOCKS_PER_STEP
    ppb = pps // bps
    kp = rest[:pps]
    p_ref, pown_ref, lg_ref, ksum_ref = rest[pps:]
    s = pl.program_id(1)
    qbd = qbd_ref[0]
    qbd_b = qbd.astype(BF16)

    def lg_block(b):
        return b // bps, slice((b % bps) * MOBA_BLOCK, (b % bps + 1) * MOBA_BLOCK)

    sums = []
    for k in range(pps):
        hsum = []
        heads = []
        for h in range(H_A):
            kh = kp[k][0, 0, pl.ds(h, PAGE_SIZE, stride=H_A), :]
            heads.append(kh.astype(BF16))
            hsum.append(jnp.sum(kh, axis=0, keepdims=True))
        page = jnp.concatenate(heads, axis=1)
        lg = _dot_nt(qbd_b, page) + bias_ref[:, k * PAGE_SIZE:(k + 1) * PAGE_SIZE]
        lg_ref[s, :, k * PAGE_SIZE:(k + 1) * PAGE_SIZE] = lg
        sums.append(jnp.concatenate(hsum, axis=1))
    blocks = [sum(sums[ppb * b:ppb * (b + 1)]) * (1.0 / MOBA_BLOCK) for b in range(bps)]
    ksum_ref[pl.ds(pl.multiple_of(s * bps, bps), bps), :] = jnp.concatenate(blocks, axis=0)

    @pl.when(s == n_steps - 1)
    def _():
        n_blk = n_steps * bps
        sc = _dot_nt(qbd, ksum_ref[...], precision=HIGHEST)
        bidx = lax.broadcasted_iota(jnp.int32, sc.shape, 1)
        sel = jnp.zeros(sc.shape, F32)
        for _ in range(MOBA_TOPK):
            mx = jnp.max(sc, axis=-1, keepdims=True)
            first = jnp.min(jnp.where(sc == mx, bidx, n_blk), axis=-1, keepdims=True)
            hit = bidx == first
            sel = jnp.where(hit, 1.0, sel)
            sc = jnp.where(hit, 2.0 * NEG_INF, sc)

        lo = _dot_nt(qbd, knew_ref[...]) + bown_ref[...]
        kc = lax.broadcasted_iota(jnp.int32, lo.shape, 1)
        qr = lax.broadcasted_iota(jnp.int32, lo.shape, 0) % n_tok
        lo = jnp.where(kc <= qr, lo, NEG_INF)

        def masked(b):
            st, sl = lg_block(b)
            return jnp.where(sel[:, b:b + 1] > 0.0, lg_ref[st, :, sl], NEG_INF)

        run = masked(0)
        for b in range(1, n_blk):
            run = jnp.maximum(run, masked(b))
        m = jnp.maximum(jnp.max(lo, axis=-1, keepdims=True), jnp.max(run, axis=-1, keepdims=True))
        e_own = jnp.exp(lo - m)
        tot = jnp.zeros(run.shape, F32)
        for b in range(n_blk):
            st, sl = lg_block(b)
            e = jnp.exp(masked(b) - m)
            lg_ref[st, :, sl] = e
            tot = tot + e
        l = jnp.sum(e_own, axis=-1, keepdims=True) + jnp.sum(tot, axis=-1, keepdims=True)
        inv = 1.0 / l
        pown_ref[0] = e_own * inv
        for b in range(n_blk):
            st, sl = lg_block(b)
            p_ref[0, :, b * MOBA_BLOCK:(b + 1) * MOBA_BLOCK] = (lg_ref[st, :, sl] * inv).astype(BF16)


def _moba_sample_values_kernel(pt_ref, p_ref, pown_ref, vnew_ref, *rest, n_steps, n_tok):
    pps = PAGES_PER_STEP
    vp = rest[:pps]
    o_ref, acc_ref = rest[pps:]
    s = pl.program_id(1)

    @pl.when(s == 0)
    def _():
        acc_ref[...] = _dot(pown_ref[0], vnew_ref[...])

    acc = acc_ref[...]
    for k in range(pps):
        page = jnp.concatenate(
            [vp[k][0, 0, pl.ds(h, PAGE_SIZE, stride=H_A), :].astype(BF16) for h in range(H_A)], axis=1)
        acc = acc + _dot(p_ref[0, :, k * PAGE_SIZE:(k + 1) * PAGE_SIZE], page)
    acc_ref[...] = acc

    @pl.when(s == n_steps - 1)
    def _():
        for h in range(H_A):
            o_ref[:, h * HD_A:(h + 1) * HD_A] = acc[h * n_tok:(h + 1) * n_tok,
                                                    h * HD_A:(h + 1) * HD_A].astype(o_ref.dtype)


def moba_sample(proj, row_blk0, k_colblk, v_colblk, qbd, cache_k, cache_v, layer, page_table,
                bias_past, bias_own, n_tok):
    n_seq, n_pages = page_table.shape
    pps = PAGES_PER_STEP
    n_steps = n_pages // pps
    n_past = n_pages * PAGE_SIZE
    wa = H_A * HD_A
    nrow = H_A * n_tok
    page_block = (1, 1, PAGE_SIZE * H_A, HD_A)
    cache_k = cache_k.reshape(cache_k.shape[:2] + (PAGE_SIZE * H_A, HD_A))
    cache_v = cache_v.reshape(cache_v.shape[:2] + (PAGE_SIZE * H_A, HD_A))

    def pmap(k):
        return lambda b, s, pt: (layer, pt[b, s * pps + k], 0, 0)

    p, p_own = pl.pallas_call(
        functools.partial(_moba_sample_keys_kernel, n_steps=n_steps, n_tok=n_tok),
        grid_spec=pltpu.PrefetchScalarGridSpec(
            num_scalar_prefetch=1,
            grid=(n_seq, n_steps),
            in_specs=[pl.BlockSpec((1, nrow, wa), lambda b, s, pt: (b, 0, 0)),
                      pl.BlockSpec((n_tok, wa), lambda b, s, pt: (row_blk0 + b, k_colblk)),
                      pl.BlockSpec((nrow, pps * PAGE_SIZE), lambda b, s, pt: (0, s)),
                      pl.BlockSpec((nrow, n_tok), lambda b, s, pt: (0, 0))]
            + [pl.BlockSpec(page_block, pmap(k)) for k in range(pps)],
            out_specs=[pl.BlockSpec((1, nrow, n_past), lambda b, s, pt: (b, 0, 0)),
                       pl.BlockSpec((1, nrow, n_tok), lambda b, s, pt: (b, 0, 0))],
            scratch_shapes=[pltpu.VMEM((n_steps, nrow, pps * PAGE_SIZE), F32),
                            pltpu.VMEM((n_past // MOBA_BLOCK, wa), F32)]),
        out_shape=[jax.ShapeDtypeStruct((n_seq, nrow, n_past), BF16),
                   jax.ShapeDtypeStruct((n_seq, nrow, n_tok), F32)],
        compiler_params=_cparams(("arbitrary", "arbitrary")),
        name="moba_sample_keys",
    )(page_table, qbd, proj, bias_past, bias_own, *([cache_k] * pps))

    return pl.pallas_call(
        functools.partial(_moba_sample_values_kernel, n_steps=n_steps, n_tok=n_tok),
        grid_spec=pltpu.PrefetchScalarGridSpec(
            num_scalar_prefetch=1,
            grid=(n_seq, n_steps),
            in_specs=[pl.BlockSpec((1, nrow, pps * PAGE_SIZE), lambda b, s, pt: (b, 0, s)),
                      pl.BlockSpec((1, nrow, n_tok), lambda b, s, pt: (b, 0, 0)),
                      pl.BlockSpec((n_tok, wa), lambda b, s, pt: (row_blk0 + b, v_colblk))]
            + [pl.BlockSpec(page_block, pmap(k)) for k in range(pps)],
            out_specs=pl.BlockSpec((n_tok, wa), lambda b, s, pt: (b, 0)),
            scratch_shapes=[pltpu.VMEM((nrow, wa), F32)]),
        out_shape=jax.ShapeDtypeStruct((n_seq * n_tok, wa), BF16),
        compiler_params=_cparams(("arbitrary", "arbitrary")),
        name="moba_sample_values",
    )(page_table, p, p_own, proj, *([cache_v] * pps))


def _gdn_prep_kernel(x_ref, halo_ref, ab_ref, cw_ref, alog_ref, dtb_ref,
                     q_ref, k_ref, v_ref, gb_ref, *, tt):
    x = x_ref[...]
    xf = jnp.concatenate([halo_ref[0], x], axis=0)
    cw = cw_ref[...]
    conv = xf[5:5 + tt] * cw[0:1]
    for w in range(1, CONV_WIDTH):
        conv = conv + xf[5 + w:5 + w + tt] * cw[w:w + 1]
    act = conv * jax.nn.sigmoid(conv)
    wk = H_B * DK_B
    for h in range(H_B):
        q = act[:, h * DK_B:(h + 1) * DK_B]
        k = act[:, wk + h * DK_B:wk + (h + 1) * DK_B]
        q_ref[0, h] = q * lax.rsqrt(jnp.sum(q * q, axis=-1, keepdims=True) + RMS_EPS) * (DK_B ** -0.5)
        k_ref[0, h] = k * lax.rsqrt(jnp.sum(k * k, axis=-1, keepdims=True) + RMS_EPS)
        v_ref[0, h] = act[:, 2 * wk + h * DK_B:2 * wk + (h + 1) * DK_B]
    ab = ab_ref[...]
    t = ab + dtb_ref[...]
    sp = jnp.maximum(t, 0.0) + jnp.log(1.0 + jnp.exp(-jnp.abs(t)))
    g = -jnp.exp(alog_ref[...]) * sp
    lane = lax.broadcasted_iota(jnp.int32, ab.shape, 1)
    gb_ref[0] = jnp.where(lane < H_B, g, jax.nn.sigmoid(ab))


def gdn_prep(proj, ab, halo, conv_w, a_log, dt_bias, row0, n_seq, seq_len, tt, qkv_colblk):
    w3 = conv_w.shape[1]
    nt = seq_len // tt
    rb0 = row0 // tt
    pad = jnp.zeros((1, 128 - H_B), F32)
    alog = jnp.concatenate([a_log.reshape(1, H_B), pad], axis=1)
    dtb = jnp.concatenate([dt_bias.reshape(1, H_B), pad], axis=1)
    hm = jax.ShapeDtypeStruct((n_seq, H_B, seq_len, DK_B), F32)
    hspec = pl.BlockSpec((1, H_B, tt, DK_B), lambda n, t: (n, 0, t, 0))
    return pl.pallas_call(
        functools.partial(_gdn_prep_kernel, tt=tt),
        grid=(n_seq, nt),
        in_specs=[pl.BlockSpec((tt, w3), lambda n, t: (rb0 + n * nt + t, qkv_colblk)),
                  pl.BlockSpec((1, 8, w3), lambda n, t: (n * nt + t, 0, 0)),
                  pl.BlockSpec((tt, 128), lambda n, t: (rb0 + n * nt + t, 0)),
                  pl.BlockSpec((CONV_WIDTH, w3), lambda n, t: (0, 0)),
                  pl.BlockSpec((1, 128), lambda n, t: (0, 0)),
                  pl.BlockSpec((1, 128), lambda n, t: (0, 0))],
        out_specs=[hspec, hspec, hspec,
                   pl.BlockSpec((1, tt, 128), lambda n, t: (n, t, 0))],
        out_shape=[hm, hm, hm, jax.ShapeDtypeStruct((n_seq, seq_len, 128), F32)],
        compiler_params=_cparams(("parallel", "parallel")),
        name="gdn_prep",
    )(proj, halo, ab, conv_w, alog, dtb)


def _split2(a):
    hi = a.astype(BF16)
    return hi, (a - hi.astype(F32)).astype(BF16)


def _split3(a):
    hi = a.astype(BF16)
    r1 = a - hi.astype(F32)
    mid = r1.astype(BF16)
    return hi, mid, (r1 - mid.astype(F32)).astype(BF16)


def _dot3(a, b):
    ah, al = _split2(a)
    bh, bl = _split2(b)
    return _dot(ah, bh) + (_dot(ah, bl) + _dot(al, bh))


def _dot_exact_lhs(a_bf16, b):
    b1, b2, b3 = _split3(b)
    return _dot(a_bf16, b1) + (_dot(a_bf16, b2) + _dot(a_bf16, b3))


def _unit_lower_inverse(a_list, c):
    r = lax.broadcasted_iota(jnp.int32, (c, c), 0)
    cc = lax.broadcasted_iota(jnp.int32, (c, c), 1)
    eye = jnp.where(r == cc, 1.0, 0.0)
    base = min(c, 8)
    n1 = [jnp.where((r // base) == (cc // base), -a, 0.0) for a in a_list]
    n2 = [_dot3(x, x) for x in n1]
    n4 = [_dot3(x, x) for x in n2]
    t = [_dot3(eye + x, eye + y) for x, y in zip(n1, n2)]
    t = [_dot3(x, eye + y) for x, y in zip(t, n4)]
    b = base
    while b < c:
        inner = ((r // (2 * b)) == (cc // (2 * b))) & ((r // b) != (cc // b))
        left = [_dot3(x, jnp.where(inner, a, 0.0)) for x, a in zip(t, a_list)]
        t = [x - _dot3(y, x) for x, y in zip(t, left)]
        b *= 2
    return t


def _gdn_chunk_kernel(*refs, nb, c):
    q_ref, k_ref, v_ref, gb_ref = refs[:4]
    z_refs = refs[4:4 + nb]
    ng_ref, s0_ref, o_ref, sout_ref, s_ref = refs[4 + nb:]
    ci = pl.program_id(1)

    @pl.when(ci == 0)
    def _():
        s_ref[...] = s0_ref[...]

    r = lax.broadcasted_iota(jnp.int32, (c, c), 0)
    cc = lax.broadcasted_iota(jnp.int32, (c, c), 1)
    incl = r >= cc
    strict = r > cc
    ltri = jnp.where(incl, 1.0, 0.0).astype(BF16)
    eye = jnp.where(r == cc, 1.0, 0.0)
    ones = jnp.ones((c, c), BF16)
    ng = ng_ref[...]
    ch = [(n, h) for n in range(nb) for h in range(H_B)]
    gbv = [gb_ref[n] for n in range(nb)]
    q = [q_ref[n, h] for n, h in ch]
    k = [k_ref[n, h] for n, h in ch]
    v = [v_ref[n, h] for n, h in ch]
    gcol = [jnp.broadcast_to(gbv[n][:, h:h + 1], (c, DK_B)) for n, h in ch]
    bcol = [jnp.broadcast_to(gbv[n][:, H_B + h:H_B + h + 1], (c, DK_B)) for n, h in ch]
    big_g = [_dot_exact_lhs(ltri, g) for g in gcol]
    gj = [_dot_exact_lhs(ones, g[:, :c] * eye) for g in big_g]
    decay = [jnp.where(incl, jnp.exp(jnp.minimum(g[:, :c] - x, 0.0)), 0.0) for g, x in zip(big_g, gj)]
    kk = [_dot_nt(x, x) for x in k]
    a = [jnp.where(strict, b[:, :c] * x * d, 0.0) for b, x, d in zip(bcol, kk, decay)]
    t = _unit_lower_inverse(a, c)
    gam = [jnp.exp(g) for g in big_g]
    u = [_dot3(x, b * y) for x, b, y in zip(t, bcol, v)]
    w = [_dot3(x, b * g * y) for x, b, g, y in zip(t, bcol, gam, k)]
    qk = [_dot_nt(x, y) * d for x, y, d in zip(q, k, decay)]
    g_last = [g[c - 1:c, :] for g in big_g]
    k_tail = [jnp.exp(gl - g) * y for gl, g, y in zip(g_last, big_g, k)]
    s = [s_ref[n, h] for n, h in ch]
    uu = [x - _dot(y, z_) for x, y, z_ in zip(u, w, s)]
    o = [_dot(g * x, z_) + _dot(y, x2) for g, x, z_, y, x2 in zip(gam, q, s, qk, uu)]
    s_new = [jnp.exp(gl) * z_ + _dot_tn(y, x) for gl, z_, y, x in zip(g_last, s, k_tail, uu)]
    for (n, h), x in zip(ch, s_new):
        s_ref[n, h] = x
    for (n, h), x in zip(ch, o):
        x = x * lax.rsqrt(jnp.mean(x * x, axis=-1, keepdims=True) + RMS_EPS) * ng
        zh = z_refs[n][:, h * DK_B:(h + 1) * DK_B]
        o_ref[n, :, h * DK_B:(h + 1) * DK_B] = (x * (zh * jax.nn.sigmoid(zh))).astype(o_ref.dtype)

    @pl.when(ci == pl.num_programs(1) - 1)
    def _():
        sout_ref[...] = s_ref[...]


def gdn_chunks(q, k, v, gb, proj, z_colblk, row0, norm_g, s0, nb, c):
    n_seq, _, seq_len, _ = q.shape
    nc = seq_len // c
    wv = H_B * DK_B
    rb0 = row0 // c
    hspec = pl.BlockSpec((nb, H_B, c, DK_B), lambda n, ci: (n, 0, ci, 0))
    sspec = pl.BlockSpec((nb, H_B, DK_B, DK_B), lambda n, ci: (n, 0, 0, 0))

    def zmap(j):
        return lambda n, ci: (rb0 + (n * nb + j) * nc + ci, z_colblk)

    return pl.pallas_call(
        functools.partial(_gdn_chunk_kernel, nb=nb, c=c),
        grid=(n_seq // nb, nc),
        in_specs=[hspec, hspec, hspec,
                  pl.BlockSpec((nb, c, 128), lambda n, ci: (n, ci, 0))]
        + [pl.BlockSpec((c, wv), zmap(j)) for j in range(nb)]
        + [pl.BlockSpec((1, DK_B), lambda n, ci: (0, 0)), sspec],
        out_specs=[pl.BlockSpec((nb, c, wv), lambda n, ci: (n, ci, 0)), sspec],
        out_shape=[jax.ShapeDtypeStruct((n_seq, seq_len, wv), BF16),
                   jax.ShapeDtypeStruct((n_seq, H_B, DK_B, DK_B), F32)],
        scratch_shapes=[pltpu.VMEM((nb, H_B, DK_B, DK_B), F32)],
        compiler_params=_cparams(("parallel", "arbitrary")),
        name="gdn_chunks",
    )(q, k, v, gb, *([proj] * nb), norm_g.reshape(1, DK_B), s0)


def _merge_kernel(oa_ref, ob_ref, wa_ref, wb_ref, ga_ref, gb_ref, o_ref):
    ya = _dot(oa_ref[...], wa_ref[...])
    yb = _dot(ob_ref[...], wb_ref[...])
    o_ref[...] = (jax.nn.sigmoid(ga_ref[...]) * ya
                  + jax.nn.sigmoid(gb_ref[...]) * yb).astype(o_ref.dtype)


def branch_merge(out_a, out_b, wa, wb, proj, ga_col0, gb_col0, tm, tn):
    m, ka = out_a.shape
    n = wa.shape[1]
    ga_blk, gb_blk = ga_col0 // tn, gb_col0 // tn
    assert ga_blk * tn == ga_col0 and gb_blk * tn == gb_col0
    return pl.pallas_call(
        _merge_kernel,
        grid=(m // tm, n // tn),
        in_specs=[pl.BlockSpec((tm, ka), lambda i, j: (i, 0)),
                  pl.BlockSpec((tm, ka), lambda i, j: (i, 0)),
                  pl.BlockSpec((ka, tn), lambda i, j: (0, j)),
                  pl.BlockSpec((ka, tn), lambda i, j: (0, j)),
                  pl.BlockSpec((tm, tn), lambda i, j: (i, ga_blk + j)),
                  pl.BlockSpec((tm, tn), lambda i, j: (i, gb_blk + j))],
        out_specs=pl.BlockSpec((tm, tn), lambda i, j: (i, j)),
        out_shape=jax.ShapeDtypeStruct((m, n), BF16),
        compiler_params=_cparams(("parallel", "parallel")),
        name="branch_merge",
    )(out_a, out_b, wa, wb, proj, proj)


def _matmul_residual_kernel(x_ref, w_ref, r_ref, o_ref):
    o_ref[...] = r_ref[...] + _dot(x_ref[...], w_ref[...])


def matmul_residual(x, w, res, tm, tn):
    m, k = x.shape
    n = w.shape[1]
    return pl.pallas_call(
        _matmul_residual_kernel,
        grid=(m // tm, n // tn),
        in_specs=[pl.BlockSpec((tm, k), lambda i, j: (i, 0)),
                  pl.BlockSpec((k, tn), lambda i, j: (0, j)),
                  pl.BlockSpec((tm, tn), lambda i, j: (i, j))],
        out_specs=pl.BlockSpec((tm, tn), lambda i, j: (i, j)),
        out_shape=jax.ShapeDtypeStruct((m, n), F32),
        compiler_params=_cparams(("parallel", "parallel")),
        name="out_proj",
    )(x, w, res)


MOE_TA = 256


def _lane_rotate(x, shift):
    parts = [pltpu.roll(x[:, c:c + 128], shift, 1) for c in range(0, x.shape[1], 128)]
    return parts[0] if len(parts) == 1 else jnp.concatenate(parts, axis=1)


MOE_RUN = 4


def _block_out_copy(buf_ref, out_ref, sem_ref, j, first_block, col_tile):
    rows, cols = buf_ref.shape[1:]
    dst = out_ref.at[pl.ds(pl.multiple_of((first_block + j) * rows, rows), rows),
                     pl.ds(pl.multiple_of(col_tile * cols, cols), cols)]
    return pltpu.make_async_copy(buf_ref.at[j], dst, sem_ref.at[j])


def _zero_fill(buf_ref, out_ref, sem_ref, n_fill, first_block, col_tile):
    for j in range(buf_ref.shape[0]):
        @pl.when(j < n_fill)
        def _(j=j):
            buf_ref[j] = jnp.zeros(buf_ref.shape[1:], buf_ref.dtype)
            _block_out_copy(buf_ref, out_ref, sem_ref, j, first_block, col_tile).start()
    for j in range(buf_ref.shape[0]):
        @pl.when(j < n_fill)
        def _(j=j):
            _block_out_copy(buf_ref, out_ref, sem_ref, j, first_block, col_tile).wait()


def _moe_up_kernel(re_ref, rb_ref, rl_ref, nr_ref, *refs):
    r = MOE_RUN
    x_refs = refs[:r]
    wa_ref, wb_ref, ba_ref, bb_ref, out_ref, wab_ref, wbb_ref, buf_ref, sem_ref = refs[r:]
    v = pl.program_id(0)
    n = pl.program_id(1)

    @pl.when(v >= nr_ref[0])
    def _():
        _zero_fill(buf_ref, out_ref, sem_ref, rl_ref[v], rb_ref[v], n)

    @pl.when(v < nr_ref[0])
    def _():
        wab_ref[...] = wa_ref[0].astype(BF16)
        wbb_ref[...] = wb_ref[0].astype(BF16)
        n_live = rl_ref[v]
        for j in range(r):
            @pl.when(j < n_live)
            def _(j=j):
                half = x_refs[j].shape[1]
                x_hi, x_lo = _unpack_bf16_pair(x_refs[j][...])
                ga = (_dot(x_hi, wab_ref[:half, :]) + _dot(x_lo, wab_ref[half:, :])
                      + ba_ref[0])
                gb = _dot(x_hi, wbb_ref[:half, :]) + _dot(x_lo, wbb_ref[half:, :]) + bb_ref[0]
                even = (lax.broadcasted_iota(jnp.int32, ga.shape, 1) % 2) == 0
                gate = jnp.where(even, ga, _lane_rotate(gb, 1))
                up = jnp.where(even, _lane_rotate(ga, 127), gb)
                gl = jnp.minimum(gate, SWIGLU_LIMIT)
                up = jnp.clip(up, -SWIGLU_LIMIT, SWIGLU_LIMIT)
                buf_ref[j] = (gl * jax.nn.sigmoid(SWIGLU_ALPHA * gl) * (up + 1.0)).astype(BF16)
                _block_out_copy(buf_ref, out_ref, sem_ref, j, rb_ref[v], n).start()
        for j in range(r):
            @pl.when(j < n_live)
            def _(j=j):
                _block_out_copy(buf_ref, out_ref, sem_ref, j, rb_ref[v], n).wait()


def _bf16_bits(w):
    return lax.bitcast_convert_type(w.astype(BF16).astype(F32), jnp.uint32)


def _moe_down_kernel(re_ref, rb_ref, rl_ref, nr_ref, *refs):
    r = MOE_RUN
    a_refs = refs[:r]
    wd_ref, bd_ref, out_ref, wdb_ref, buf_ref, sem_ref = refs[r:]
    v = pl.program_id(0)
    n = pl.program_id(1)

    @pl.when(v >= nr_ref[0])
    def _():
        _zero_fill(buf_ref, out_ref, sem_ref, rl_ref[v], rb_ref[v], n)

    @pl.when(v < nr_ref[0])
    def _():
        f = wd_ref.shape[1]
        for t in range(f // (2 * MOE_TA)):
            wa = wd_ref[0, t * MOE_TA:(t + 1) * MOE_TA, :]
            wb = wd_ref[0, f // 2 + t * MOE_TA:f // 2 + (t + 1) * MOE_TA, :]
            pair = (_bf16_bits(wa) >> 16) | _bf16_bits(wb)
            wdb_ref[t * 2 * MOE_TA:(t + 1) * 2 * MOE_TA, :] = pltpu.bitcast(pair, BF16)
        n_live = rl_ref[v]
        for j in range(r):
            @pl.when(j < n_live)
            def _(j=j):
                buf_ref[j] = _dot(a_refs[j][...], wdb_ref[...]) + bd_ref[0]
                _block_out_copy(buf_ref, out_ref, sem_ref, j, rb_ref[v], n).start()
        for j in range(r):
            @pl.when(j < n_live)
            def _(j=j):
                _block_out_copy(buf_ref, out_ref, sem_ref, j, rb_ref[v], n).wait()


def moe_experts(x_rows, runs, w_gate_up, b_gate_up, w_down, b_down, tn_down):
    rows = x_rows.shape[0]
    n_exp, d, f2 = w_gate_up.shape
    f = f2 // 2
    dm = w_down.shape[2]
    n_steps = runs[0].shape[0]
    r = MOE_RUN

    def rowmap(j):
        return lambda v, n, re, rb, rl, nr: (rb[v] + jnp.minimum(j, rl[v] - 1), 0)

    def wmap(off, n_tiles):
        return lambda v, n, re, rb, rl, nr: (re[v], 0, off + jnp.where(v < nr[0], n, n_tiles - 1))

    nt = f // (2 * MOE_TA)
    tw = 2 * MOE_TA
    bgu = b_gate_up.reshape(n_exp, 1, f2)
    act = pl.pallas_call(
        _moe_up_kernel,
        grid_spec=pltpu.PrefetchScalarGridSpec(
            num_scalar_prefetch=4,
            grid=(n_steps, nt),
            in_specs=[pl.BlockSpec((MOE_ROWS, d // 2), rowmap(j)) for j in range(r)]
            + [pl.BlockSpec((1, d, tw), wmap(0, nt)),
               pl.BlockSpec((1, d, tw), wmap(nt, nt)),
               pl.BlockSpec((1, 1, tw), wmap(0, nt)),
               pl.BlockSpec((1, 1, tw), wmap(nt, nt))],
            out_specs=pl.BlockSpec(memory_space=pl.ANY),
            scratch_shapes=[pltpu.VMEM((d, tw), BF16), pltpu.VMEM((d, tw), BF16),
                            pltpu.VMEM((r, MOE_ROWS, tw), BF16), pltpu.SemaphoreType.DMA((r,))]),
        out_shape=jax.ShapeDtypeStruct((rows, f), BF16),
        compiler_params=_cparams(("arbitrary", "arbitrary")),
        name="moe_up",
    )(*runs, *([x_rows] * r), w_gate_up, w_gate_up, bgu, bgu)
    nt_d = dm // tn_down
    return pl.pallas_call(
        _moe_down_kernel,
        grid_spec=pltpu.PrefetchScalarGridSpec(
            num_scalar_prefetch=4,
            grid=(n_steps, nt_d),
            in_specs=[pl.BlockSpec((MOE_ROWS, f), rowmap(j)) for j in range(r)]
            + [pl.BlockSpec((1, f, tn_down), wmap(0, nt_d)),
               pl.BlockSpec((1, 1, tn_down), wmap(0, nt_d))],
            out_specs=pl.BlockSpec(memory_space=pl.ANY),
            scratch_shapes=[pltpu.VMEM((f, tn_down), BF16),
                            pltpu.VMEM((r, MOE_ROWS, tn_down), F32), pltpu.SemaphoreType.DMA((r,))]),
        out_shape=jax.ShapeDtypeStruct((rows, dm), F32),
        compiler_params=_cparams(("arbitrary", "arbitrary")),
        name="moe_down",
    )(*runs, *([act] * r), w_down, b_down.reshape(n_exp, 1, dm))


def _final_kernel(y_ref, *rest, n_first):
    e_refs = rest[:TOP_K]
    gate_ref, g_ref, o1_ref, o2_ref = rest[TOP_K:]
    x = y_ref[...]
    for kk in range(TOP_K):
        x = x + e_refs[kk][...] * gate_ref[:, kk:kk + 1]
    out = x * lax.rsqrt(jnp.mean(x * x, axis=-1, keepdims=True) + RMS_EPS) * g_ref[...]
    i = pl.program_id(0)

    @pl.when(i < n_first)
    def _():
        o1_ref[...] = out

    @pl.when(i >= n_first)
    def _():
        o2_ref[...] = out


def combine_residual_norm(y, expert_out, gate, g, tm, m_first):
    m, d = y.shape
    nt = m // tm
    n_first = m_first // tm
    assert nt * tm == m and n_first * tm == m_first and 0 < n_first < nt
    return pl.pallas_call(
        functools.partial(_final_kernel, n_first=n_first),
        grid=(nt,),
        in_specs=[pl.BlockSpec((tm, d), lambda i: (i, 0))]
        + [pl.BlockSpec((tm, d), functools.partial(lambda i, kk: (kk * nt + i, 0), kk=kk))
           for kk in range(TOP_K)]
        + [pl.BlockSpec((tm, TOP_K), lambda i: (i, 0)),
           pl.BlockSpec((1, d), lambda i: (0, 0))],
        out_specs=[pl.BlockSpec((tm, d), lambda i: (jnp.minimum(i, n_first - 1), 0)),
                   pl.BlockSpec((tm, d), lambda i: (jnp.maximum(i - n_first, 0), 0))],
        out_shape=[jax.ShapeDtypeStruct((m_first, d), F32),
                   jax.ShapeDtypeStruct((m - m_first, d), F32)],
        compiler_params=_cparams(("arbitrary",)),
        name="combine_norm",
    )(y, *([expert_out] * TOP_K), gate, g.reshape(1, d))


def _rel_bias_table(rel_bias, dist):
    n = jnp.maximum(dist, 0)
    max_exact = N_BUCKETS // 2
    nf = jnp.maximum(n, max_exact).astype(F32)
    large = max_exact + (jnp.log(nf / max_exact) / math.log(MAX_DISTANCE / max_exact)
                         * (N_BUCKETS - max_exact)).astype(jnp.int32)
    bucket = jnp.where(n < max_exact, n, jnp.minimum(large, N_BUCKETS - 1))
    return rel_bias[:, bucket].astype(F32)


def _route(logits, n_tok):
    top_v, top_e = lax.top_k(logits, TOP_K)
    gate = jax.nn.softmax(top_v, axis=-1)
    n_exp = logits.shape[1]
    n_assign = n_tok * TOP_K
    flat_e = top_e.reshape(-1)
    order = jnp.argsort(flat_e, stable=True)
    e_sorted = flat_e[order]
    tok_sorted = (order // TOP_K).astype(jnp.int32)
    counts = jnp.zeros((n_exp,), jnp.int32).at[flat_e].add(1)
    padded = (counts + MOE_ROWS - 1) // MOE_ROWS * MOE_ROWS
    pad_end = jnp.cumsum(padded)
    pad_start = pad_end - padded
    start = jnp.cumsum(counts) - counts
    dest = (pad_start[e_sorted] + jnp.arange(n_assign) - start[e_sorted]).astype(jnp.int32)
    n_blocks = -(-n_assign // MOE_ROWS) + n_exp
    row_tok = jnp.zeros((n_blocks * MOE_ROWS,), jnp.int32).at[dest].add(tok_sorted)
    pos = jnp.zeros((n_assign,), jnp.int32).at[order].add(dest).reshape(n_tok, TOP_K)
    nblk_e = padded // MOE_ROWS
    nrun_e = (nblk_e + MOE_RUN - 1) // MOE_RUN
    run_end = jnp.cumsum(nrun_e)
    run_start = run_end - nrun_e
    n_runs = run_end[-1]
    n_steps = n_exp + -(-n_blocks // MOE_RUN)
    steps = jnp.arange(n_steps)
    step = jnp.minimum(steps, n_runs - 1)
    run_e = jnp.sum(run_end[None, :] <= step[:, None], axis=1)
    piece = step - run_start[run_e]
    run_b0 = pad_start[run_e] // MOE_ROWS + piece * MOE_RUN
    run_len = jnp.minimum(MOE_RUN, nblk_e[run_e] - piece * MOE_RUN)
    tail_b0 = jnp.minimum(pad_end[-1] // MOE_ROWS + (steps - n_runs) * MOE_RUN, n_blocks)
    tail_len = jnp.clip(n_blocks - tail_b0, 0, MOE_RUN)
    real = steps < n_runs
    run_b0 = jnp.where(real, run_b0, tail_b0)
    run_len = jnp.where(real, run_len, tail_len)
    runs = (run_e.astype(jnp.int32), run_b0.astype(jnp.int32), run_len.astype(jnp.int32),
            n_runs.astype(jnp.int32).reshape(1))
    return gate, row_tok, runs, pos


def kernel(x_prompt, x_sample, cache_k, cache_v, state_gdn, state_conv, page_table, norm_mix_g, w_in, rel_bias, conv_w, a_log, dt_bias, gdn_norm_g, w_branch_a, w_branch_b, w_out, norm_ffn_g, router_w, router_b, w_gate_up, b_gate_up, w_down, b_down, norm_final_g):
    n_p, seq, d = x_prompt.shape
    n_s, dseq, _ = x_sample.shape
    assert w_in.shape[0] == 1
    l = 0
    wa = H_A * HD_A
    wk = H_B * DK_B
    qkv_b = 3 * wk
    n_prow = n_p * seq
    n_srow = n_s * dseq
    n_tok = n_prow + n_srow
    past = page_table.shape[1] * PAGE_SIZE
    assert past % MOBA_BLOCK == 0 and seq % MOBA_BLOCK == 0

    x = jnp.concatenate([x_prompt.reshape(n_prow, d), x_sample.reshape(n_srow, d)], axis=0)

    w = w_in[l]
    c_z = 3 * wa + qkv_b
    c_ab = c_z + wk
    c_g = c_ab + 2 * H_B
    w_main = jnp.concatenate([w[:, :c_ab], w[:, c_g:]], axis=1).astype(BF16)
    w_ab = jnp.concatenate([w[:, c_ab:c_g], jnp.zeros((d, 128 - 2 * H_B), F32)], axis=1).astype(BF16)
    tm = 768
    proj = norm_matmul(x, norm_mix_g[l], w_main, tm, 1024)
    ab = norm_matmul(x, norm_mix_g[l], w_ab, tm, 128)

    k_all = proj[:, wa:2 * wa]
    v_all = proj[:, 2 * wa:3 * wa]

    d_max = max(past + dseq - 1, seq)
    far_first = _rel_bias_table(rel_bias, d_max - jnp.arange(d_max + MOBA_BLOCK))
    out_a_p = moba_prompt(proj, far_first[:, d_max - seq:], n_p, seq, 0, H_A, 2 * H_A)

    q_s = proj[n_prow:, :wa].reshape(n_s, dseq, H_A, HD_A) * (HD_A ** -0.5)
    qbd = jnp.einsum('blhd,hg->bhlgd', q_s, jnp.eye(H_A, dtype=F32)).reshape(n_s, H_A * dseq, wa)
    lead = d_max - (past + dseq - 1)
    bias_past = jnp.stack([far_first[:, lead + dseq - 1 - t:lead + dseq - 1 - t + past]
                           for t in range(dseq)], axis=1)
    bias_past = bias_past.reshape(H_A * dseq, past)
    bias_own = jnp.stack([far_first[:, d_max - t:d_max - t + dseq] for t in range(dseq)], axis=1)
    bias_own = bias_own.reshape(H_A * dseq, dseq)
    out_a_s = moba_sample(proj, n_prow // dseq, 1, 2, qbd, cache_k, cache_v, l, page_table,
                          bias_past, bias_own, dseq)
    out_a = jnp.concatenate([out_a_p, out_a_s], axis=0)

    qkv_colblk = 3 * wa // qkv_b
    assert qkv_colblk * qkv_b == 3 * wa
    tt = 512
    c0 = 3 * wa
    groups = lax.slice(proj.reshape(n_tok // 8, 8, proj.shape[1]), (tt // 8 - 1, 0, c0),
                       (n_prow // 8, 8, c0 + qkv_b), (tt // 8, 1, 1)).reshape(n_p, seq // tt, 8, qkv_b)
    halo0 = jnp.zeros((n_p, 1, 8, qkv_b), F32)
    halo_p = jnp.concatenate([halo0, groups[:, :-1]], axis=1).reshape(n_p * (seq // tt), 8, qkv_b)
    qp, kp, vp, gbp = gdn_prep(proj, ab, halo_p, conv_w[l], a_log[l], dt_bias[l],
                               0, n_p, seq, tt, qkv_colblk)
    raw_s = proj[n_prow:, 3 * wa:3 * wa + qkv_b].reshape(n_s, dseq, qkv_b)
    halo_s = jnp.concatenate([jnp.zeros((n_s, 8 - (CONV_WIDTH - 1), qkv_b), F32), state_conv[l]], axis=1)
    qs_, ks_, vs_, gbs = gdn_prep(proj, ab, halo_s, conv_w[l], a_log[l], dt_bias[l],
                                  n_prow, n_s, dseq, dseq, qkv_colblk)
    z_colblk = c_z // wk
    assert z_colblk * wk == c_z
    c_p = math.gcd(seq, GDN_CHUNK)
    out_b_p, s_p = gdn_chunks(qp, kp, vp, gbp, proj, z_colblk, 0, gdn_norm_g[l],
                              jnp.zeros((n_p, H_B, DK_B, DK_B), F32), n_p, c_p)
    c_s = math.gcd(dseq, GDN_CHUNK)
    out_b_s, s_s = gdn_chunks(qs_, ks_, vs_, gbs, proj, z_colblk, n_prow, gdn_norm_g[l],
                              state_gdn[l], 2, c_s)
    out_b = jnp.concatenate([out_b_p.reshape(n_prow, wk), out_b_s.reshape(n_srow, wk)], axis=0)
    conv_p = jnp.stack([proj[(n + 1) * seq - (CONV_WIDTH - 1):(n + 1) * seq, c0:c0 + qkv_b]
                        for n in range(n_p)])
    conv_s = jnp.concatenate([state_conv[l], raw_s], axis=1)[:, dseq:, :]

    merged = branch_merge(out_a, out_b, w_branch_a[l].astype(BF16), w_branch_b[l].astype(BF16),
                          proj, c_ab, c_ab + d, tm, 1024)
    y1 = matmul_residual(merged, w_out[l].astype(BF16), x, tm, 1024)

    n_exp = router_w.shape[2]
    w_r = jnp.concatenate([router_w[l], jnp.zeros((d, 128 - n_exp), F32)], axis=1).astype(BF16)
    r_logits, h2 = norm_matmul(y1, norm_ffn_g[l], w_r, tm, 128, emit_h=True)
    logits = r_logits[:, :n_exp] + router_b[l].astype(F32)
    gate, row_tok, runs, pos = _route(logits, n_tok)
    x_rows = h2[lax.optimization_barrier(row_tok)]
    y_rows = moe_experts(x_rows, runs, w_gate_up.reshape(w_gate_up.shape[1:]),
                         b_gate_up.reshape(b_gate_up.shape[1:]), w_down.reshape(w_down.shape[1:]),
                         b_down.reshape(b_down.shape[1:]), 512)
    picked = y_rows[pos.T.reshape(-1)]
    y_p, y_s = combine_residual_norm(y1, picked, gate, norm_final_g, math.gcd(n_prow, n_srow, 256), n_prow)

    y_prompt = y_p.reshape(n_p, seq, d)
    y_sample = y_s.reshape(n_s, dseq, d)
    k_p = k_all[:n_prow].reshape(1, n_p, seq, H_A, HD_A)
    v_p = v_all[:n_prow].reshape(1, n_p, seq, H_A, HD_A)
    k_s = k_all[n_prow:].reshape(1, n_s, dseq, H_A, HD_A)
    v_s = v_all[n_prow:].reshape(1, n_s, dseq, H_A, HD_A)
    return (y_prompt, y_sample, k_p, v_p, s_p[None], conv_p[None],
            k_s, v_s, s_s[None], conv_s[None])
```

```python
import functools
import math

import jax
import jax.numpy as jnp
from jax import lax
from jax.experimental import pallas as pl
from jax.experimental.pallas import tpu as pltpu

F32 = jnp.float32
BF16 = jnp.bfloat16
HIGHEST = lax.Precision.HIGHEST

RMS_EPS = 1e-6
NEG_INF = -1e30

H_A = 8
HD_A = 128
MOBA_BLOCK = 256
MOBA_TOPK = 3
MOBA_GROUP = 4
PAGE_SIZE = 128
N_BUCKETS = 32
MAX_DISTANCE = 4096

H_B = 8
DK_B = 128
CONV_WIDTH = 4
GDN_CHUNK = 64

TOP_K = 4
SWIGLU_LIMIT = 7.0
SWIGLU_ALPHA = 1.702
MOE_ROWS = 256

VMEM_LIMIT = 56 * 1024 * 1024
LANES = 128


def _cparams(sem):
    return pltpu.CompilerParams(dimension_semantics=sem, vmem_limit_bytes=VMEM_LIMIT)


def _dot(a, b, precision=None):
    return jnp.dot(a, b, preferred_element_type=F32, precision=precision)


def _dot_nt(a, b, precision=None):
    return lax.dot_general(a, b, (((1,), (1,)), ((), ())),
                           preferred_element_type=F32, precision=precision)


def _dot_tn(a, b, precision=None):
    return lax.dot_general(a, b, (((0,), (0,)), ((), ())),
                           preferred_element_type=F32, precision=precision)


def _norm_matmul_kernel(x_ref, g_ref, w_ref, o_ref, *rest, emit_h):
    if emit_h:
        h_out_ref, h_ref = rest
    else:
        (h_ref,) = rest

    @pl.when(pl.program_id(1) == 0)
    def _():
        x = x_ref[...]
        y = x * lax.rsqrt(jnp.mean(x * x, axis=-1, keepdims=True) + RMS_EPS)
        h_ref[...] = (y * g_ref[...]).astype(BF16)

    if emit_h:
        half = h_ref.shape[1] // 2
        h_out_ref[...] = _pack_bf16_pair(h_ref[:, :half], h_ref[:, half:])
    o_ref[...] = _dot(h_ref[...], w_ref[...])


def _pack_bf16_pair(hi, lo):
    hi_bits = lax.bitcast_convert_type(hi.astype(F32), jnp.uint32)
    lo_bits = lax.bitcast_convert_type(lo.astype(F32), jnp.uint32)
    return lax.bitcast_convert_type(hi_bits | (lo_bits >> 16), F32)


def _unpack_bf16_pair(words):
    bits = lax.bitcast_convert_type(words, jnp.uint32)
    hi = lax.bitcast_convert_type(bits & jnp.uint32(0xFFFF0000), F32)
    lo = lax.bitcast_convert_type(bits << 16, F32)
    return hi.astype(BF16), lo.astype(BF16)


def norm_matmul(x, g, w, tm, tn, emit_h=False):
    m, k = x.shape
    n = w.shape[1]
    out_shape = [jax.ShapeDtypeStruct((m, n), F32)]
    out_specs = [pl.BlockSpec((tm, tn), lambda i, j: (i, j))]
    if emit_h:
        out_shape.append(jax.ShapeDtypeStruct((m, k // 2), F32))
        out_specs.append(pl.BlockSpec((tm, k // 2), lambda i, j: (i, 0)))
    res = pl.pallas_call(
        functools.partial(_norm_matmul_kernel, emit_h=emit_h),
        grid=(m // tm, n // tn),
        in_specs=[pl.BlockSpec((tm, k), lambda i, j: (i, 0)),
                  pl.BlockSpec((1, k), lambda i, j: (0, 0)),
                  pl.BlockSpec((k, tn), lambda i, j: (0, j))],
        out_specs=out_specs,
        out_shape=out_shape,
        scratch_shapes=[pltpu.VMEM((tm, k), BF16)],
        compiler_params=_cparams(("parallel", "arbitrary")),
        name="norm_matmul",
    )(x, g.reshape(1, k), w)
    return res if emit_h else res[0]


def _moba_prompt_kernel(q_ref, k_ref, v_ref, rv_ref, o_ref,
                        bias_ref, kb_ref, vb_ref, kmean_ref, m_ref, l_ref, acc_ref,
                        *, n_blocks):
    i = pl.program_id(2)
    blk = MOBA_BLOCK
    seq_len = n_blocks * blk

    @pl.when((pl.program_id(1) == 0) & (i == 0))
    def _():
        for m in range(n_blocks):
            st = blk * (n_blocks - 1 - m)
            win = jnp.broadcast_to(rv_ref[0, :, st:st + 2 * blk], (blk, 2 * blk))
            bias_ref[m] = pltpu.roll(win, 0, 1, stride=1, stride_axis=0)[:, blk:]

    @pl.when(i == 0)
    def _():
        k = k_ref[...]
        kb_ref[:, :HD_A] = k.astype(BF16)
        rblk = lax.broadcasted_iota(jnp.int32, (seq_len, HD_A), 0) // blk
        lane = lax.broadcasted_iota(jnp.int32, (seq_len, HD_A), 1)
        kb_ref[:, HD_A:] = jnp.where(lane == rblk, 1.0, 0.0).astype(BF16)
        vb_ref[...] = v_ref[...].astype(BF16)
        kmean_ref[...] = jnp.mean(k.reshape(n_blocks, blk, HD_A), axis=1)

    qs = q_ref[...] * (HD_A ** -0.5)
    qb = qs.astype(BF16)

    st = _dot_nt(kmean_ref[...], qs, precision=HIGHEST)
    row0 = pl.multiple_of(i * blk, blk)
    logits = _dot_nt(qb, kb_ref[pl.ds(row0, blk), :HD_A]) + bias_ref[0]
    rowi = lax.broadcasted_iota(jnp.int32, st.shape, 0)
    st = jnp.where(rowi < i, st, NEG_INF)
    r = lax.broadcasted_iota(jnp.int32, (blk, blk), 0)
    c_ = lax.broadcasted_iota(jnp.int32, (blk, blk), 1)
    logits = jnp.where(c_ <= r, logits, NEG_INF)
    m0 = jnp.max(logits, axis=-1, keepdims=True)
    rank = jnp.zeros(st.shape, F32)
    for c in range(n_blocks - 1):
        sc = st[c:c + 1, :]
        rank = rank + jnp.where((sc > st) | ((sc == st) & (c < rowi)), 1.0, 0.0)
    sel = jnp.where((rank < MOBA_TOPK) & (rowi < i), 1.0, 0.0)
    p = jnp.exp(logits - m0)
    er = lax.broadcasted_iota(jnp.int32, (n_blocks, HD_A), 0)
    ec = lax.broadcasted_iota(jnp.int32, (n_blocks, HD_A), 1)
    selq = _dot_tn(sel, jnp.where(er == ec, 1.0, 0.0))
    m_ref[...] = m0
    l_ref[...] = jnp.sum(p, axis=-1, keepdims=True)
    acc_ref[...] = _dot(p.astype(BF16), vb_ref[pl.ds(row0, blk), :])
    lane = lax.broadcasted_iota(jnp.int32, selq.shape, 1)
    negm = jnp.where((lane < n_blocks) & (selq < 0.5), NEG_INF, 0.0)
    q_aug = jnp.concatenate([qb, negm.astype(BF16)], axis=1)

    grp = MOBA_GROUP

    def group(t, carry):
        j0 = grp * t
        rows = pl.ds(pl.multiple_of(j0 * blk, grp * blk), grp * blk)
        lg = _dot_nt(q_aug, kb_ref[rows, :])
        lg = lg + jnp.concatenate([bias_ref[jnp.maximum(i - j0 - g, 0)] for g in range(grp)], axis=1)
        m_old = m_ref[...]
        m_new = jnp.maximum(m_old, jnp.max(lg, axis=-1, keepdims=True))
        alpha = jnp.exp(m_old - m_new)
        pj = jnp.exp(lg - m_new)
        m_ref[...] = m_new
        l_ref[...] = alpha * l_ref[...] + jnp.sum(pj, axis=-1, keepdims=True)
        acc_ref[...] = alpha * acc_ref[...] + _dot(pj.astype(BF16), vb_ref[rows, :])
        return carry

    lax.fori_loop(0, (i + grp - 1) // grp, group, 0)
    o_ref[...] = (acc_ref[...] / l_ref[...]).astype(o_ref.dtype)


def moba_prompt(proj, rv, n_seq, seq_len, q_col, k_col, v_col):
    blk = MOBA_BLOCK
    nb = seq_len // blk
    assert nb % MOBA_GROUP == 0
    n_dist = (nb + 1) * blk
    rv = rv.reshape(H_A, 1, n_dist)
    return pl.pallas_call(
        functools.partial(_moba_prompt_kernel, n_blocks=nb),
        grid=(H_A, n_seq, nb),
        in_specs=[pl.BlockSpec((blk, HD_A), lambda h, n, i: (n * nb + i, q_col + h)),
                  pl.BlockSpec((seq_len, HD_A), lambda h, n, i: (n, k_col + h)),
                  pl.BlockSpec((seq_len, HD_A), lambda h, n, i: (n, v_col + h)),
                  pl.BlockSpec((1, 1, n_dist), lambda h, n, i: (h, 0, 0))],
        out_specs=pl.BlockSpec((blk, HD_A), lambda h, n, i: (n * nb + i, h)),
        out_shape=jax.ShapeDtypeStruct((n_seq * seq_len, H_A * HD_A), BF16),
        scratch_shapes=[pltpu.VMEM((nb, blk, blk), F32),
                        pltpu.VMEM((seq_len, 2 * HD_A), BF16),
                        pltpu.VMEM((seq_len, HD_A), BF16),
                        pltpu.VMEM((nb, HD_A), F32),
                        pltpu.VMEM((blk, 1), F32),
                        pltpu.VMEM((blk, 1), F32),
                        pltpu.VMEM((blk, HD_A), F32)],
        compiler_params=_cparams(("arbitrary", "arbitrary", "arbitrary")),
        name="moba_prompt",
    )(proj, proj, proj, rv)


PAGES_PER_STEP = 16
BLOCKS_PER_STEP = PAGES_PER_STEP * PAGE_SIZE // MOBA_BLOCK


def _moba_sample_keys_kernel(pt_ref, qbd_ref, knew_ref, bias_ref, bown_ref, *rest,
                             n_steps, n_tok):
    pps = PAGES_PER_STEP
    bps = BLOCKS_PER_STEP
    ppb = pps // bps
    kp = rest[:pps]
    p_ref, pown_ref, lg_ref, ksum_ref = rest[pps:]
    s = pl.program_id(1)
    qbd = qbd_ref[0]
    qbd_b = qbd.astype(BF16)

    def lg_block(b):
        return b // bps, slice((b % bps) * MOBA_BLOCK, (b % bps + 1) * MOBA_BLOCK)

    sums = []
    for k in range(pps):
        hsum = []
        heads = []
        for h in range(H_A):
            kh = kp[k][0, 0, pl.ds(h, PAGE_SIZE, stride=H_A), :]
            heads.append(kh.astype(BF16))
            hsum.append(jnp.sum(kh, axis=0, keepdims=True))
        page = jnp.concatenate(heads, axis=1)
        lg = _dot_nt(qbd_b, page) + bias_ref[:, k * PAGE_SIZE:(k + 1) * PAGE_SIZE]
        lg_ref[s, :, k * PAGE_SIZE:(k + 1) * PAGE_SIZE] = lg
        sums.append(jnp.concatenate(hsum, axis=1))
    blocks = [sum(sums[ppb * b:ppb * (b + 1)]) * (1.0 / MOBA_BLOCK) for b in range(bps)]
    ksum_ref[pl.ds(pl.multiple_of(s * bps, bps), bps), :] = jnp.concatenate(blocks, axis=0)

    @pl.when(s == n_steps - 1)
    def _():
        n_blk = n_steps * bps
        sc = _dot_nt(qbd, ksum_ref[...], precision=HIGHEST)
        bidx = lax.broadcasted_iota(jnp.int32, sc.shape, 1)
        sel = jnp.zeros(sc.shape, F32)
        for _ in range(MOBA_TOPK):
            mx = jnp.max(sc, axis=-1, keepdims=True)
            first = jnp.min(jnp.where(sc == mx, bidx, n_blk), axis=-1, keepdims=True)
            hit = bidx == first
            sel = jnp.where(hit, 1.0, sel)
            sc = jnp.where(hit, 2.0 * NEG_INF, sc)

        lo = _dot_nt(qbd, knew_ref[...]) + bown_ref[...]
        kc = lax.broadcasted_iota(jnp.int32, lo.shape, 1)
        qr = lax.broadcasted_iota(jnp.int32, lo.shape, 0) % n_tok
        lo = jnp.where(kc <= qr, lo, NEG_INF)

        def masked(b):
            st, sl = lg_block(b)
            return jnp.where(sel[:, b:b + 1] > 0.0, lg_ref[st, :, sl], NEG_INF)

        run = masked(0)
        for b in range(1, n_blk):
            run = jnp.maximum(run, masked(b))
        m = jnp.maximum(jnp.max(lo, axis=-1, keepdims=True), jnp.max(run, axis=-1, keepdims=True))
        e_own = jnp.exp(lo - m)
        tot = jnp.zeros(run.shape, F32)
        for b in range(n_blk):
            st, sl = lg_block(b)
            e = jnp.exp(masked(b) - m)
            lg_ref[st, :, sl] = e
            tot = tot + e
        l = jnp.sum(e_own, axis=-1, keepdims=True) + jnp.sum(tot, axis=-1, keepdims=True)
        inv = 1.0 / l
        pown_ref[0] = e_own * inv
        for b in range(n_blk):
            st, sl = lg_block(b)
            p_ref[0, :, b * MOBA_BLOCK:(b + 1) * MOBA_BLOCK] = (lg_ref[st, :, sl] * inv).astype(BF16)


def _moba_sample_values_kernel(pt_ref, p_ref, pown_ref, vnew_ref, *rest, n_steps, n_tok):
    pps = PAGES_PER_STEP
    vp = rest[:pps]
    o_ref, acc_ref = rest[pps:]
    s = pl.program_id(1)

    @pl.when(s == 0)
    def _():
        acc_ref[...] = _dot(pown_ref[0], vnew_ref[...])

    acc = acc_ref[...]
    for k in range(pps):
        page = jnp.concatenate(
            [vp[k][0, 0, pl.ds(h, PAGE_SIZE, stride=H_A), :].astype(BF16) for h in range(H_A)], axis=1)
        acc = acc + _dot(p_ref[0, :, k * PAGE_SIZE:(k + 1) * PAGE_SIZE], page)
    acc_ref[...] = acc

    @pl.when(s == n_steps - 1)
    def _():
        for h in range(H_A):
            o_ref[:, h * HD_A:(h + 1) * HD_A] = acc[h * n_tok:(h + 1) * n_tok,
                                                    h * HD_A:(h + 1) * HD_A].astype(o_ref.dtype)


def moba_sample(proj, row_blk0, k_colblk, v_colblk, qbd, cache_k, cache_v, layer, page_table,
                bias_past, bias_own, n_tok):
    n_seq, n_pages = page_table.shape
    pps = PAGES_PER_STEP
    n_steps = n_pages // pps
    n_past = n_pages * PAGE_SIZE
    wa = H_A * HD_A
    nrow = H_A * n_tok
    page_block = (1, 1, PAGE_SIZE * H_A, HD_A)
    cache_k = cache_k.reshape(cache_k.shape[:2] + (PAGE_SIZE * H_A, HD_A))
    cache_v = cache_v.reshape(cache_v.shape[:2] + (PAGE_SIZE * H_A, HD_A))

    def pmap(k):
        return lambda b, s, pt: (layer, pt[b, s * pps + k], 0, 0)

    p, p_own = pl.pallas_call(
        functools.partial(_moba_sample_keys_kernel, n_steps=n_steps, n_tok=n_tok),
        grid_spec=pltpu.PrefetchScalarGridSpec(
            num_scalar_prefetch=1,
            grid=(n_seq, n_steps),
            in_specs=[pl.BlockSpec((1, nrow, wa), lambda b, s, pt: (b, 0, 0)),
                      pl.BlockSpec((n_tok, wa), lambda b, s, pt: (row_blk0 + b, k_colblk)),
                      pl.BlockSpec((nrow, pps * PAGE_SIZE), lambda b, s, pt: (0, s)),
                      pl.BlockSpec((nrow, n_tok), lambda b, s, pt: (0, 0))]
            + [pl.BlockSpec(page_block, pmap(k)) for k in range(pps)],
            out_specs=[pl.BlockSpec((1, nrow, n_past), lambda b, s, pt: (b, 0, 0)),
                       pl.BlockSpec((1, nrow, n_tok), lambda b, s, pt: (b, 0, 0))],
            scratch_shapes=[pltpu.VMEM((n_steps, nrow, pps * PAGE_SIZE), F32),
                            pltpu.VMEM((n_past // MOBA_BLOCK, wa), F32)]),
        out_shape=[jax.ShapeDtypeStruct((n_seq, nrow, n_past), BF16),
                   jax.ShapeDtypeStruct((n_seq, nrow, n_tok), F32)],
        compiler_params=_cparams(("arbitrary", "arbitrary")),
        name="moba_sample_keys",
    )(page_table, qbd, proj, bias_past, bias_own, *([cache_k] * pps))

    return pl.pallas_call(
        functools.partial(_moba_sample_values_kernel, n_steps=n_steps, n_tok=n_tok),
        grid_spec=pltpu.PrefetchScalarGridSpec(
            num_scalar_prefetch=1,
            grid=(n_seq, n_steps),
            in_specs=[pl.BlockSpec((1, nrow, pps * PAGE_SIZE), lambda b, s, pt: (b, 0, s)),
                      pl.BlockSpec((1, nrow, n_tok), lambda b, s, pt: (b, 0, 0)),
                      pl.BlockSpec((n_tok, wa), lambda b, s, pt: (row_blk0 + b, v_colblk))]
            + [pl.BlockSpec(page_block, pmap(k)) for k in range(pps)],
            out_specs=pl.BlockSpec((n_tok, wa), lambda b, s, pt: (b, 0)),
            scratch_shapes=[pltpu.VMEM((nrow, wa), F32)]),
        out_shape=jax.ShapeDtypeStruct((n_seq * n_tok, wa), BF16),
        compiler_params=_cparams(("arbitrary", "arbitrary")),
        name="moba_sample_values",
    )(page_table, p, p_own, proj, *([cache_v] * pps))


def _gdn_prep_kernel(x_ref, halo_ref, ab_ref, cw_ref, alog_ref, dtb_ref,
                     q_ref, k_ref, v_ref, gb_ref, *, tt):
    x = x_ref[...]
    xf = jnp.concatenate([halo_ref[0], x], axis=0)
    cw = cw_ref[...]
    conv = xf[5:5 + tt] * cw[0:1]
    for w in range(1, CONV_WIDTH):
        conv = conv + xf[5 + w:5 + w + tt] * cw[w:w + 1]
    act = conv * jax.nn.sigmoid(conv)
    wk = H_B * DK_B
    for h in range(H_B):
        q = act[:, h * DK_B:(h + 1) * DK_B]
        k = act[:, wk + h * DK_B:wk + (h + 1) * DK_B]
        q_ref[0, h] = q * lax.rsqrt(jnp.sum(q * q, axis=-1, keepdims=True) + RMS_EPS) * (DK_B ** -0.5)
        k_ref[0, h] = k * lax.rsqrt(jnp.sum(k * k, axis=-1, keepdims=True) + RMS_EPS)
        v_ref[0, h] = act[:, 2 * wk + h * DK_B:2 * wk + (h + 1) * DK_B]
    ab = ab_ref[...]
    t = ab + dtb_ref[...]
    sp = jnp.maximum(t, 0.0) + jnp.log(1.0 + jnp.exp(-jnp.abs(t)))
    g = -jnp.exp(alog_ref[...]) * sp
    lane = lax.broadcasted_iota(jnp.int32, ab.shape, 1)
    gb_ref[0] = jnp.where(lane < H_B, g, jax.nn.sigmoid(ab))


def gdn_prep(proj, ab, halo, conv_w, a_log, dt_bias, row0, n_seq, seq_len, tt, qkv_colblk):
    w3 = conv_w.shape[1]
    nt = seq_len // tt
    rb0 = row0 // tt
    pad = jnp.zeros((1, LANES - H_B), F32)
    alog = jnp.concatenate([a_log.reshape(1, H_B), pad], axis=1)
    dtb = jnp.concatenate([dt_bias.reshape(1, H_B), pad], axis=1)
    hm = jax.ShapeDtypeStruct((n_seq, H_B, seq_len, DK_B), F32)
    hspec = pl.BlockSpec((1, H_B, tt, DK_B), lambda n, t: (n, 0, t, 0))
    return pl.pallas_call(
        functools.partial(_gdn_prep_kernel, tt=tt),
        grid=(n_seq, nt),
        in_specs=[pl.BlockSpec((tt, w3), lambda n, t: (rb0 + n * nt + t, qkv_colblk)),
                  pl.BlockSpec((1, 8, w3), lambda n, t: (n * nt + t, 0, 0)),
                  pl.BlockSpec((tt, LANES), lambda n, t: (rb0 + n * nt + t, 0)),
                  pl.BlockSpec((CONV_WIDTH, w3), lambda n, t: (0, 0)),
                  pl.BlockSpec((1, LANES), lambda n, t: (0, 0)),
                  pl.BlockSpec((1, LANES), lambda n, t: (0, 0))],
        out_specs=[hspec, hspec, hspec,
                   pl.BlockSpec((1, tt, LANES), lambda n, t: (n, t, 0))],
        out_shape=[hm, hm, hm, jax.ShapeDtypeStruct((n_seq, seq_len, LANES), F32)],
        compiler_params=_cparams(("parallel", "parallel")),
        name="gdn_prep",
    )(proj, halo, ab, conv_w, alog, dtb)


def _split2(a):
    hi = a.astype(BF16)
    return hi, (a - hi.astype(F32)).astype(BF16)


def _split3(a):
    hi = a.astype(BF16)
    r1 = a - hi.astype(F32)
    mid = r1.astype(BF16)
    return hi, mid, (r1 - mid.astype(F32)).astype(BF16)


def _dot3(a, b):
    ah, al = _split2(a)
    bh, bl = _split2(b)
    return _dot(ah, bh) + (_dot(ah, bl) + _dot(al, bh))


def _dot_exact_lhs(a_bf16, b):
    b1, b2, b3 = _split3(b)
    return _dot(a_bf16, b1) + (_dot(a_bf16, b2) + _dot(a_bf16, b3))


def _unit_lower_inverse(a_list, c):
    r = lax.broadcasted_iota(jnp.int32, (c, c), 0)
    cc = lax.broadcasted_iota(jnp.int32, (c, c), 1)
    eye = jnp.where(r == cc, 1.0, 0.0)
    base = min(c, 8)
    n1 = [jnp.where((r // base) == (cc // base), -a, 0.0) for a in a_list]
    n2 = [_dot3(x, x) for x in n1]
    n4 = [_dot3(x, x) for x in n2]
    t = [_dot3(eye + x, eye + y) for x, y in zip(n1, n2)]
    t = [_dot3(x, eye + y) for x, y in zip(t, n4)]
    b = base
    while b < c:
        inner = ((r // (2 * b)) == (cc // (2 * b))) & ((r // b) != (cc // b))
        left = [_dot3(x, jnp.where(inner, a, 0.0)) for x, a in zip(t, a_list)]
        t = [x - _dot3(y, x) for x, y in zip(t, left)]
        b *= 2
    return t


def _gdn_chunk_kernel(*refs, nb, c):
    q_ref, k_ref, v_ref, gb_ref = refs[:4]
    z_refs = refs[4:4 + nb]
    ng_ref, s0_ref, o_ref, sout_ref, s_ref = refs[4 + nb:]
    ci = pl.program_id(1)

    @pl.when(ci == 0)
    def _():
        s_ref[...] = s0_ref[...]

    r = lax.broadcasted_iota(jnp.int32, (c, c), 0)
    cc = lax.broadcasted_iota(jnp.int32, (c, c), 1)
    incl = r >= cc
    strict = r > cc
    ltri = jnp.where(incl, 1.0, 0.0).astype(BF16)
    eye = jnp.where(r == cc, 1.0, 0.0)
    ones = jnp.ones((c, c), BF16)
    ng = ng_ref[...]
    ch = [(n, h) for n in range(nb) for h in range(H_B)]
    gbv = [gb_ref[n] for n in range(nb)]
    q = [q_ref[n, h] for n, h in ch]
    k = [k_ref[n, h] for n, h in ch]
    v = [v_ref[n, h] for n, h in ch]
    gcol = [jnp.broadcast_to(gbv[n][:, h:h + 1], (c, DK_B)) for n, h in ch]
    bcol = [jnp.broadcast_to(gbv[n][:, H_B + h:H_B + h + 1], (c, DK_B)) for n, h in ch]
    big_g = [_dot_exact_lhs(ltri, g) for g in gcol]
    gj = [_dot_exact_lhs(ones, g[:, :c] * eye) for g in big_g]
    decay = [jnp.where(incl, jnp.exp(jnp.minimum(g[:, :c] - x, 0.0)), 0.0) for g, x in zip(big_g, gj)]
    kk = [_dot_nt(x, x) for x in k]
    a = [jnp.where(strict, b[:, :c] * x * d, 0.0) for b, x, d in zip(bcol, kk, decay)]
    t = _unit_lower_inverse(a, c)
    gam = [jnp.exp(g) for g in big_g]
    u = [_dot3(x, b * y) for x, b, y in zip(t, bcol, v)]
    w = [_dot3(x, b * g * y) for x, b, g, y in zip(t, bcol, gam, k)]
    qk = [_dot_nt(x, y) * d for x, y, d in zip(q, k, decay)]
    g_last = [g[c - 1:c, :] for g in big_g]
    k_tail = [jnp.exp(gl - g) * y for gl, g, y in zip(g_last, big_g, k)]
    s = [s_ref[n, h] for n, h in ch]
    uu = [x - _dot(y, z_) for x, y, z_ in zip(u, w, s)]
    o = [_dot(g * x, z_) + _dot(y, x2) for g, x, z_, y, x2 in zip(gam, q, s, qk, uu)]
    s_new = [jnp.exp(gl) * z_ + _dot_tn(y, x) for gl, z_, y, x in zip(g_last, s, k_tail, uu)]
    for (n, h), x in zip(ch, s_new):
        s_ref[n, h] = x
    for (n, h), x in zip(ch, o):
        x = x * lax.rsqrt(jnp.mean(x * x, axis=-1, keepdims=True) + RMS_EPS) * ng
        zh = z_refs[n][:, h * DK_B:(h + 1) * DK_B]
        o_ref[n, :, h * DK_B:(h + 1) * DK_B] = (x * (zh * jax.nn.sigmoid(zh))).astype(o_ref.dtype)

    @pl.when(ci == pl.num_programs(1) - 1)
    def _():
        sout_ref[...] = s_ref[...]


def gdn_chunks(q, k, v, gb, proj, z_colblk, row0, norm_g, s0, nb, c):
    n_seq, _, seq_len, _ = q.shape
    nc = seq_len // c
    wv = H_B * DK_B
    rb0 = row0 // c
    hspec = pl.BlockSpec((nb, H_B, c, DK_B), lambda n, ci: (n, 0, ci, 0))
    sspec = pl.BlockSpec((nb, H_B, DK_B, DK_B), lambda n, ci: (n, 0, 0, 0))

    def zmap(j):
        return lambda n, ci: (rb0 + (n * nb + j) * nc + ci, z_colblk)

    return pl.pallas_call(
        functools.partial(_gdn_chunk_kernel, nb=nb, c=c),
        grid=(n_seq // nb, nc),
        in_specs=[hspec, hspec, hspec,
                  pl.BlockSpec((nb, c, LANES), lambda n, ci: (n, ci, 0))]
        + [pl.BlockSpec((c, wv), zmap(j)) for j in range(nb)]
        + [pl.BlockSpec((1, DK_B), lambda n, ci: (0, 0)), sspec],
        out_specs=[pl.BlockSpec((nb, c, wv), lambda n, ci: (n, ci, 0)), sspec],
        out_shape=[jax.ShapeDtypeStruct((n_seq, seq_len, wv), BF16),
                   jax.ShapeDtypeStruct((n_seq, H_B, DK_B, DK_B), F32)],
        scratch_shapes=[pltpu.VMEM((nb, H_B, DK_B, DK_B), F32)],
        compiler_params=_cparams(("parallel", "arbitrary")),
        name="gdn_chunks",
    )(q, k, v, gb, *([proj] * nb), norm_g.reshape(1, DK_B), s0)


def _merge_kernel(oa_ref, ob_ref, wa_ref, wb_ref, ga_ref, gb_ref, o_ref):
    ya = _dot(oa_ref[...], wa_ref[...])
    yb = _dot(ob_ref[...], wb_ref[...])
    o_ref[...] = (jax.nn.sigmoid(ga_ref[...]) * ya
                  + jax.nn.sigmoid(gb_ref[...]) * yb).astype(o_ref.dtype)


def branch_merge(out_a, out_b, wa, wb, proj, ga_col0, gb_col0, tm, tn):
    m, ka = out_a.shape
    n = wa.shape[1]
    ga_blk, gb_blk = ga_col0 // tn, gb_col0 // tn
    assert ga_blk * tn == ga_col0 and gb_blk * tn == gb_col0
    return pl.pallas_call(
        _merge_kernel,
        grid=(m // tm, n // tn),
        in_specs=[pl.BlockSpec((tm, ka), lambda i, j: (i, 0)),
                  pl.BlockSpec((tm, ka), lambda i, j: (i, 0)),
                  pl.BlockSpec((ka, tn), lambda i, j: (0, j)),
                  pl.BlockSpec((ka, tn), lambda i, j: (0, j)),
                  pl.BlockSpec((tm, tn), lambda i, j: (i, ga_blk + j)),
                  pl.BlockSpec((tm, tn), lambda i, j: (i, gb_blk + j))],
        out_specs=pl.BlockSpec((tm, tn), lambda i, j: (i, j)),
        out_shape=jax.ShapeDtypeStruct((m, n), BF16),
        compiler_params=_cparams(("parallel", "parallel")),
        name="branch_merge",
    )(out_a, out_b, wa, wb, proj, proj)


def _matmul_residual_kernel(x_ref, w_ref, r_ref, o_ref):
    o_ref[...] = r_ref[...] + _dot(x_ref[...], w_ref[...])


def matmul_residual(x, w, res, tm, tn):
    m, k = x.shape
    n = w.shape[1]
    return pl.pallas_call(
        _matmul_residual_kernel,
        grid=(m // tm, n // tn),
        in_specs=[pl.BlockSpec((tm, k), lambda i, j: (i, 0)),
                  pl.BlockSpec((k, tn), lambda i, j: (0, j)),
                  pl.BlockSpec((tm, tn), lambda i, j: (i, j))],
        out_specs=pl.BlockSpec((tm, tn), lambda i, j: (i, j)),
        out_shape=jax.ShapeDtypeStruct((m, n), F32),
        compiler_params=_cparams(("parallel", "parallel")),
        name="out_proj",
    )(x, w, res)


MOE_TA = 256


def _lane_rotate(x, shift):
    parts = [pltpu.roll(x[:, c:c + LANES], shift, 1) for c in range(0, x.shape[1], LANES)]
    return parts[0] if len(parts) == 1 else jnp.concatenate(parts, axis=1)


MOE_RUN = 8


def _block_out_copy(buf_ref, out_ref, sem_ref, j, first_block, col_tile):
    rows, cols = buf_ref.shape[1:]
    dst = out_ref.at[pl.ds(pl.multiple_of((first_block + j) * rows, rows), rows),
                     pl.ds(pl.multiple_of(col_tile * cols, cols), cols)]
    return pltpu.make_async_copy(buf_ref.at[j], dst, sem_ref.at[j])


def _zero_fill(buf_ref, out_ref, sem_ref, n_fill, first_block, col_tile):
    for j in range(buf_ref.shape[0]):
        @pl.when(j < n_fill)
        def _(j=j):
            buf_ref[j] = jnp.zeros(buf_ref.shape[1:], buf_ref.dtype)
            _block_out_copy(buf_ref, out_ref, sem_ref, j, first_block, col_tile).start()
    for j in range(buf_ref.shape[0]):
        @pl.when(j < n_fill)
        def _(j=j):
            _block_out_copy(buf_ref, out_ref, sem_ref, j, first_block, col_tile).wait()


def _moe_up_kernel(re_ref, rb_ref, rl_ref, nr_ref, *refs):
    r = MOE_RUN
    x_refs = refs[:r]
    wa_ref, wb_ref, ba_ref, bb_ref, out_ref, wab_ref, wbb_ref, buf_ref, sem_ref = refs[r:]
    v = pl.program_id(0)
    n = pl.program_id(1)

    @pl.when(v >= nr_ref[0])
    def _():
        _zero_fill(buf_ref, out_ref, sem_ref, rl_ref[v], rb_ref[v], n)

    @pl.when(v < nr_ref[0])
    def _():
        wab_ref[...] = wa_ref[0].astype(BF16)
        wbb_ref[...] = wb_ref[0].astype(BF16)
        n_live = rl_ref[v]
        for j in range(r):
            @pl.when(j < n_live)
            def _(j=j):
                half = x_refs[j].shape[1]
                x_hi, x_lo = _unpack_bf16_pair(x_refs[j][...])
                ga = (_dot(x_hi, wab_ref[:half, :]) + _dot(x_lo, wab_ref[half:, :])
                      + ba_ref[0])
                gb = _dot(x_hi, wbb_ref[:half, :]) + _dot(x_lo, wbb_ref[half:, :]) + bb_ref[0]
                even = (lax.broadcasted_iota(jnp.int32, ga.shape, 1) % 2) == 0
                gate = jnp.where(even, ga, _lane_rotate(gb, 1))
                up = jnp.where(even, _lane_rotate(ga, LANES - 1), gb)
                gl = jnp.minimum(gate, SWIGLU_LIMIT)
                up = jnp.clip(up, -SWIGLU_LIMIT, SWIGLU_LIMIT)
                buf_ref[j] = (gl * jax.nn.sigmoid(SWIGLU_ALPHA * gl) * (up + 1.0)).astype(BF16)
                _block_out_copy(buf_ref, out_ref, sem_ref, j, rb_ref[v], n).start()
        for j in range(r):
            @pl.when(j < n_live)
            def _(j=j):
                _block_out_copy(buf_ref, out_ref, sem_ref, j, rb_ref[v], n).wait()


def _bf16_bits(w):
    return lax.bitcast_convert_type(w.astype(BF16).astype(F32), jnp.uint32)


def _moe_down_kernel(re_ref, rb_ref, rl_ref, nr_ref, *refs):
    r = MOE_RUN
    a_refs = refs[:r]
    wd_ref, bd_ref, out_ref, wdb_ref, buf_ref, sem_ref = refs[r:]
    v = pl.program_id(0)
    n = pl.program_id(1)

    @pl.when(v >= nr_ref[0])
    def _():
        _zero_fill(buf_ref, out_ref, sem_ref, rl_ref[v], rb_ref[v], n)

    @pl.when(v < nr_ref[0])
    def _():
        f = wd_ref.shape[1]
        for t in range(f // (2 * MOE_TA)):
            wa = wd_ref[0, t * MOE_TA:(t + 1) * MOE_TA, :]
            wb = wd_ref[0, f // 2 + t * MOE_TA:f // 2 + (t + 1) * MOE_TA, :]
            pair = (_bf16_bits(wa) >> 16) | _bf16_bits(wb)
            wdb_ref[t * 2 * MOE_TA:(t + 1) * 2 * MOE_TA, :] = pltpu.bitcast(pair, BF16)
        n_live = rl_ref[v]
        for j in range(r):
            @pl.when(j < n_live)
            def _(j=j):
                buf_ref[j] = _dot(a_refs[j][...], wdb_ref[...]) + bd_ref[0]
                _block_out_copy(buf_ref, out_ref, sem_ref, j, rb_ref[v], n).start()
        for j in range(r):
            @pl.when(j < n_live)
            def _(j=j):
                _block_out_copy(buf_ref, out_ref, sem_ref, j, rb_ref[v], n).wait()


def moe_experts(x_rows, runs, w_gate_up, b_gate_up, w_down, b_down, tn_down):
    rows = x_rows.shape[0]
    n_exp, d, f2 = w_gate_up.shape
    f = f2 // 2
    dm = w_down.shape[2]
    n_steps = runs[0].shape[0]
    r = MOE_RUN

    def rowmap(j):
        return lambda v, n, re, rb, rl, nr: (rb[v] + jnp.minimum(j, rl[v] - 1), 0)

    def wmap(off, n_tiles):
        return lambda v, n, re, rb, rl, nr: (re[v], 0, off + jnp.where(v < nr[0], n, n_tiles - 1))

    nt = f // (2 * MOE_TA)
    tw = 2 * MOE_TA
    bgu = b_gate_up.reshape(n_exp, 1, f2)
    act = pl.pallas_call(
        _moe_up_kernel,
        grid_spec=pltpu.PrefetchScalarGridSpec(
            num_scalar_prefetch=4,
            grid=(n_steps, nt),
            in_specs=[pl.BlockSpec((MOE_ROWS, d // 2), rowmap(j)) for j in range(r)]
            + [pl.BlockSpec((1, d, tw), wmap(0, nt)),
               pl.BlockSpec((1, d, tw), wmap(nt, nt)),
               pl.BlockSpec((1, 1, tw), wmap(0, nt)),
               pl.BlockSpec((1, 1, tw), wmap(nt, nt))],
            out_specs=pl.BlockSpec(memory_space=pl.ANY),
            scratch_shapes=[pltpu.VMEM((d, tw), BF16), pltpu.VMEM((d, tw), BF16),
                            pltpu.VMEM((r, MOE_ROWS, tw), BF16), pltpu.SemaphoreType.DMA((r,))]),
        out_shape=jax.ShapeDtypeStruct((rows, f), BF16),
        compiler_params=_cparams(("arbitrary", "arbitrary")),
        name="moe_up",
    )(*runs, *([x_rows] * r), w_gate_up, w_gate_up, bgu, bgu)
    nt_d = dm // tn_down
    return pl.pallas_call(
        _moe_down_kernel,
        grid_spec=pltpu.PrefetchScalarGridSpec(
            num_scalar_prefetch=4,
            grid=(n_steps, nt_d),
            in_specs=[pl.BlockSpec((MOE_ROWS, f), rowmap(j)) for j in range(r)]
            + [pl.BlockSpec((1, f, tn_down), wmap(0, nt_d)),
               pl.BlockSpec((1, 1, tn_down), wmap(0, nt_d))],
            out_specs=pl.BlockSpec(memory_space=pl.ANY),
            scratch_shapes=[pltpu.VMEM((f, tn_down), BF16),
                            pltpu.VMEM((r, MOE_ROWS, tn_down), F32), pltpu.SemaphoreType.DMA((r,))]),
        out_shape=jax.ShapeDtypeStruct((rows, dm), F32),
        compiler_params=_cparams(("arbitrary", "arbitrary")),
        name="moe_down",
    )(*runs, *([act] * r), w_down, b_down.reshape(n_exp, 1, dm))


def _final_kernel(y_ref, *rest, n_first):
    e_refs = rest[:TOP_K]
    gate_ref, g_ref, o1_ref, o2_ref = rest[TOP_K:]
    x = y_ref[...]
    for kk in range(TOP_K):
        x = x + e_refs[kk][...] * gate_ref[:, kk:kk + 1]
    out = x * lax.rsqrt(jnp.mean(x * x, axis=-1, keepdims=True) + RMS_EPS) * g_ref[...]
    i = pl.program_id(0)

    @pl.when(i < n_first)
    def _():
        o1_ref[...] = out

    @pl.when(i >= n_first)
    def _():
        o2_ref[...] = out


def combine_residual_norm(y, expert_out, gate, g, tm, m_first):
    m, d = y.shape
    nt = m // tm
    n_first = m_first // tm
    assert nt * tm == m and n_first * tm == m_first and 0 < n_first < nt
    return pl.pallas_call(
        functools.partial(_final_kernel, n_first=n_first),
        grid=(nt,),
        in_specs=[pl.BlockSpec((tm, d), lambda i: (i, 0))]
        + [pl.BlockSpec((tm, d), functools.partial(lambda i, kk: (kk * nt + i, 0), kk=kk))
           for kk in range(TOP_K)]
        + [pl.BlockSpec((tm, TOP_K), lambda i: (i, 0)),
           pl.BlockSpec((1, d), lambda i: (0, 0))],
        out_specs=[pl.BlockSpec((tm, d), lambda i: (jnp.minimum(i, n_first - 1), 0)),
                   pl.BlockSpec((tm, d), lambda i: (jnp.maximum(i - n_first, 0), 0))],
        out_shape=[jax.ShapeDtypeStruct((m_first, d), F32),
                   jax.ShapeDtypeStruct((m - m_first, d), F32)],
        compiler_params=_cparams(("arbitrary",)),
        name="combine_norm",
    )(y, *([expert_out] * TOP_K), gate, g.reshape(1, d))


def _rel_bias_table(rel_bias, dist):
    n = jnp.maximum(dist, 0)
    max_exact = N_BUCKETS // 2
    nf = jnp.maximum(n, max_exact).astype(F32)
    large = max_exact + (jnp.log(nf / max_exact) / math.log(MAX_DISTANCE / max_exact)
                         * (N_BUCKETS - max_exact)).astype(jnp.int32)
    bucket = jnp.where(n < max_exact, n, jnp.minimum(large, N_BUCKETS - 1))
    return rel_bias[:, bucket].astype(F32)


def _route(logits, n_tok):
    top_v, top_e = lax.top_k(logits, TOP_K)
    gate = jax.nn.softmax(top_v, axis=-1)
    n_exp = logits.shape[1]
    n_assign = n_tok * TOP_K
    flat_e = top_e.reshape(-1)
    order = jnp.argsort(flat_e, stable=True)
    e_sorted = flat_e[order]
    tok_sorted = (order // TOP_K).astype(jnp.int32)
    counts = jnp.zeros((n_exp,), jnp.int32).at[flat_e].add(1)
    padded = (counts + MOE_ROWS - 1) // MOE_ROWS * MOE_ROWS
    pad_end = jnp.cumsum(padded)
    pad_start = pad_end - padded
    start = jnp.cumsum(counts) - counts
    dest = (pad_start[e_sorted] + jnp.arange(n_assign) - start[e_sorted]).astype(jnp.int32)
    n_blocks = -(-n_assign // MOE_ROWS) + n_exp
    row_tok = jnp.zeros((n_blocks * MOE_ROWS,), jnp.int32).at[dest].add(tok_sorted)
    pos = jnp.zeros((n_assign,), jnp.int32).at[order].add(dest).reshape(n_tok, TOP_K)
    nblk_e = padded // MOE_ROWS
    nrun_e = (nblk_e + MOE_RUN - 1) // MOE_RUN
    run_end = jnp.cumsum(nrun_e)
    run_start = run_end - nrun_e
    n_runs = run_end[-1]
    n_steps = n_exp + -(-n_blocks // MOE_RUN)
    steps = jnp.arange(n_steps)
    step = jnp.minimum(steps, n_runs - 1)
    run_e = jnp.sum(run_end[None, :] <= step[:, None], axis=1)
    piece = step - run_start[run_e]
    run_b0 = pad_start[run_e] // MOE_ROWS + piece * MOE_RUN
    run_len = jnp.minimum(MOE_RUN, nblk_e[run_e] - piece * MOE_RUN)
    tail_b0 = jnp.minimum(pad_end[-1] // MOE_ROWS + (steps - n_runs) * MOE_RUN, n_blocks)
    tail_len = jnp.clip(n_blocks - tail_b0, 0, MOE_RUN)
    real = steps < n_runs
    run_b0 = jnp.where(real, run_b0, tail_b0)
    run_len = jnp.where(real, run_len, tail_len)
    runs = (run_e.astype(jnp.int32), run_b0.astype(jnp.int32), run_len.astype(jnp.int32),
            n_runs.astype(jnp.int32).reshape(1))
    return gate, row_tok, runs, pos


def kernel(x_prompt, x_sample, cache_k, cache_v, state_gdn, state_conv, page_table, norm_mix_g, w_in, rel_bias, conv_w, a_log, dt_bias, gdn_norm_g, w_branch_a, w_branch_b, w_out, norm_ffn_g, router_w, router_b, w_gate_up, b_gate_up, w_down, b_down, norm_final_g):
    n_p, seq, d = x_prompt.shape
    n_s, dseq, _ = x_sample.shape
    assert w_in.shape[0] == 1
    l = 0
    wa = H_A * HD_A
    wk = H_B * DK_B
    qkv_b = 3 * wk
    n_prow = n_p * seq
    n_srow = n_s * dseq
    n_tok = n_prow + n_srow
    past = page_table.shape[1] * PAGE_SIZE
    assert past % MOBA_BLOCK == 0 and seq % MOBA_BLOCK == 0

    x = jnp.concatenate([x_prompt.reshape(n_prow, d), x_sample.reshape(n_srow, d)], axis=0)

    w = w_in[l]
    c_z = 3 * wa + qkv_b
    c_ab = c_z + wk
    c_g = c_ab + 2 * H_B
    w_main = jnp.concatenate([w[:, :c_ab], w[:, c_g:]], axis=1).astype(BF16)
    w_ab = jnp.concatenate([w[:, c_ab:c_g], jnp.zeros((d, LANES - 2 * H_B), F32)], axis=1).astype(BF16)
    tm = 768
    proj = norm_matmul(x, norm_mix_g[l], w_main, tm, 1024)
    ab = norm_matmul(x, norm_mix_g[l], w_ab, tm, LANES)

    k_all = proj[:, wa:2 * wa]
    v_all = proj[:, 2 * wa:3 * wa]

    d_max = max(past + dseq - 1, seq)
    far_first = _rel_bias_table(rel_bias, d_max - jnp.arange(d_max + MOBA_BLOCK))
    out_a_p = moba_prompt(proj, far_first[:, d_max - seq:], n_p, seq, 0, H_A, 2 * H_A)

    q_s = proj[n_prow:, :wa].reshape(n_s, dseq, H_A, HD_A) * (HD_A ** -0.5)
    qbd = jnp.einsum('blhd,hg->bhlgd', q_s, jnp.eye(H_A, dtype=F32)).reshape(n_s, H_A * dseq, wa)
    lead = d_max - (past + dseq - 1)
    bias_past = jnp.stack([far_first[:, lead + dseq - 1 - t:lead + dseq - 1 - t + past]
                           for t in range(dseq)], axis=1)
    bias_past = bias_past.reshape(H_A * dseq, past)
    bias_own = jnp.stack([far_first[:, d_max - t:d_max - t + dseq] for t in range(dseq)], axis=1)
    bias_own = bias_own.reshape(H_A * dseq, dseq)
    out_a_s = moba_sample(proj, n_prow // dseq, 1, 2, qbd, cache_k, cache_v, l, page_table,
                          bias_past, bias_own, dseq)
    out_a = jnp.concatenate([out_a_p, out_a_s], axis=0)

    qkv_colblk = 3 * wa // qkv_b
    assert qkv_colblk * qkv_b == 3 * wa
    tt = 512
    c0 = 3 * wa
    groups = lax.slice(proj.reshape(n_tok // 8, 8, proj.shape[1]), (tt // 8 - 1, 0, c0),
                       (n_prow // 8, 8, c0 + qkv_b), (tt // 8, 1, 1)).reshape(n_p, seq // tt, 8, qkv_b)
    halo0 = jnp.zeros((n_p, 1, 8, qkv_b), F32)
    halo_p = jnp.concatenate([halo0, groups[:, :-1]], axis=1).reshape(n_p * (seq // tt), 8, qkv_b)
    qp, kp, vp, gbp = gdn_prep(proj, ab, halo_p, conv_w[l], a_log[l], dt_bias[l],
                               0, n_p, seq, tt, qkv_colblk)
    raw_s = proj[n_prow:, 3 * wa:3 * wa + qkv_b].reshape(n_s, dseq, qkv_b)
    halo_s = jnp.concatenate([jnp.zeros((n_s, 8 - (CONV_WIDTH - 1), qkv_b), F32), state_conv[l]], axis=1)
    qs_, ks_, vs_, gbs = gdn_prep(proj, ab, halo_s, conv_w[l], a_log[l], dt_bias[l],
                                  n_prow, n_s, dseq, dseq, qkv_colblk)
    z_colblk = c_z // wk
    assert z_colblk * wk == c_z
    c_p = math.gcd(seq, GDN_CHUNK)
    out_b_p, s_p = gdn_chunks(qp, kp, vp, gbp, proj, z_colblk, 0, gdn_norm_g[l],
                              jnp.zeros((n_p, H_B, DK_B, DK_B), F32), n_p, c_p)
    c_s = math.gcd(dseq, GDN_CHUNK)
    out_b_s, s_s = gdn_chunks(qs_, ks_, vs_, gbs, proj, z_colblk, n_prow, gdn_norm_g[l],
                              state_gdn[l], 2, c_s)
    out_b = jnp.concatenate([out_b_p.reshape(n_prow, wk), out_b_s.reshape(n_srow, wk)], axis=0)
    conv_p = jnp.stack([proj[(n + 1) * seq - (CONV_WIDTH - 1):(n + 1) * seq, c0:c0 + qkv_b]
                        for n in range(n_p)])
    conv_s = jnp.concatenate([state_conv[l], raw_s], axis=1)[:, dseq:, :]

    merged = branch_merge(out_a, out_b, w_branch_a[l].astype(BF16), w_branch_b[l].astype(BF16),
                          proj, c_ab, c_ab + d, tm, 1024)
    y1 = matmul_residual(merged, w_out[l].astype(BF16), x, tm, 1024)

    n_exp = router_w.shape[2]
    w_r = jnp.concatenate([router_w[l], jnp.zeros((d, LANES - n_exp), F32)], axis=1).astype(BF16)
    r_logits, h2 = norm_matmul(y1, norm_ffn_g[l], w_r, tm, LANES, emit_h=True)
    logits = r_logits[:, :n_exp] + router_b[l].astype(F32)
    gate, row_tok, runs, pos = _route(logits, n_tok)
    x_rows = h2[lax.optimization_barrier(row_tok)]
    y_rows = moe_experts(x_rows, runs, w_gate_up.reshape(w_gate_up.shape[1:]),
                         b_gate_up.reshape(b_gate_up.shape[1:]), w_down.reshape(w_down.shape[1:]),
                         b_down.reshape(b_down.shape[1:]), 1024)
    picked = y_rows[pos.T.reshape(-1)]
    y_p, y_s = combine_residual_norm(y1, picked, gate, norm_final_g, math.gcd(n_prow, n_srow, 256), n_prow)

    y_prompt = y_p.reshape(n_p, seq, d)
    y_sample = y_s.reshape(n_s, dseq, d)
    k_p = k_all[:n_prow].reshape(1, n_p, seq, H_A, HD_A)
    v_p = v_all[:n_prow].reshape(1, n_p, seq, H_A, HD_A)
    k_s = k_all[n_prow:].reshape(1, n_s, dseq, H_A, HD_A)
    v_s = v_all[n_prow:].reshape(1, n_s, dseq, H_A, HD_A)
    return (y_prompt, y_sample, k_p, v_p, s_p[None], conv_p[None],
            k_s, v_s, s_s[None], conv_s[None])
```

```python
import functools
import math

import jax
import jax.numpy as jnp
from jax import lax
from jax.experimental import pallas as pl
from jax.experimental.pallas import tpu as pltpu

F32 = jnp.float32
BF16 = jnp.bfloat16
HIGHEST = lax.Precision.HIGHEST

RMS_EPS = 1e-6
NEG_INF = -1e30

H_A = 8
HD_A = 128
MOBA_BLOCK = 256
MOBA_TOPK = 3
MOBA_GROUP = 4
PAGE_SIZE = 128
N_BUCKETS = 32
MAX_DISTANCE = 4096

H_B = 8
DK_B = 128
CONV_WIDTH = 4
GDN_CHUNK = 64

TOP_K = 4
SWIGLU_LIMIT = 7.0
SWIGLU_ALPHA = 1.702
MOE_ROWS = 256

VMEM_LIMIT = 56 * 1024 * 1024
LANES = 128


def _cparams(sem):
    return pltpu.CompilerParams(dimension_semantics=sem, vmem_limit_bytes=VMEM_LIMIT)


def _dot(a, b, precision=None):
    return jnp.dot(a, b, preferred_element_type=F32, precision=precision)


def _dot_nt(a, b, precision=None):
    return lax.dot_general(a, b, (((1,), (1,)), ((), ())),
                           preferred_element_type=F32, precision=precision)


def _dot_tn(a, b, precision=None):
    return lax.dot_general(a, b, (((0,), (0,)), ((), ())),
                           preferred_element_type=F32, precision=precision)


def _norm_matmul_kernel(x_ref, g_ref, w_ref, o_ref, *rest, emit_h):
    if emit_h:
        h_out_ref, h_ref = rest
    else:
        (h_ref,) = rest

    @pl.when(pl.program_id(1) == 0)
    def _():
        x = x_ref[...]
        y = x * lax.rsqrt(jnp.mean(x * x, axis=-1, keepdims=True) + RMS_EPS)
        h_ref[...] = (y * g_ref[...]).astype(BF16)

    if emit_h:
        half = h_ref.shape[1] // 2
        h_out_ref[...] = _pack_bf16_pair(h_ref[:, :half], h_ref[:, half:])
    o_ref[...] = _dot(h_ref[...], w_ref[...])


def _pack_bf16_pair(hi, lo):
    hi_bits = lax.bitcast_convert_type(hi.astype(F32), jnp.uint32)
    lo_bits = lax.bitcast_convert_type(lo.astype(F32), jnp.uint32)
    return lax.bitcast_convert_type(hi_bits | (lo_bits >> 16), F32)


def _unpack_bf16_pair(words):
    bits = lax.bitcast_convert_type(words, jnp.uint32)
    hi = lax.bitcast_convert_type(bits & jnp.uint32(0xFFFF0000), F32)
    lo = lax.bitcast_convert_type(bits << 16, F32)
    return hi.astype(BF16), lo.astype(BF16)


def norm_matmul(x, g, w, tm, tn, emit_h=False):
    m, k = x.shape
    n = w.shape[1]
    out_shape = [jax.ShapeDtypeStruct((m, n), F32)]
    out_specs = [pl.BlockSpec((tm, tn), lambda i, j: (i, j))]
    if emit_h:
        out_shape.append(jax.ShapeDtypeStruct((m, k // 2), F32))
        out_specs.append(pl.BlockSpec((tm, k // 2), lambda i, j: (i, 0)))
    res = pl.pallas_call(
        functools.partial(_norm_matmul_kernel, emit_h=emit_h),
        grid=(m // tm, n // tn),
        in_specs=[pl.BlockSpec((tm, k), lambda i, j: (i, 0)),
                  pl.BlockSpec((1, k), lambda i, j: (0, 0)),
                  pl.BlockSpec((k, tn), lambda i, j: (0, j))],
        out_specs=out_specs,
        out_shape=out_shape,
        scratch_shapes=[pltpu.VMEM((tm, k), BF16)],
        compiler_params=_cparams(("parallel", "arbitrary")),
        name="norm_matmul",
    )(x, g.reshape(1, k), w)
    return res if emit_h else res[0]


def _moba_prompt_kernel(q_ref, k_ref, v_ref, rv_ref, o_ref,
                        bias_ref, kb_ref, vb_ref, kmean_ref, m_ref, l_ref, acc_ref,
                        *, n_blocks):
    i = pl.program_id(2)
    blk = MOBA_BLOCK
    seq_len = n_blocks * blk

    @pl.when((pl.program_id(1) == 0) & (i == 0))
    def _():
        for m in range(n_blocks):
            st = blk * (n_blocks - 1 - m)
            win = jnp.broadcast_to(rv_ref[0, :, st:st + 2 * blk], (blk, 2 * blk))
            bias_ref[m] = pltpu.roll(win, 0, 1, stride=1, stride_axis=0)[:, blk:]

    @pl.when(i == 0)
    def _():
        k = k_ref[...]
        kb_ref[:, :HD_A] = k.astype(BF16)
        rblk = lax.broadcasted_iota(jnp.int32, (seq_len, HD_A), 0) // blk
        lane = lax.broadcasted_iota(jnp.int32, (seq_len, HD_A), 1)
        kb_ref[:, HD_A:] = jnp.where(lane == rblk, 1.0, 0.0).astype(BF16)
        vb_ref[...] = v_ref[...].astype(BF16)
        kmean_ref[...] = jnp.mean(k.reshape(n_blocks, blk, HD_A), axis=1)

    qs = q_ref[...] * (HD_A ** -0.5)
    qb = qs.astype(BF16)

    st = _dot_nt(kmean_ref[...], qs, precision=HIGHEST)
    row0 = pl.multiple_of(i * blk, blk)
    logits = _dot_nt(qb, kb_ref[pl.ds(row0, blk), :HD_A]) + bias_ref[0]
    rowi = lax.broadcasted_iota(jnp.int32, st.shape, 0)
    st = jnp.where(rowi < i, st, NEG_INF)
    r = lax.broadcasted_iota(jnp.int32, (blk, blk), 0)
    c_ = lax.broadcasted_iota(jnp.int32, (blk, blk), 1)
    logits = jnp.where(c_ <= r, logits, NEG_INF)
    m0 = jnp.max(logits, axis=-1, keepdims=True)
    rank = jnp.zeros(st.shape, F32)
    for c in range(n_blocks - 1):
        sc = st[c:c + 1, :]
        rank = rank + jnp.where((sc > st) | ((sc == st) & (c < rowi)), 1.0, 0.0)
    sel = jnp.where((rank < MOBA_TOPK) & (rowi < i), 1.0, 0.0)
    p = jnp.exp(logits - m0)
    er = lax.broadcasted_iota(jnp.int32, (n_blocks, HD_A), 0)
    ec = lax.broadcasted_iota(jnp.int32, (n_blocks, HD_A), 1)
    selq = _dot_tn(sel, jnp.where(er == ec, 1.0, 0.0))
    m_ref[...] = m0
    l_ref[...] = jnp.sum(p, axis=-1, keepdims=True)
    acc_ref[...] = _dot(p.astype(BF16), vb_ref[pl.ds(row0, blk), :])
    lane = lax.broadcasted_iota(jnp.int32, selq.shape, 1)
    negm = jnp.where((lane < n_blocks) & (selq < 0.5), NEG_INF, 0.0)
    q_aug = jnp.concatenate([qb, negm.astype(BF16)], axis=1)

    grp = MOBA_GROUP

    def group(t, carry):
        j0 = grp * t
        rows = pl.ds(pl.multiple_of(j0 * blk, grp * blk), grp * blk)
        lg = _dot_nt(q_aug, kb_ref[rows, :])
        lg = lg + jnp.concatenate([bias_ref[jnp.maximum(i - j0 - g, 0)] for g in range(grp)], axis=1)
        m_old = m_ref[...]
        m_new = jnp.maximum(m_old, jnp.max(lg, axis=-1, keepdims=True))
        alpha = jnp.exp(m_old - m_new)
        pj = jnp.exp(lg - m_new)
        m_ref[...] = m_new
        l_ref[...] = alpha * l_ref[...] + jnp.sum(pj, axis=-1, keepdims=True)
        acc_ref[...] = alpha * acc_ref[...] + _dot(pj.astype(BF16), vb_ref[rows, :])
        return carry

    lax.fori_loop(0, (i + grp - 1) // grp, group, 0)
    o_ref[...] = (acc_ref[...] / l_ref[...]).astype(o_ref.dtype)


def moba_prompt(proj, rv, n_seq, seq_len, q_col, k_col, v_col):
    blk = MOBA_BLOCK
    nb = seq_len // blk
    assert nb % MOBA_GROUP == 0
    n_dist = (nb + 1) * blk
    rv = rv.reshape(H_A, 1, n_dist)
    return pl.pallas_call(
        functools.partial(_moba_prompt_kernel, n_blocks=nb),
        grid=(H_A, n_seq, nb),
        in_specs=[pl.BlockSpec((blk, HD_A), lambda h, n, i: (n * nb + i, q_col + h)),
                  pl.BlockSpec((seq_len, HD_A), lambda h, n, i: (n, k_col + h)),
                  pl.BlockSpec((seq_len, HD_A), lambda h, n, i: (n, v_col + h)),
                  pl.BlockSpec((1, 1, n_dist), lambda h, n, i: (h, 0, 0))],
        out_specs=pl.BlockSpec((blk, HD_A), lambda h, n, i: (n * nb + i, h)),
        out_shape=jax.ShapeDtypeStruct((n_seq * seq_len, H_A * HD_A), BF16),
        scratch_shapes=[pltpu.VMEM((nb, blk, blk), F32),
                        pltpu.VMEM((seq_len, 2 * HD_A), BF16),
                        pltpu.VMEM((seq_len, HD_A), BF16),
                        pltpu.VMEM((nb, HD_A), F32),
                        pltpu.VMEM((blk, 1), F32),
                        pltpu.VMEM((blk, 1), F32),
                        pltpu.VMEM((blk, HD_A), F32)],
        compiler_params=_cparams(("arbitrary", "arbitrary", "arbitrary")),
        name="moba_prompt",
    )(proj, proj, proj, rv)


PAGES_PER_STEP = 16
BLOCKS_PER_STEP = PAGES_PER_STEP * PAGE_SIZE // MOBA_BLOCK


def _moba_sample_keys_kernel(pt_ref, qbd_ref, knew_ref, bias_ref, bown_ref, *rest,
                             n_steps, n_tok):
    pps = PAGES_PER_STEP
    bps = BLOCKS_PER_STEP
    ppb = pps // bps
    kp = rest[:pps]
    p_ref, pown_ref, lg_ref, ksum_ref = rest[pps:]
    s = pl.program_id(1)
    qbd = qbd_ref[0]
    qbd_b = qbd.astype(BF16)

    def lg_block(b):
        return b // bps, slice((b % bps) * MOBA_BLOCK, (b % bps + 1) * MOBA_BLOCK)

    sums = []
    for k in range(pps):
        hsum = []
        heads = []
        for h in range(H_A):
            kh = kp[k][0, 0, pl.ds(h, PAGE_SIZE, stride=H_A), :]
            heads.append(kh.astype(BF16))
            hsum.append(jnp.sum(kh, axis=0, keepdims=True))
        page = jnp.concatenate(heads, axis=1)
        lg = _dot_nt(qbd_b, page) + bias_ref[:, k * PAGE_SIZE:(k + 1) * PAGE_SIZE]
        lg_ref[s, :, k * PAGE_SIZE:(k + 1) * PAGE_SIZE] = lg
        sums.append(jnp.concatenate(hsum, axis=1))
    blocks = [sum(sums[ppb * b:ppb * (b + 1)]) * (1.0 / MOBA_BLOCK) for b in range(bps)]
    ksum_ref[pl.ds(pl.multiple_of(s * bps, bps), bps), :] = jnp.concatenate(blocks, axis=0)

    @pl.when(s == n_steps - 1)
    def _():
        n_blk = n_steps * bps
        sc = _dot_nt(qbd, ksum_ref[...], precision=HIGHEST)
        bidx = lax.broadcasted_iota(jnp.int32, sc.shape, 1)
        sel = jnp.zeros(sc.shape, F32)
        for _ in range(MOBA_TOPK):
            mx = jnp.max(sc, axis=-1, keepdims=True)
            first = jnp.min(jnp.where(sc == mx, bidx, n_blk), axis=-1, keepdims=True)
            hit = bidx == first
            sel = jnp.where(hit, 1.0, sel)
            sc = jnp.where(hit, 2.0 * NEG_INF, sc)

        lo = _dot_nt(qbd, knew_ref[...]) + bown_ref[...]
        kc = lax.broadcasted_iota(jnp.int32, lo.shape, 1)
        qr = lax.broadcasted_iota(jnp.int32, lo.shape, 0) % n_tok
        lo = jnp.where(kc <= qr, lo, NEG_INF)

        def masked(b):
            st, sl = lg_block(b)
            return jnp.where(sel[:, b:b + 1] > 0.0, lg_ref[st, :, sl], NEG_INF)

        run = masked(0)
        for b in range(1, n_blk):
            run = jnp.maximum(run, masked(b))
        m = jnp.maximum(jnp.max(lo, axis=-1, keepdims=True), jnp.max(run, axis=-1, keepdims=True))
        e_own = jnp.exp(lo - m)
        tot = jnp.zeros(run.shape, F32)
        for b in range(n_blk):
            st, sl = lg_block(b)
            e = jnp.exp(masked(b) - m)
            lg_ref[st, :, sl] = e
            tot = tot + e
        l = jnp.sum(e_own, axis=-1, keepdims=True) + jnp.sum(tot, axis=-1, keepdims=True)
        inv = 1.0 / l
        pown_ref[0] = e_own * inv
        for b in range(n_blk):
            st, sl = lg_block(b)
            p_ref[0, :, b * MOBA_BLOCK:(b + 1) * MOBA_BLOCK] = (lg_ref[st, :, sl] * inv).astype(BF16)


def _moba_sample_values_kernel(pt_ref, p_ref, pown_ref, vnew_ref, *rest, n_steps, n_tok):
    pps = PAGES_PER_STEP
    vp = rest[:pps]
    o_ref, acc_ref = rest[pps:]
    s = pl.program_id(1)

    @pl.when(s == 0)
    def _():
        acc_ref[...] = _dot(pown_ref[0], vnew_ref[...])

    acc = acc_ref[...]
    for k in range(pps):
        page = jnp.concatenate(
            [vp[k][0, 0, pl.ds(h, PAGE_SIZE, stride=H_A), :].astype(BF16) for h in range(H_A)], axis=1)
        acc = acc + _dot(p_ref[0, :, k * PAGE_SIZE:(k + 1) * PAGE_SIZE], page)
    acc_ref[...] = acc

    @pl.when(s == n_steps - 1)
    def _():
        for h in range(H_A):
            o_ref[:, h * HD_A:(h + 1) * HD_A] = acc[h * n_tok:(h + 1) * n_tok,
                                                    h * HD_A:(h + 1) * HD_A].astype(o_ref.dtype)


def moba_sample(proj, row_blk0, k_colblk, v_colblk, qbd, cache_k, cache_v, layer, page_table,
                bias_past, bias_own, n_tok):
    n_seq, n_pages = page_table.shape
    pps = PAGES_PER_STEP
    n_steps = n_pages // pps
    n_past = n_pages * PAGE_SIZE
    wa = H_A * HD_A
    nrow = H_A * n_tok
    page_block = (1, 1, PAGE_SIZE * H_A, HD_A)
    cache_k = cache_k.reshape(cache_k.shape[:2] + (PAGE_SIZE * H_A, HD_A))
    cache_v = cache_v.reshape(cache_v.shape[:2] + (PAGE_SIZE * H_A, HD_A))

    def pmap(k):
        return lambda b, s, pt: (layer, pt[b, s * pps + k], 0, 0)

    p, p_own = pl.pallas_call(
        functools.partial(_moba_sample_keys_kernel, n_steps=n_steps, n_tok=n_tok),
        grid_spec=pltpu.PrefetchScalarGridSpec(
            num_scalar_prefetch=1,
            grid=(n_seq, n_steps),
            in_specs=[pl.BlockSpec((1, nrow, wa), lambda b, s, pt: (b, 0, 0)),
                      pl.BlockSpec((n_tok, wa), lambda b, s, pt: (row_blk0 + b, k_colblk)),
                      pl.BlockSpec((nrow, pps * PAGE_SIZE), lambda b, s, pt: (0, s)),
                      pl.BlockSpec((nrow, n_tok), lambda b, s, pt: (0, 0))]
            + [pl.BlockSpec(page_block, pmap(k)) for k in range(pps)],
            out_specs=[pl.BlockSpec((1, nrow, n_past), lambda b, s, pt: (b, 0, 0)),
                       pl.BlockSpec((1, nrow, n_tok), lambda b, s, pt: (b, 0, 0))],
            scratch_shapes=[pltpu.VMEM((n_steps, nrow, pps * PAGE_SIZE), F32),
                            pltpu.VMEM((n_past // MOBA_BLOCK, wa), F32)]),
        out_shape=[jax.ShapeDtypeStruct((n_seq, nrow, n_past), BF16),
                   jax.ShapeDtypeStruct((n_seq, nrow, n_tok), F32)],
        compiler_params=_cparams(("arbitrary", "arbitrary")),
        name="moba_sample_keys",
    )(page_table, qbd, proj, bias_past, bias_own, *([cache_k] * pps))

    return pl.pallas_call(
        functools.partial(_moba_sample_values_kernel, n_steps=n_steps, n_tok=n_tok),
        grid_spec=pltpu.PrefetchScalarGridSpec(
            num_scalar_prefetch=1,
            grid=(n_seq, n_steps),
            in_specs=[pl.BlockSpec((1, nrow, pps * PAGE_SIZE), lambda b, s, pt: (b, 0, s)),
                      pl.BlockSpec((1, nrow, n_tok), lambda b, s, pt: (b, 0, 0)),
                      pl.BlockSpec((n_tok, wa), lambda b, s, pt: (row_blk0 + b, v_colblk))]
            + [pl.BlockSpec(page_block, pmap(k)) for k in range(pps)],
            out_specs=pl.BlockSpec((n_tok, wa), lambda b, s, pt: (b, 0)),
            scratch_shapes=[pltpu.VMEM((nrow, wa), F32)]),
        out_shape=jax.ShapeDtypeStruct((n_seq * n_tok, wa), BF16),
        compiler_params=_cparams(("arbitrary", "arbitrary")),
        name="moba_sample_values",
    )(page_table, p, p_own, proj, *([cache_v] * pps))


def _gdn_prep_kernel(x_ref, halo_ref, ab_ref, cw_ref, alog_ref, dtb_ref,
                     q_ref, k_ref, v_ref, gb_ref, *, tt):
    x = x_ref[...]
    xf = jnp.concatenate([halo_ref[0], x], axis=0)
    cw = cw_ref[...]
    conv = xf[5:5 + tt] * cw[0:1]
    for w in range(1, CONV_WIDTH):
        conv = conv + xf[5 + w:5 + w + tt] * cw[w:w + 1]
    act = conv * jax.nn.sigmoid(conv)
    wk = H_B * DK_B
    for h in range(H_B):
        q = act[:, h * DK_B:(h + 1) * DK_B]
        k = act[:, wk + h * DK_B:wk + (h + 1) * DK_B]
        q_ref[0, h] = q * lax.rsqrt(jnp.sum(q * q, axis=-1, keepdims=True) + RMS_EPS) * (DK_B ** -0.5)
        k_ref[0, h] = k * lax.rsqrt(jnp.sum(k * k, axis=-1, keepdims=True) + RMS_EPS)
        v_ref[0, h] = act[:, 2 * wk + h * DK_B:2 * wk + (h + 1) * DK_B]
    ab = ab_ref[...]
    t = ab + dtb_ref[...]
    sp = jnp.maximum(t, 0.0) + jnp.log(1.0 + jnp.exp(-jnp.abs(t)))
    g = -jnp.exp(alog_ref[...]) * sp
    lane = lax.broadcasted_iota(jnp.int32, ab.shape, 1)
    gb_ref[0] = jnp.where(lane < H_B, g, jax.nn.sigmoid(ab))


def gdn_prep(proj, ab, halo, conv_w, a_log, dt_bias, row0, n_seq, seq_len, tt, qkv_colblk):
    w3 = conv_w.shape[1]
    nt = seq_len // tt
    rb0 = row0 // tt
    pad = jnp.zeros((1, LANES - H_B), F32)
    alog = jnp.concatenate([a_log.reshape(1, H_B), pad], axis=1)
    dtb = jnp.concatenate([dt_bias.reshape(1, H_B), pad], axis=1)
    hm = jax.ShapeDtypeStruct((n_seq, H_B, seq_len, DK_B), F32)
    hspec = pl.BlockSpec((1, H_B, tt, DK_B), lambda n, t: (n, 0, t, 0))
    return pl.pallas_call(
        functools.partial(_gdn_prep_kernel, tt=tt),
        grid=(n_seq, nt),
        in_specs=[pl.BlockSpec((tt, w3), lambda n, t: (rb0 + n * nt + t, qkv_colblk)),
                  pl.BlockSpec((1, 8, w3), lambda n, t: (n * nt + t, 0, 0)),
                  pl.BlockSpec((tt, LANES), lambda n, t: (rb0 + n * nt + t, 0)),
                  pl.BlockSpec((CONV_WIDTH, w3), lambda n, t: (0, 0)),
                  pl.BlockSpec((1, LANES), lambda n, t: (0, 0)),
                  pl.BlockSpec((1, LANES), lambda n, t: (0, 0))],
        out_specs=[hspec, hspec, hspec,
                   pl.BlockSpec((1, tt, LANES), lambda n, t: (n, t, 0))],
        out_shape=[hm, hm, hm, jax.ShapeDtypeStruct((n_seq, seq_len, LANES), F32)],
        compiler_params=_cparams(("parallel", "parallel")),
        name="gdn_prep",
    )(proj, halo, ab, conv_w, alog, dtb)


def _split2(a):
    hi = a.astype(BF16)
    return hi, (a - hi.astype(F32)).astype(BF16)


def _split3(a):
    hi = a.astype(BF16)
    r1 = a - hi.astype(F32)
    mid = r1.astype(BF16)
    return hi, mid, (r1 - mid.astype(F32)).astype(BF16)


def _dot3(a, b):
    ah, al = _split2(a)
    bh, bl = _split2(b)
    return _dot(ah, bh) + (_dot(ah, bl) + _dot(al, bh))


def _dot_exact_lhs(a_bf16, b):
    b1, b2, b3 = _split3(b)
    return _dot(a_bf16, b1) + (_dot(a_bf16, b2) + _dot(a_bf16, b3))


def _unit_lower_inverse(a_list, c):
    r = lax.broadcasted_iota(jnp.int32, (c, c), 0)
    cc = lax.broadcasted_iota(jnp.int32, (c, c), 1)
    eye = jnp.where(r == cc, 1.0, 0.0)
    base = min(c, 8)
    n1 = [jnp.where((r // base) == (cc // base), -a, 0.0) for a in a_list]
    n2 = [_dot3(x, x) for x in n1]
    n4 = [_dot3(x, x) for x in n2]
    t = [_dot3(eye + x, eye + y) for x, y in zip(n1, n2)]
    t = [_dot3(x, eye + y) for x, y in zip(t, n4)]
    b = base
    while b < c:
        inner = ((r // (2 * b)) == (cc // (2 * b))) & ((r // b) != (cc // b))
        left = [_dot3(x, jnp.where(inner, a, 0.0)) for x, a in zip(t, a_list)]
        t = [x - _dot3(y, x) for x, y in zip(t, left)]
        b *= 2
    return t


def _gdn_chunk_kernel(*refs, nb, c):
    q_ref, k_ref, v_ref, gb_ref = refs[:4]
    z_refs = refs[4:4 + nb]
    ng_ref, s0_ref, o_ref, sout_ref, s_ref = refs[4 + nb:]
    ci = pl.program_id(1)

    @pl.when(ci == 0)
    def _():
        s_ref[...] = s0_ref[...]

    r = lax.broadcasted_iota(jnp.int32, (c, c), 0)
    cc = lax.broadcasted_iota(jnp.int32, (c, c), 1)
    incl = r >= cc
    strict = r > cc
    ltri = jnp.where(incl, 1.0, 0.0).astype(BF16)
    eye = jnp.where(r == cc, 1.0, 0.0)
    ones = jnp.ones((c, c), BF16)
    ng = ng_ref[...]
    ch = [(n, h) for n in range(nb) for h in range(H_B)]
    gbv = [gb_ref[n] for n in range(nb)]
    q = [q_ref[n, h] for n, h in ch]
    k = [k_ref[n, h] for n, h in ch]
    v = [v_ref[n, h] for n, h in ch]
    gcol = [jnp.broadcast_to(gbv[n][:, h:h + 1], (c, DK_B)) for n, h in ch]
    bcol = [jnp.broadcast_to(gbv[n][:, H_B + h:H_B + h + 1], (c, DK_B)) for n, h in ch]
    big_g = [_dot_exact_lhs(ltri, g) for g in gcol]
    gj = [_dot_exact_lhs(ones, g[:, :c] * eye) for g in big_g]
    decay = [jnp.where(incl, jnp.exp(jnp.minimum(g[:, :c] - x, 0.0)), 0.0) for g, x in zip(big_g, gj)]
    kb = [x.astype(BF16) for x in k]
    kk = [_dot_nt(x, x) for x in kb]
    a = [jnp.where(strict, b[:, :c] * x * d, 0.0) for b, x, d in zip(bcol, kk, decay)]
    t = _unit_lower_inverse(a, c)
    gam = [jnp.exp(g) for g in big_g]
    u = [_dot3(x, b * y) for x, b, y in zip(t, bcol, v)]
    w = [_dot3(x, b * g * y) for x, b, g, y in zip(t, bcol, gam, k)]
    qk = [_dot_nt(x.astype(BF16), y) * d for x, y, d in zip(q, kb, decay)]
    g_last = [g[c - 1:c, :] for g in big_g]
    k_tail = [jnp.exp(gl - g) * y for gl, g, y in zip(g_last, big_g, k)]
    s = [s_ref[n, h] for n, h in ch]
    sb = [x.astype(BF16) for x in s]
    uu = [x - _dot(y.astype(BF16), z_) for x, y, z_ in zip(u, w, sb)]
    ub = [x.astype(BF16) for x in uu]
    o = [_dot((g * x).astype(BF16), z_) + _dot(y.astype(BF16), x2)
         for g, x, z_, y, x2 in zip(gam, q, sb, qk, ub)]
    s_new = [jnp.exp(gl) * z_ + _dot_tn(y.astype(BF16), x) for gl, z_, y, x in zip(g_last, s, k_tail, ub)]
    for (n, h), x in zip(ch, s_new):
        s_ref[n, h] = x
    for (n, h), x in zip(ch, o):
        x = x * lax.rsqrt(jnp.mean(x * x, axis=-1, keepdims=True) + RMS_EPS) * ng
        zh = z_refs[n][:, h * DK_B:(h + 1) * DK_B]
        o_ref[n, :, h * DK_B:(h + 1) * DK_B] = (x * (zh * jax.nn.sigmoid(zh))).astype(o_ref.dtype)

    @pl.when(ci == pl.num_programs(1) - 1)
    def _():
        sout_ref[...] = s_ref[...]


def gdn_chunks(q, k, v, gb, proj, z_colblk, row0, norm_g, s0, nb, c):
    n_seq, _, seq_len, _ = q.shape
    nc = seq_len // c
    wv = H_B * DK_B
    rb0 = row0 // c
    hspec = pl.BlockSpec((nb, H_B, c, DK_B), lambda n, ci: (n, 0, ci, 0))
    sspec = pl.BlockSpec((nb, H_B, DK_B, DK_B), lambda n, ci: (n, 0, 0, 0))

    def zmap(j):
        return lambda n, ci: (rb0 + (n * nb + j) * nc + ci, z_colblk)

    return pl.pallas_call(
        functools.partial(_gdn_chunk_kernel, nb=nb, c=c),
        grid=(n_seq // nb, nc),
        in_specs=[hspec, hspec, hspec,
                  pl.BlockSpec((nb, c, LANES), lambda n, ci: (n, ci, 0))]
        + [pl.BlockSpec((c, wv), zmap(j)) for j in range(nb)]
        + [pl.BlockSpec((1, DK_B), lambda n, ci: (0, 0)), sspec],
        out_specs=[pl.BlockSpec((nb, c, wv), lambda n, ci: (n, ci, 0)), sspec],
        out_shape=[jax.ShapeDtypeStruct((n_seq, seq_len, wv), BF16),
                   jax.ShapeDtypeStruct((n_seq, H_B, DK_B, DK_B), F32)],
        scratch_shapes=[pltpu.VMEM((nb, H_B, DK_B, DK_B), F32)],
        compiler_params=_cparams(("parallel", "arbitrary")),
        name="gdn_chunks",
    )(q, k, v, gb, *([proj] * nb), norm_g.reshape(1, DK_B), s0)


def _merge_kernel(oa_ref, ob_ref, wa_ref, wb_ref, ga_ref, gb_ref, o_ref):
    ya = _dot(oa_ref[...], wa_ref[...])
    yb = _dot(ob_ref[...], wb_ref[...])
    o_ref[...] = (jax.nn.sigmoid(ga_ref[...]) * ya
                  + jax.nn.sigmoid(gb_ref[...]) * yb).astype(o_ref.dtype)


def branch_merge(out_a, out_b, wa, wb, proj, ga_col0, gb_col0, tm, tn):
    m, ka = out_a.shape
    n = wa.shape[1]
    ga_blk, gb_blk = ga_col0 // tn, gb_col0 // tn
    assert ga_blk * tn == ga_col0 and gb_blk * tn == gb_col0
    return pl.pallas_call(
        _merge_kernel,
        grid=(m // tm, n // tn),
        in_specs=[pl.BlockSpec((tm, ka), lambda i, j: (i, 0)),
                  pl.BlockSpec((tm, ka), lambda i, j: (i, 0)),
                  pl.BlockSpec((ka, tn), lambda i, j: (0, j)),
                  pl.BlockSpec((ka, tn), lambda i, j: (0, j)),
                  pl.BlockSpec((tm, tn), lambda i, j: (i, ga_blk + j)),
                  pl.BlockSpec((tm, tn), lambda i, j: (i, gb_blk + j))],
        out_specs=pl.BlockSpec((tm, tn), lambda i, j: (i, j)),
        out_shape=jax.ShapeDtypeStruct((m, n), BF16),
        compiler_params=_cparams(("parallel", "parallel")),
        name="branch_merge",
    )(out_a, out_b, wa, wb, proj, proj)


def _matmul_residual_kernel(x_ref, w_ref, r_ref, o_ref):
    o_ref[...] = r_ref[...] + _dot(x_ref[...], w_ref[...])


def matmul_residual(x, w, res, tm, tn):
    m, k = x.shape
    n = w.shape[1]
    return pl.pallas_call(
        _matmul_residual_kernel,
        grid=(m // tm, n // tn),
        in_specs=[pl.BlockSpec((tm, k), lambda i, j: (i, 0)),
                  pl.BlockSpec((k, tn), lambda i, j: (0, j)),
                  pl.BlockSpec((tm, tn), lambda i, j: (i, j))],
        out_specs=pl.BlockSpec((tm, tn), lambda i, j: (i, j)),
        out_shape=jax.ShapeDtypeStruct((m, n), F32),
        compiler_params=_cparams(("parallel", "parallel")),
        name="out_proj",
    )(x, w, res)


MOE_TA = 256


def _lane_rotate(x, shift):
    parts = [pltpu.roll(x[:, c:c + LANES], shift, 1) for c in range(0, x.shape[1], LANES)]
    return parts[0] if len(parts) == 1 else jnp.concatenate(parts, axis=1)


MOE_RUN = 8


def _block_out_copy(buf_ref, out_ref, sem_ref, j, first_block, col_tile):
    rows, cols = buf_ref.shape[1:]
    dst = out_ref.at[pl.ds(pl.multiple_of((first_block + j) * rows, rows), rows),
                     pl.ds(pl.multiple_of(col_tile * cols, cols), cols)]
    return pltpu.make_async_copy(buf_ref.at[j], dst, sem_ref.at[j])


def _zero_fill(buf_ref, out_ref, sem_ref, n_fill, first_block, col_tile):
    for j in range(buf_ref.shape[0]):
        @pl.when(j < n_fill)
        def _(j=j):
            buf_ref[j] = jnp.zeros(buf_ref.shape[1:], buf_ref.dtype)
            _block_out_copy(buf_ref, out_ref, sem_ref, j, first_block, col_tile).start()
    for j in range(buf_ref.shape[0]):
        @pl.when(j < n_fill)
        def _(j=j):
            _block_out_copy(buf_ref, out_ref, sem_ref, j, first_block, col_tile).wait()


def _moe_up_kernel(re_ref, rb_ref, rl_ref, nr_ref, *refs):
    r = MOE_RUN
    x_refs = refs[:r]
    wa_ref, wb_ref, ba_ref, bb_ref, out_ref, wab_ref, wbb_ref, buf_ref, sem_ref = refs[r:]
    v = pl.program_id(0)
    n = pl.program_id(1)

    @pl.when(v >= nr_ref[0])
    def _():
        _zero_fill(buf_ref, out_ref, sem_ref, rl_ref[v], rb_ref[v], n)

    @pl.when(v < nr_ref[0])
    def _():
        wab_ref[...] = wa_ref[0].astype(BF16)
        wbb_ref[...] = wb_ref[0].astype(BF16)
        n_live = rl_ref[v]
        for j in range(r):
            @pl.when(j < n_live)
            def _(j=j):
                half = x_refs[j].shape[1]
                x_hi, x_lo = _unpack_bf16_pair(x_refs[j][...])
                ga = (_dot(x_hi, wab_ref[:half, :]) + _dot(x_lo, wab_ref[half:, :])
                      + ba_ref[0])
                gb = _dot(x_hi, wbb_ref[:half, :]) + _dot(x_lo, wbb_ref[half:, :]) + bb_ref[0]
                even = (lax.broadcasted_iota(jnp.int32, ga.shape, 1) % 2) == 0
                gate = jnp.where(even, ga, _lane_rotate(gb, 1))
                up = jnp.where(even, _lane_rotate(ga, LANES - 1), gb)
                gl = jnp.minimum(gate, SWIGLU_LIMIT)
                up = jnp.clip(up, -SWIGLU_LIMIT, SWIGLU_LIMIT)
                buf_ref[j] = (gl * jax.nn.sigmoid(SWIGLU_ALPHA * gl) * (up + 1.0)).astype(BF16)
                _block_out_copy(buf_ref, out_ref, sem_ref, j, rb_ref[v], n).start()
        for j in range(r):
            @pl.when(j < n_live)
            def _(j=j):
                _block_out_copy(buf_ref, out_ref, sem_ref, j, rb_ref[v], n).wait()


def _bf16_bits(w):
    return lax.bitcast_convert_type(w.astype(BF16).astype(F32), jnp.uint32)


def _moe_down_kernel(re_ref, rb_ref, rl_ref, nr_ref, *refs):
    r = MOE_RUN
    a_refs = refs[:r]
    wd_ref, bd_ref, out_ref, wdb_ref, buf_ref, sem_ref = refs[r:]
    v = pl.program_id(0)
    n = pl.program_id(1)

    @pl.when(v >= nr_ref[0])
    def _():
        _zero_fill(buf_ref, out_ref, sem_ref, rl_ref[v], rb_ref[v], n)

    @pl.when(v < nr_ref[0])
    def _():
        f = wd_ref.shape[1]
        for t in range(f // (2 * MOE_TA)):
            wa = wd_ref[0, t * MOE_TA:(t + 1) * MOE_TA, :]
            wb = wd_ref[0, f // 2 + t * MOE_TA:f // 2 + (t + 1) * MOE_TA, :]
            pair = (_bf16_bits(wa) >> 16) | _bf16_bits(wb)
            wdb_ref[t * 2 * MOE_TA:(t + 1) * 2 * MOE_TA, :] = pltpu.bitcast(pair, BF16)
        n_live = rl_ref[v]
        for j in range(r):
            @pl.when(j < n_live)
            def _(j=j):
                buf_ref[j] = _dot(a_refs[j][...], wdb_ref[...]) + bd_ref[0]
                _block_out_copy(buf_ref, out_ref, sem_ref, j, rb_ref[v], n).start()
        for j in range(r):
            @pl.when(j < n_live)
            def _(j=j):
                _block_out_copy(buf_ref, out_ref, sem_ref, j, rb_ref[v], n).wait()


def moe_experts(x_rows, runs, w_gate_up, b_gate_up, w_down, b_down, tn_down):
    rows = x_rows.shape[0]
    n_exp, d, f2 = w_gate_up.shape
    f = f2 // 2
    dm = w_down.shape[2]
    n_steps = runs[0].shape[0]
    r = MOE_RUN

    def rowmap(j):
        return lambda v, n, re, rb, rl, nr: (rb[v] + jnp.minimum(j, rl[v] - 1), 0)

    def wmap(off, n_tiles):
        return lambda v, n, re, rb, rl, nr: (re[v], 0, off + jnp.where(v < nr[0], n, n_tiles - 1))

    nt = f // (2 * MOE_TA)
    tw = 2 * MOE_TA
    bgu = b_gate_up.reshape(n_exp, 1, f2)
    act = pl.pallas_call(
        _moe_up_kernel,
        grid_spec=pltpu.PrefetchScalarGridSpec(
            num_scalar_prefetch=4,
            grid=(n_steps, nt),
            in_specs=[pl.BlockSpec((MOE_ROWS, d // 2), rowmap(j)) for j in range(r)]
            + [pl.BlockSpec((1, d, tw), wmap(0, nt)),
               pl.BlockSpec((1, d, tw), wmap(nt, nt)),
               pl.BlockSpec((1, 1, tw), wmap(0, nt)),
               pl.BlockSpec((1, 1, tw), wmap(nt, nt))],
            out_specs=pl.BlockSpec(memory_space=pl.ANY),
            scratch_shapes=[pltpu.VMEM((d, tw), BF16), pltpu.VMEM((d, tw), BF16),
                            pltpu.VMEM((r, MOE_ROWS, tw), BF16), pltpu.SemaphoreType.DMA((r,))]),
        out_shape=jax.ShapeDtypeStruct((rows, f), BF16),
        compiler_params=_cparams(("arbitrary", "arbitrary")),
        name="moe_up",
    )(*runs, *([x_rows] * r), w_gate_up, w_gate_up, bgu, bgu)
    nt_d = dm // tn_down
    return pl.pallas_call(
        _moe_down_kernel,
        grid_spec=pltpu.PrefetchScalarGridSpec(
            num_scalar_prefetch=4,
            grid=(n_steps, nt_d),
            in_specs=[pl.BlockSpec((MOE_ROWS, f), rowmap(j)) for j in range(r)]
            + [pl.BlockSpec((1, f, tn_down), wmap(0, nt_d)),
               pl.BlockSpec((1, 1, tn_down), wmap(0, nt_d))],
            out_specs=pl.BlockSpec(memory_space=pl.ANY),
            scratch_shapes=[pltpu.VMEM((f, tn_down), BF16),
                            pltpu.VMEM((r, MOE_ROWS, tn_down), F32), pltpu.SemaphoreType.DMA((r,))]),
        out_shape=jax.ShapeDtypeStruct((rows, dm), F32),
        compiler_params=_cparams(("arbitrary", "arbitrary")),
        name="moe_down",
    )(*runs, *([act] * r), w_down, b_down.reshape(n_exp, 1, dm))


def _final_kernel(y_ref, *rest, n_first):
    e_refs = rest[:TOP_K]
    gate_ref, g_ref, o1_ref, o2_ref = rest[TOP_K:]
    x = y_ref[...]
    for kk in range(TOP_K):
        x = x + e_refs[kk][...] * gate_ref[:, kk:kk + 1]
    out = x * lax.rsqrt(jnp.mean(x * x, axis=-1, keepdims=True) + RMS_EPS) * g_ref[...]
    i = pl.program_id(0)

    @pl.when(i < n_first)
    def _():
        o1_ref[...] = out

    @pl.when(i >= n_first)
    def _():
        o2_ref[...] = out


def combine_residual_norm(y, expert_out, gate, g, tm, m_first):
    m, d = y.shape
    nt = m // tm
    n_first = m_first // tm
    assert nt * tm == m and n_first * tm == m_first and 0 < n_first < nt
    return pl.pallas_call(
        functools.partial(_final_kernel, n_first=n_first),
        grid=(nt,),
        in_specs=[pl.BlockSpec((tm, d), lambda i: (i, 0))]
        + [pl.BlockSpec((tm, d), functools.partial(lambda i, kk: (kk * nt + i, 0), kk=kk))
           for kk in range(TOP_K)]
        + [pl.BlockSpec((tm, TOP_K), lambda i: (i, 0)),
           pl.BlockSpec((1, d), lambda i: (0, 0))],
        out_specs=[pl.BlockSpec((tm, d), lambda i: (jnp.minimum(i, n_first - 1), 0)),
                   pl.BlockSpec((tm, d), lambda i: (jnp.maximum(i - n_first, 0), 0))],
        out_shape=[jax.ShapeDtypeStruct((m_first, d), F32),
                   jax.ShapeDtypeStruct((m - m_first, d), F32)],
        compiler_params=_cparams(("arbitrary",)),
        name="combine_norm",
    )(y, *([expert_out] * TOP_K), gate, g.reshape(1, d))


def _rel_bias_table(rel_bias, dist):
    n = jnp.maximum(dist, 0)
    max_exact = N_BUCKETS // 2
    nf = jnp.maximum(n, max_exact).astype(F32)
    large = max_exact + (jnp.log(nf / max_exact) / math.log(MAX_DISTANCE / max_exact)
                         * (N_BUCKETS - max_exact)).astype(jnp.int32)
    bucket = jnp.where(n < max_exact, n, jnp.minimum(large, N_BUCKETS - 1))
    return rel_bias[:, bucket].astype(F32)


def _route(logits, n_tok):
    top_v, top_e = lax.top_k(logits, TOP_K)
    gate = jax.nn.softmax(top_v, axis=-1)
    n_exp = logits.shape[1]
    n_assign = n_tok * TOP_K
    flat_e = top_e.reshape(-1)
    order = jnp.argsort(flat_e, stable=True)
    e_sorted = flat_e[order]
    tok_sorted = (order // TOP_K).astype(jnp.int32)
    counts = jnp.zeros((n_exp,), jnp.int32).at[flat_e].add(1)
    padded = (counts + MOE_ROWS - 1) // MOE_ROWS * MOE_ROWS
    pad_end = jnp.cumsum(padded)
    pad_start = pad_end - padded
    start = jnp.cumsum(counts) - counts
    dest = (pad_start[e_sorted] + jnp.arange(n_assign) - start[e_sorted]).astype(jnp.int32)
    n_blocks = -(-n_assign // MOE_ROWS) + n_exp
    row_tok = jnp.zeros((n_blocks * MOE_ROWS,), jnp.int32).at[dest].add(tok_sorted)
    pos = jnp.zeros((n_assign,), jnp.int32).at[order].add(dest).reshape(n_tok, TOP_K)
    nblk_e = padded // MOE_ROWS
    nrun_e = (nblk_e + MOE_RUN - 1) // MOE_RUN
    run_end = jnp.cumsum(nrun_e)
    run_start = run_end - nrun_e
    n_runs = run_end[-1]
    n_steps = n_exp + -(-n_blocks // MOE_RUN)
    steps = jnp.arange(n_steps)
    step = jnp.minimum(steps, n_runs - 1)
    run_e = jnp.sum(run_end[None, :] <= step[:, None], axis=1)
    piece = step - run_start[run_e]
    run_b0 = pad_start[run_e] // MOE_ROWS + piece * MOE_RUN
    run_len = jnp.minimum(MOE_RUN, nblk_e[run_e] - piece * MOE_RUN)
    tail_b0 = jnp.minimum(pad_end[-1] // MOE_ROWS + (steps - n_runs) * MOE_RUN, n_blocks)
    tail_len = jnp.clip(n_blocks - tail_b0, 0, MOE_RUN)
    real = steps < n_runs
    run_b0 = jnp.where(real, run_b0, tail_b0)
    run_len = jnp.where(real, run_len, tail_len)
    runs = (run_e.astype(jnp.int32), run_b0.astype(jnp.int32), run_len.astype(jnp.int32),
            n_runs.astype(jnp.int32).reshape(1))
    return gate, row_tok, runs, pos


def kernel(x_prompt, x_sample, cache_k, cache_v, state_gdn, state_conv, page_table, norm_mix_g, w_in, rel_bias, conv_w, a_log, dt_bias, gdn_norm_g, w_branch_a, w_branch_b, w_out, norm_ffn_g, router_w, router_b, w_gate_up, b_gate_up, w_down, b_down, norm_final_g):
    n_p, seq, d = x_prompt.shape
    n_s, dseq, _ = x_sample.shape
    assert w_in.shape[0] == 1
    l = 0
    wa = H_A * HD_A
    wk = H_B * DK_B
    qkv_b = 3 * wk
    n_prow = n_p * seq
    n_srow = n_s * dseq
    n_tok = n_prow + n_srow
    past = page_table.shape[1] * PAGE_SIZE
    assert past % MOBA_BLOCK == 0 and seq % MOBA_BLOCK == 0

    x = jnp.concatenate([x_prompt.reshape(n_prow, d), x_sample.reshape(n_srow, d)], axis=0)

    w = w_in[l]
    c_z = 3 * wa + qkv_b
    c_ab = c_z + wk
    c_g = c_ab + 2 * H_B
    w_main = jnp.concatenate([w[:, :c_ab], w[:, c_g:]], axis=1).astype(BF16)
    w_ab = jnp.concatenate([w[:, c_ab:c_g], jnp.zeros((d, LANES - 2 * H_B), F32)], axis=1).astype(BF16)
    tm = 768
    proj = norm_matmul(x, norm_mix_g[l], w_main, tm, 1024)
    ab = norm_matmul(x, norm_mix_g[l], w_ab, tm, LANES)

    k_all = proj[:, wa:2 * wa]
    v_all = proj[:, 2 * wa:3 * wa]

    d_max = max(past + dseq - 1, seq)
    far_first = _rel_bias_table(rel_bias, d_max - jnp.arange(d_max + MOBA_BLOCK))
    out_a_p = moba_prompt(proj, far_first[:, d_max - seq:], n_p, seq, 0, H_A, 2 * H_A)

    q_s = proj[n_prow:, :wa].reshape(n_s, dseq, H_A, HD_A) * (HD_A ** -0.5)
    qbd = jnp.einsum('blhd,hg->bhlgd', q_s, jnp.eye(H_A, dtype=F32)).reshape(n_s, H_A * dseq, wa)
    lead = d_max - (past + dseq - 1)
    bias_past = jnp.stack([far_first[:, lead + dseq - 1 - t:lead + dseq - 1 - t + past]
                           for t in range(dseq)], axis=1)
    bias_past = bias_past.reshape(H_A * dseq, past)
    bias_own = jnp.stack([far_first[:, d_max - t:d_max - t + dseq] for t in range(dseq)], axis=1)
    bias_own = bias_own.reshape(H_A * dseq, dseq)
    out_a_s = moba_sample(proj, n_prow // dseq, 1, 2, qbd, cache_k, cache_v, l, page_table,
                          bias_past, bias_own, dseq)
    out_a = jnp.concatenate([out_a_p, out_a_s], axis=0)

    qkv_colblk = 3 * wa // qkv_b
    assert qkv_colblk * qkv_b == 3 * wa
    tt = 512
    c0 = 3 * wa
    groups = lax.slice(proj.reshape(n_tok // 8, 8, proj.shape[1]), (tt // 8 - 1, 0, c0),
                       (n_prow // 8, 8, c0 + qkv_b), (tt // 8, 1, 1)).reshape(n_p, seq // tt, 8, qkv_b)
    halo0 = jnp.zeros((n_p, 1, 8, qkv_b), F32)
    halo_p = jnp.concatenate([halo0, groups[:, :-1]], axis=1).reshape(n_p * (seq // tt), 8, qkv_b)
    qp, kp, vp, gbp = gdn_prep(proj, ab, halo_p, conv_w[l], a_log[l], dt_bias[l],
                               0, n_p, seq, tt, qkv_colblk)
    raw_s = proj[n_prow:, 3 * wa:3 * wa + qkv_b].reshape(n_s, dseq, qkv_b)
    halo_s = jnp.concatenate([jnp.zeros((n_s, 8 - (CONV_WIDTH - 1), qkv_b), F32), state_conv[l]], axis=1)
    qs_, ks_, vs_, gbs = gdn_prep(proj, ab, halo_s, conv_w[l], a_log[l], dt_bias[l],
                                  n_prow, n_s, dseq, dseq, qkv_colblk)
    z_colblk = c_z // wk
    assert z_colblk * wk == c_z
    c_p = math.gcd(seq, GDN_CHUNK)
    out_b_p, s_p = gdn_chunks(qp, kp, vp, gbp, proj, z_colblk, 0, gdn_norm_g[l],
                              jnp.zeros((n_p, H_B, DK_B, DK_B), F32), n_p, c_p)
    c_s = math.gcd(dseq, GDN_CHUNK)
    out_b_s, s_s = gdn_chunks(qs_, ks_, vs_, gbs, proj, z_colblk, n_prow, gdn_norm_g[l],
                              state_gdn[l], 2, c_s)
    out_b = jnp.concatenate([out_b_p.reshape(n_prow, wk), out_b_s.reshape(n_srow, wk)], axis=0)
    conv_p = jnp.stack([proj[(n + 1) * seq - (CONV_WIDTH - 1):(n + 1) * seq, c0:c0 + qkv_b]
                        for n in range(n_p)])
    conv_s = jnp.concatenate([state_conv[l], raw_s], axis=1)[:, dseq:, :]

    merged = branch_merge(out_a, out_b, w_branch_a[l].astype(BF16), w_branch_b[l].astype(BF16),
                          proj, c_ab, c_ab + d, tm, 1024)
    y1 = matmul_residual(merged, w_out[l].astype(BF16), x, tm, 1024)

    n_exp = router_w.shape[2]
    w_r = jnp.concatenate([router_w[l], jnp.zeros((d, LANES - n_exp), F32)], axis=1).astype(BF16)
    r_logits, h2 = norm_matmul(y1, norm_ffn_g[l], w_r, tm, LANES, emit_h=True)
    logits = r_logits[:, :n_exp] + router_b[l].astype(F32)
    gate, row_tok, runs, pos = _route(logits, n_tok)
    x_rows = h2[lax.optimization_barrier(row_tok)]
    y_rows = moe_experts(x_rows, runs, w_gate_up.reshape(w_gate_up.shape[1:]),
                         b_gate_up.reshape(b_gate_up.shape[1:]), w_down.reshape(w_down.shape[1:]),
                         b_down.reshape(b_down.shape[1:]), 1024)
    picked = y_rows[pos.T.reshape(-1)]
    y_p, y_s = combine_residual_norm(y1, picked, gate, norm_final_g, math.gcd(n_prow, n_srow, 256), n_prow)

    y_prompt = y_p.reshape(n_p, seq, d)
    y_sample = y_s.reshape(n_s, dseq, d)
    k_p = k_all[:n_prow].reshape(1, n_p, seq, H_A, HD_A)
    v_p = v_all[:n_prow].reshape(1, n_p, seq, H_A, HD_A)
    k_s = k_all[n_prow:].reshape(1, n_s, dseq, H_A, HD_A)
    v_s = v_all[n_prow:].reshape(1, n_s, dseq, H_A, HD_A)
    return (y_prompt, y_sample, k_p, v_p, s_p[None], conv_p[None],
            k_s, v_s, s_s[None], conv_s[None])
```
